```python
import math
import jax
import jax.numpy as jnp
from jax import lax
import numpy as np

D_MODEL = 1024
BATCH = 8
SEQ = 2048
DEPTH = 2
DEC_BATCH = 32
DEC_SEQ = 4
PAST_LEN = 8192
PAGE_SIZE = 128

HG_HEADS = 4
HG_DK = 128
HG_DV = 128
HG_CHUNK = 64
DA_HEADS = 4
DA_DK = 64
DA_DV = 2 * DA_DK
Q_BLOCK = 128
ROPE_THETA = 500000.0
ROT_DIM = DA_DK // 4
N_GROUPS = 4
EXPERTS_PER_GROUP = 4
N_EXPERTS = N_GROUPS * EXPERTS_PER_GROUP
TOP_K_IN_GROUP = 2
EXPERT_FF = D_MODEL // 2
NORM_EPS = 1e-6
NEG = -1e30
F_MIN = 1e-30

HG_OUT = HG_HEADS * HG_DV
DA_OUT = DA_HEADS * DA_DV
MIX_WIDTH = HG_OUT + DA_OUT
_W = [HG_HEADS * HG_DK, HG_HEADS * HG_DK, HG_HEADS * HG_DV, HG_HEADS * HG_DV,
      DA_HEADS * 2 * DA_DK, DA_HEADS * 2 * DA_DK, DA_HEADS * DA_DV]
IN_SPLITS = [int(v) for v in np.cumsum(_W)[:-1]]
IN_WIDTH = int(sum(_W))

kernel_name = "hymba_hgrn2_diffattn_hmoe_step"


def rms_norm(x, w):
    xf = x.astype(jnp.float32)
    y = xf * lax.rsqrt(jnp.mean(xf * xf, axis=-1, keepdims=True) + NORM_EPS)
    return (y * w.astype(jnp.float32)).astype(x.dtype)


def rope(x, pos):
    half = ROT_DIM // 2
    inv = ROPE_THETA ** (-jnp.arange(0, ROT_DIM, 2, dtype=jnp.float32) / ROT_DIM)
    ang = pos.astype(jnp.float32)[:, None] * inv[None, :]
    cos = jnp.cos(ang)[:, None, None, :]
    sin = jnp.sin(ang)[:, None, None, :]
    xr = x[..., :ROT_DIM].astype(jnp.float32)
    x1, x2 = xr[..., :half], xr[..., half:]
    rot = jnp.concatenate([x1 * cos - x2 * sin, x2 * cos + x1 * sin], axis=-1)
    return jnp.concatenate([rot.astype(x.dtype), x[..., ROT_DIM:]], axis=-1)


def project(xn, w_in, lb, pos):
    b, t, _ = xn.shape
    z = xn @ w_in
    hq, hf, hi, hg, dq, dk, dv = jnp.split(z, IN_SPLITS, axis=-1)
    f32 = jnp.float32
    q_h = hq.reshape(b, t, HG_HEADS, HG_DK).astype(f32) * HG_DK ** -0.5
    f_raw = hf.reshape(b, t, HG_HEADS, HG_DK).astype(f32)
    f = lb + (1.0 - lb) * jax.nn.sigmoid(f_raw)
    log_f = jnp.log(jnp.maximum(f, F_MIN))
    k_h = (1.0 - lb) * jax.nn.sigmoid(-f_raw)
    i_h = hi.reshape(b, t, HG_HEADS, HG_DV).astype(f32)
    q_d = rope(dq.reshape(b, t, DA_HEADS, 2, DA_DK), pos)
    k_d = rope(dk.reshape(b, t, DA_HEADS, 2, DA_DK), pos)
    v_d = dv.reshape(b, t, DA_HEADS, DA_DV)
    return q_h, k_h, i_h, log_f, hg, q_d, k_d, v_d


def hgrn2_recurrence(q, k, v, log_f, s0):
    b, t = q.shape[:2]
    c = HG_CHUNK if t % HG_CHUNK == 0 else t
    n = t // c

    def to_chunks(a):
        return a.reshape((b, n, c) + a.shape[2:]).swapaxes(0, 1)

    tri = jnp.tril(jnp.ones((c, c), dtype=bool))[None, :, :, None, None]

    def step(s, inp):
        qc, kc, vc, gc = inp
        g = jnp.cumsum(gc, axis=1)
        o_inter = jnp.einsum('bthk,bhkv->bthv', qc * jnp.exp(g), s)
        diff = g[:, :, None] - g[:, None, :]
        decay = jnp.where(tri, jnp.exp(jnp.minimum(diff, 0.0)), 0.0)
        a = jnp.einsum('bthk,btshk,bshk->bths', qc, decay, kc)
        o_intra = jnp.einsum('bths,bshv->bthv', a, vc)
        g_last = g[:, -1]
        s_new = jnp.exp(g_last)[..., None] * s + jnp.einsum(
            'bshk,bshv->bhkv', kc * jnp.exp(g_last[:, None] - g), vc)
        return s_new, o_inter + o_intra

    xs = (to_chunks(q), to_chunks(k), to_chunks(v), to_chunks(log_f))
    s_fin, o = lax.scan(step, s0, xs)
    return o.swapaxes(0, 1).reshape(b, t, HG_HEADS, HG_DV), s_fin


def diff_attn_core(q, qpos, segs, lam):
    scale = DA_DK ** -0.5
    scores = []
    for k, v, kpos in segs:
        s = jnp.einsum('bqhcd,bkhcd->bhcqk', q, k).astype(jnp.float32) * scale
        scores.append(jnp.where(kpos[None, :] <= qpos[:, None], s, NEG))
    p = jax.nn.softmax(jnp.concatenate(scores, axis=-1), axis=-1)
    a = p[:, :, 0] - lam * p[:, :, 1]
    outs = []
    off = 0
    for k, v, kpos in segs:
        nk = k.shape[1]
        outs.append(jnp.einsum('bhqk,bkhd->bqhd', a[..., off:off + nk], v.astype(jnp.float32)))
        off += nk
    return sum(outs)


def diff_attn_prompt(q, k, v, pos, lam):
    b, s = q.shape[:2]
    nb = s // Q_BLOCK
    qb = q.reshape(b, nb, Q_BLOCK, DA_HEADS, 2, DA_DK).swapaxes(0, 1)
    starts = jnp.arange(nb, dtype=jnp.int32) * Q_BLOCK

    def blk(args):
        qblk, start = args
        qpos = start + jnp.arange(Q_BLOCK, dtype=jnp.int32)
        return diff_attn_core(qblk, qpos, [(k, v, pos)], lam)

    out = lax.map(blk, (qb, starts))
    return out.swapaxes(0, 1).reshape(b, s, DA_HEADS, DA_DV)


def hier_moe(xn, w_r1, b_r1, w_r2, b_r2, w_gate, w_up, w_down):
    b, t, d = xn.shape
    xt = xn.reshape(-1, d)
    n = xt.shape[0]
    lg1 = (xt @ w_r1 + b_r1).astype(jnp.float32)
    p1 = jax.nn.softmax(lg1, axis=-1)
    grp = jnp.argmax(lg1, axis=-1)
    p_grp = jnp.take_along_axis(p1, grp[:, None], axis=-1)[:, 0]
    lg2 = (xt @ w_r2 + b_r2).astype(jnp.float32).reshape(n, N_GROUPS, EXPERTS_PER_GROUP)
    lg2g = jnp.take_along_axis(lg2, grp[:, None, None], axis=1)[:, 0]
    top_v, top_i = lax.top_k(lg2g, TOP_K_IN_GROUP)
    w2 = jax.nn.softmax(top_v, axis=-1)
    within = jnp.einsum('nk,nke->ne', w2, jax.nn.one_hot(top_i, EXPERTS_PER_GROUP, dtype=jnp.float32))
    gates = (p_grp[:, None, None] * jax.nn.one_hot(grp, N_GROUPS, dtype=jnp.float32)[:, :, None]
             * within[:, None, :]).reshape(n, N_EXPERTS).astype(xt.dtype)
    y = jnp.zeros_like(xt)
    for e in range(N_EXPERTS):
        h = jax.nn.silu(xt @ w_gate[e]) * (xt @ w_up[e])
        y = y + (h * gates[:, e:e + 1]) @ w_down[e]
    return y.reshape(b, t, d)


def trunk_layer(h, pos, s0, past, lam, lam_init, lb, attn_norm_w, w_in, hgrn_norm_w,
                diff_norm_w, w_o, ffn_norm_w, w_r1, b_r1, w_r2, b_r2, w_gate, w_up, w_down):
    b, t, _ = h.shape
    xn = rms_norm(h, attn_norm_w)
    q_h, k_h, i_h, log_f, g_h, q_d, k_d, v_d = project(xn, w_in, lb, pos)
    o_h, s_fin = hgrn2_recurrence(q_h, k_h, i_h, log_f, s0)
    if past is None:
        o_d = diff_attn_prompt(q_d, k_d, v_d, pos, lam)
    else:
        k_past, v_past, kpos_past = past
        o_d = diff_attn_core(q_d, pos, [(k_past, v_past, kpos_past), (k_d, v_d, pos)], lam)
    g = jax.nn.sigmoid(g_h.astype(jnp.float32)).reshape(b, t, HG_HEADS, HG_DV)
    oh = rms_norm(o_h, hgrn_norm_w) * g
    od = rms_norm(o_d, diff_norm_w) * (1.0 - lam_init)
    o = jnp.concatenate([oh.reshape(b, t, HG_OUT), od.reshape(b, t, DA_OUT)], axis=-1).astype(h.dtype)
    h = h + o @ w_o
    h = h + hier_moe(rms_norm(h, ffn_norm_w), w_r1, b_r1, w_r2, b_r2, w_gate, w_up, w_down)
    k_rows = k_d.reshape(b, t, DA_HEADS, 2 * DA_DK)
    return h, k_rows, v_d, s_fin


def setup_inputs(seed: int = 0) -> dict:
    key = jax.random.key(seed)
    ks = jax.random.split(key, 24)
    f32 = jnp.float32
    n_pages = PAST_LEN // PAGE_SIZE
    n_used = DEC_BATCH * n_pages
    n_pool = n_used + n_used // 4

    def nrm(k, shape, s):
        return jax.random.normal(k, shape, f32) * s

    page_table = jax.random.permutation(ks[5], n_pool)[:n_used].reshape(DEC_BATCH, n_pages).astype(jnp.int32)
    return {
        "x_prompt": nrm(ks[0], (BATCH, SEQ, D_MODEL), 1.0),
        "x_sample": nrm(ks[1], (DEC_BATCH, DEC_SEQ, D_MODEL), 1.0),
        "cache_k": nrm(ks[2], (DEPTH, n_pool, PAGE_SIZE, DA_HEADS, 2 * DA_DK), 1.0),
        "cache_v": nrm(ks[3], (DEPTH, n_pool, PAGE_SIZE, DA_HEADS, DA_DV), 1.0),
        "state_hgrn": nrm(ks[4], (DEPTH, DEC_BATCH, HG_HEADS, HG_DK, HG_DV), 0.5),
        "page_table": page_table,
        "attn_norm_w": 1.0 + nrm(ks[6], (DEPTH, D_MODEL), 0.02),
        "w_in": nrm(ks[7], (DEPTH, D_MODEL, IN_WIDTH), D_MODEL ** -0.5),
        "hgrn_lb": nrm(ks[8], (DEPTH, HG_HEADS * HG_DK), 0.5),
        "hgrn_norm_w": 1.0 + nrm(ks[9], (DEPTH, HG_DV), 0.02),
        "diff_lambda": nrm(ks[10], (DEPTH, 4, DA_DK), 0.1),
        "diff_norm_w": 1.0 + nrm(ks[11], (DEPTH, DA_DV), 0.02),
        "w_o": nrm(ks[12], (DEPTH, MIX_WIDTH, D_MODEL), MIX_WIDTH ** -0.5),
        "ffn_norm_w": 1.0 + nrm(ks[13], (DEPTH, D_MODEL), 0.02),
        "w_r1": nrm(ks[14], (DEPTH, D_MODEL, N_GROUPS), D_MODEL ** -0.5),
        "b_r1": nrm(ks[15], (DEPTH, N_GROUPS), 0.01),
        "w_r2": nrm(ks[16], (DEPTH, D_MODEL, N_EXPERTS), D_MODEL ** -0.5),
        "b_r2": nrm(ks[17], (DEPTH, N_EXPERTS), 0.01),
        "w_gate": nrm(ks[18], (DEPTH, N_EXPERTS, D_MODEL, EXPERT_FF), D_MODEL ** -0.5),
        "w_up": nrm(ks[19], (DEPTH, N_EXPERTS, D_MODEL, EXPERT_FF), D_MODEL ** -0.5),
        "w_down": nrm(ks[20], (DEPTH, N_EXPERTS, EXPERT_FF, D_MODEL), EXPERT_FF ** -0.5),
        "final_norm_w": 1.0 + nrm(ks[21], (D_MODEL,), 0.02),
    }


def reference(x_prompt, x_sample, cache_k, cache_v, state_hgrn, page_table, attn_norm_w, w_in,
              hgrn_lb, hgrn_norm_w, diff_lambda, diff_norm_w, w_o, ffn_norm_w, w_r1, b_r1,
              w_r2, b_r2, w_gate, w_up, w_down, final_norm_w):
    b, s = x_prompt.shape[:2]
    bd, t = x_sample.shape[:2]
    n_pages = page_table.shape[1]
    past_len = n_pages * PAGE_SIZE
    pos_p = jnp.arange(s, dtype=jnp.int32)
    pos_s = past_len + jnp.arange(t, dtype=jnp.int32)
    pos_past = jnp.arange(past_len, dtype=jnp.int32)
    p_lb = jax.nn.softmax(hgrn_lb.astype(jnp.float32), axis=0)
    lb_all = jnp.cumsum(p_lb, axis=0) - p_lb[0:1]

    hp, hs = x_prompt, x_sample
    kp_l, vp_l, sp_l, ks_l, vs_l, ss_l = [], [], [], [], [], []
    for l in range(DEPTH):
        lam_init = 0.8 - 0.6 * math.exp(-0.3 * l)
        dl = diff_lambda[l].astype(jnp.float32)
        lam = jnp.exp(jnp.sum(dl[0] * dl[1])) - jnp.exp(jnp.sum(dl[2] * dl[3])) + lam_init
        lb = lb_all[l].reshape(HG_HEADS, HG_DK)
        wl = (attn_norm_w[l], w_in[l], hgrn_norm_w[l], diff_norm_w[l], w_o[l], ffn_norm_w[l],
              w_r1[l], b_r1[l], w_r2[l], b_r2[l], w_gate[l], w_up[l], w_down[l])
        s0_p = jnp.zeros((b, HG_HEADS, HG_DK, HG_DV), jnp.float32)
        hp, k_rows, v_rows, s_fin = trunk_layer(hp, pos_p, s0_p, None, lam, lam_init, lb, *wl)
        kp_l.append(k_rows)
        vp_l.append(v_rows)
        sp_l.append(s_fin)
        k_past = cache_k[l][page_table].reshape(bd, past_len, DA_HEADS, 2, DA_DK)
        v_past = cache_v[l][page_table].reshape(bd, past_len, DA_HEADS, DA_DV)
        s0_s = state_hgrn[l].astype(jnp.float32)
        hs, k_rows, v_rows, s_fin = trunk_layer(hs, pos_s, s0_s, (k_past, v_past, pos_past),
                                                lam, lam_init, lb, *wl)
        ks_l.append(k_rows)
        vs_l.append(v_rows)
        ss_l.append(s_fin)

    y_prompt = rms_norm(hp, final_norm_w)
    y_sample = rms_norm(hs, final_norm_w)
    return (y_prompt, y_sample, jnp.stack(kp_l), jnp.stack(vp_l), jnp.stack(sp_l),
            jnp.stack(ks_l), jnp.stack(vs_l), jnp.stack(ss_l))
```

```python
import functools
import math

import jax
import jax.numpy as jnp
from jax import lax
from jax.experimental import pallas as pl
from jax.experimental.pallas import tpu as pltpu

f32 = jnp.float32
bf16 = jnp.bfloat16

D_MODEL = 1024
N_HEADS = 4
HEAD_W = 128
SEG_W = N_HEADS * HEAD_W
N_SEG = 7
DA_DK = 64
ROT_DIM = DA_DK // 4
ROPE_THETA = 500000.0
PAGE_SIZE = 128
N_GROUPS = 4
EXPERTS_PER_GROUP = 4
N_EXPERTS = N_GROUPS * EXPERTS_PER_GROUP
EXPERT_FF = D_MODEL // 2
NORM_EPS = 1e-6
NEG = -1e30
F_MIN = 1e-30

HG_CHUNK = 64
HG_SUB = 16
SAMPLE_PAD = 8
GATE_LANE0 = N_GROUPS
ROUTER_W = 128
ROW_W = D_MODEL + ROUTER_W

VMEM_LIMIT = 48 * 1024 * 1024


def _dot(a, b):
    return jnp.dot(a, b, preferred_element_type=f32)


def _dot_nt(a, b):
    return lax.dot_general(a, b, (((1,), (1,)), ((), ())), preferred_element_type=f32)


def _dot_tn(a, b):
    return lax.dot_general(a, b, (((0,), (0,)), ((), ())), preferred_element_type=f32)


def _rms(x, w):
    return x * lax.rsqrt(jnp.mean(x * x, axis=-1, keepdims=True) + NORM_EPS) * w


def _split3(x):
    hi = x.astype(bf16)
    r1 = x - hi.astype(f32)
    mid = r1.astype(bf16)
    lo = (r1 - mid.astype(f32)).astype(bf16)
    return hi, mid, lo


def _proj_kernel(h_ref, nw_ref, w_ref, lb_ref, c_ref, sa_ref, sb_ref,
                 qh_ref, kh_ref, ih_ref, lf_ref, gh_ref, qd_ref, kd_ref, vd_ref, kb_ref, vb_ref):
    xn = _rms(h_ref[...], nw_ref[...]).astype(bf16)

    def seg(i):
        return _dot(xn, w_ref[:, i * SEG_W:(i + 1) * SEG_W])

    qh_ref[...] = seg(0) * (HEAD_W ** -0.5)
    hf = seg(1)
    lb = lb_ref[...]
    e = jnp.exp(-jnp.abs(hf))
    r = 1.0 / (1.0 + e)
    pos = hf >= 0.0
    sig = jnp.where(pos, r, e * r)
    nsig = jnp.where(pos, e * r, r)
    f = lb + (1.0 - lb) * sig
    lf_ref[...] = jnp.log(jnp.maximum(f, F_MIN))
    kh_ref[...] = (1.0 - lb) * nsig
    ih_ref[...] = seg(2)
    gh_ref[...] = seg(3)

    c = c_ref[...]
    sa = sa_ref[...]
    sb = sb_ref[...]

    def rope(z, hh):
        zz = z[:, hh * HEAD_W:(hh + 1) * HEAD_W]
        return zz * c + pltpu.roll(zz, HEAD_W - ROT_DIM // 2, 1) * sa + pltpu.roll(zz, ROT_DIM // 2, 1) * sb

    zq = seg(4)
    zk = seg(5)
    for hh in range(N_HEADS):
        sl = slice(hh * HEAD_W, (hh + 1) * HEAD_W)
        qd_ref[:, sl] = (rope(zq, hh) * (DA_DK ** -0.5)).astype(bf16)
        kr = rope(zk, hh)
        kd_ref[:, sl] = kr
        kb_ref[:, sl] = kr.astype(bf16)
    vd = seg(6)
    vd_ref[...] = vd
    vb_ref[...] = vd.astype(bf16)


def _proj(h, nw, w_bf, lb, tabs, tm):
    n = h.shape[0]
    npos = tabs[0].shape[0] // tm
    row = lambda i: (i, 0)
    fixed = lambda i: (0, 0)
    tab = lambda i: (i % npos, 0)
    seg_f32 = jax.ShapeDtypeStruct((n, SEG_W), f32)
    seg_bf = jax.ShapeDtypeStruct((n, SEG_W), bf16)
    seg_spec = pl.BlockSpec((tm, SEG_W), row)
    return pl.pallas_call(
        _proj_kernel,
        grid=(n // tm,),
        in_specs=[pl.BlockSpec((tm, D_MODEL), row),
                  pl.BlockSpec((1, D_MODEL), fixed),
                  pl.BlockSpec((D_MODEL, N_SEG * SEG_W), fixed),
                  pl.BlockSpec((1, SEG_W), fixed),
                  pl.BlockSpec((tm, HEAD_W), tab),
                  pl.BlockSpec((tm, HEAD_W), tab),
                  pl.BlockSpec((tm, HEAD_W), tab)],
        out_specs=[seg_spec] * 10,
        out_shape=[seg_f32] * 5 + [seg_bf, seg_f32, seg_f32, seg_bf, seg_bf],
        compiler_params=pltpu.CompilerParams(dimension_semantics=("arbitrary",), vmem_limit_bytes=VMEM_LIMIT),
        name="proj",
    )(h, nw, w_bf, lb, *tabs)


def _rope_tables(pos):
    half = ROT_DIM // 2
    inv = ROPE_THETA ** (-jnp.arange(0, ROT_DIM, 2, dtype=f32) / ROT_DIM)
    ang = pos.astype(f32)[:, None] * inv[None, :]
    cos, sin = jnp.cos(ang), jnp.sin(ang)
    t = pos.shape[0]
    rest = DA_DK - ROT_DIM
    c64 = jnp.concatenate([cos, cos, jnp.ones((t, rest), f32)], axis=1)
    sa64 = jnp.concatenate([-sin, jnp.zeros((t, half + rest), f32)], axis=1)
    sb64 = jnp.concatenate([jnp.zeros((t, half), f32), sin, jnp.zeros((t, rest), f32)], axis=1)
    return tuple(jnp.tile(a, (1, HEAD_W // DA_DK)) for a in (c64, sa64, sb64))


def _hgrn_chunk(q, k, v, lf, st, chunk, sub):
    r_i = lax.broadcasted_iota(jnp.int32, (chunk, chunk), 0)
    c_i = lax.broadcasted_iota(jnp.int32, (chunk, chunk), 1)
    tri = jnp.where(c_i <= r_i, 1.0, 0.0).astype(bf16)
    hi, mid, lo = _split3(lf)
    g = _dot(tri, hi) + _dot(tri, mid) + _dot(tri, lo)
    g_last = g[chunk - 1:chunk, :]
    o_inter = _dot_nt((q * jnp.exp(g)).astype(bf16), st.astype(bf16))
    v_bf = v.astype(bf16)
    ones = jnp.ones((HEAD_W, HEAD_W), bf16)
    row_s = lax.broadcasted_iota(jnp.int32, (sub, HEAD_W), 0)
    outs = []
    for i in range(chunk // sub):
        r0 = i * sub
        gi, qi, ki, vi = g[r0:r0 + sub], q[r0:r0 + sub], k[r0:r0 + sub], v[r0:r0 + sub]
        acc = o_inter[r0:r0 + sub]
        if i > 0:
            b = g[r0 - 1:r0, :]
            qt = (qi * jnp.exp(gi - b)).astype(bf16)
            kt = (k[0:r0] * jnp.exp(b - g[0:r0])).astype(bf16)
            a = _dot_nt(qt, kt)
            acc = acc + _dot(a.astype(bf16), v_bf[0:r0])
        ps = []
        for s in range(sub):
            d = gi - gi[s:s + 1, :]
            e = jnp.where(row_s >= s, jnp.exp(jnp.minimum(d, 0.0)), 0.0)
            ps.append(qi * e * ki[s:s + 1, :])
        a_rep = _dot(jnp.concatenate(ps, axis=0).astype(bf16), ones)
        for s in range(sub):
            acc = acc + a_rep[s * sub:(s + 1) * sub] * vi[s:s + 1, :]
        outs.append(acc)
    o = outs[0] if len(outs) == 1 else jnp.concatenate(outs, axis=0)
    kdec = (k * jnp.exp(g_last - g)).astype(bf16)
    st_new = st * jnp.exp(g_last) + _dot_tn(v_bf, kdec)
    return o, st_new


def _hgrn_kernel(q_ref, k_ref, v_ref, lf_ref, s0_ref, o_ref, sfin_ref, st_ref, *, chunk, sub, n_chunks):
    j = pl.program_id(1)

    @pl.when(j == 0)
    def _():
        for hh in range(N_HEADS):
            st_ref[hh] = s0_ref[hh].T

    def body(c, carry):
        r = pl.multiple_of(c * chunk, chunk)
        for hh in range(N_HEADS):
            sl = slice(hh * HEAD_W, (hh + 1) * HEAD_W)
            o, st_new = _hgrn_chunk(q_ref[pl.ds(r, chunk), sl], k_ref[pl.ds(r, chunk), sl],
                                    v_ref[pl.ds(r, chunk), sl], lf_ref[pl.ds(r, chunk), sl],
                                    st_ref[hh], chunk, sub)
            o_ref[pl.ds(r, chunk), sl] = o
            st_ref[hh] = st_new
        return carry

    lax.fori_loop(0, n_chunks, body, 0)

    @pl.when(j == pl.num_programs(1) - 1)
    def _():
        for hh in range(N_HEADS):
            sfin_ref[hh] = st_ref[hh].T


def _hgrn(q, k, v, lf, s0, t, tb, chunk, sub):
    n = q.shape[0]
    b = n // t
    nj = t // tb
    tok = pl.BlockSpec((tb, SEG_W), lambda bi, j: (bi * nj + j, 0))
    st = pl.BlockSpec((None, N_HEADS, HEAD_W, HEAD_W), lambda bi, j: (bi, 0, 0, 0))
    return pl.pallas_call(
        functools.partial(_hgrn_kernel, chunk=chunk, sub=sub, n_chunks=tb // chunk),
        grid=(b, nj),
        in_specs=[tok, tok, tok, tok, st],
        out_specs=[tok, st],
        out_shape=[jax.ShapeDtypeStruct((n, SEG_W), f32), jax.ShapeDtypeStruct(s0.shape, f32)],
        scratch_shapes=[pltpu.VMEM((N_HEADS, HEAD_W, HEAD_W), f32)],
        compiler_params=pltpu.CompilerParams(dimension_semantics=("arbitrary", "arbitrary"),
                                             vmem_limit_bytes=VMEM_LIMIT),
        name="hgrn",
    )(q, k, v, lf, s0)


def _stack_maps(q):
    lane = lax.broadcasted_iota(jnp.int32, q.shape, 1)
    zero = jnp.zeros_like(q)
    return jnp.concatenate([jnp.where(lane < DA_DK, q, zero), jnp.where(lane >= DA_DK, q, zero)], axis=0)


def _softmax_step(s, v_bf, m_ref, l_ref, acc_ref, rows=None):
    sl = slice(None) if rows is None else rows
    m_prev = m_ref[sl, :]
    m_new = jnp.maximum(m_prev, jnp.max(s, axis=-1, keepdims=True))
    alpha = jnp.exp(m_prev - m_new)
    p = jnp.exp(s - m_new)
    l_ref[sl, :] = alpha * l_ref[sl, :] + jnp.sum(p, axis=-1, keepdims=True)
    acc_ref[sl, :] = alpha * acc_ref[sl, :] + _dot(p.astype(bf16), v_bf)
    m_ref[sl, :] = m_new


def _attn_kernel(lam_ref, q_ref, k_ref, v_ref, o_ref, m_ref, l_ref, acc_ref, *, tq):
    i = pl.program_id(2)
    qs = _stack_maps(q_ref[...])
    m_ref[...] = jnp.full(m_ref.shape, NEG, f32)
    l_ref[...] = jnp.zeros(l_ref.shape, f32)
    acc_ref[...] = jnp.zeros(acc_ref.shape, f32)

    def step(j, masked):
        r = pl.multiple_of(j * tq, tq)
        s = _dot_nt(qs, k_ref[pl.ds(r, tq), :])
        if masked:
            row = lax.broadcasted_iota(jnp.int32, s.shape, 0) & (tq - 1)
            col = lax.broadcasted_iota(jnp.int32, s.shape, 1)
            s = jnp.where(col <= row, s, NEG)
        _softmax_step(s, v_ref[pl.ds(r, tq), :], m_ref, l_ref, acc_ref)

    def body(j, carry):
        step(j, False)
        return carry

    lax.fori_loop(0, i, body, 0)
    step(i, True)
    o = acc_ref[...] / l_ref[...]
    o_ref[...] = o[:tq] - lam_ref[0] * o[tq:]


def _attn_prompt(lam, q_bf, k_bf, v_bf, b, s, tq):
    q3, k3, v3 = (a.reshape(b, s, SEG_W) for a in (q_bf, k_bf, v_bf))
    qspec = pl.BlockSpec((None, tq, HEAD_W), lambda bi, h, i: (bi, i, h))
    kspec = pl.BlockSpec((None, s, HEAD_W), lambda bi, h, i: (bi, 0, h))
    out = pl.pallas_call(
        functools.partial(_attn_kernel, tq=tq),
        grid=(b, N_HEADS, s // tq),
        in_specs=[pl.BlockSpec(memory_space=pltpu.SMEM), qspec, kspec, kspec],
        out_specs=qspec,
        out_shape=jax.ShapeDtypeStruct((b, s, SEG_W), f32),
        scratch_shapes=[pltpu.VMEM((2 * tq, 1), f32), pltpu.VMEM((2 * tq, 1), f32),
                        pltpu.VMEM((2 * tq, HEAD_W), f32)],
        compiler_params=pltpu.CompilerParams(dimension_semantics=("arbitrary",) * 3, vmem_limit_bytes=VMEM_LIMIT),
        name="attn_prompt",
    )(lam, q3, k3, v3)
    return out.reshape(b * s, SEG_W)


def _attn_sample_kernel(pt_ref, lam_ref, qs_ref, kn_ref, vn_ref, *rest, n_pg, layer):
    del pt_ref, layer
    k_refs = rest[:n_pg]
    v_refs = rest[n_pg:2 * n_pg]
    o_ref, m_ref, l_ref, acc_ref = rest[2 * n_pg:]
    j = pl.program_id(1)
    t = SAMPLE_PAD

    @pl.when(j == 0)
    def _():
        m_ref[...] = jnp.full(m_ref.shape, NEG, f32)
        l_ref[...] = jnp.zeros(l_ref.shape, f32)
        acc_ref[...] = jnp.zeros(acc_ref.shape, f32)

    for hh in range(N_HEADS):
        rows = slice(hh * 2 * t, (hh + 1) * 2 * t)
        q = qs_ref[rows, :]
        kcat = jnp.concatenate([kr[:, hh, :].astype(bf16) for kr in k_refs], axis=0)
        vcat = jnp.concatenate([vr[:, hh, :].astype(bf16) for vr in v_refs], axis=0)
        _softmax_step(_dot_nt(q, kcat), vcat, m_ref, l_ref, acc_ref, rows)

    @pl.when(j == pl.num_programs(1) - 1)
    def _():
        for hh in range(N_HEADS):
            rows = slice(hh * 2 * t, (hh + 1) * 2 * t)
            sl = slice(hh * HEAD_W, (hh + 1) * HEAD_W)
            s = _dot_nt(qs_ref[rows, :], kn_ref[:, sl].astype(bf16))
            row = lax.broadcasted_iota(jnp.int32, s.shape, 0) & (t - 1)
            col = lax.broadcasted_iota(jnp.int32, s.shape, 1)
            s = jnp.where(col <= row, s, NEG)
            _softmax_step(s, vn_ref[:, sl].astype(bf16), m_ref, l_ref, acc_ref, rows)
            o = acc_ref[rows, :] / l_ref[rows, :]
            o_ref[:, sl] = o[:t] - lam_ref[0] * o[t:]


def _attn_sample(page_table, lam, qs, kn, vn, cache_k, cache_v, layer, n_pg):
    bd, n_pages = page_table.shape
    t = SAMPLE_PAD

    def page_spec(p):
        return pl.BlockSpec((None, None, PAGE_SIZE, N_HEADS, HEAD_W),
                            lambda bi, j, pt: (layer, pt[bi * n_pages + j * n_pg + p], 0, 0, 0))

    per_b = lambda shape: pl.BlockSpec((None,) + shape, lambda bi, j, pt: (bi, 0, 0))
    pages = [page_spec(p) for p in range(n_pg)]
    return pl.pallas_call(
        functools.partial(_attn_sample_kernel, n_pg=n_pg, layer=layer),
        grid_spec=pltpu.PrefetchScalarGridSpec(
            num_scalar_prefetch=1,
            grid=(bd, n_pages // n_pg),
            in_specs=[pl.BlockSpec(memory_space=pltpu.SMEM), per_b((2 * t * N_HEADS, HEAD_W)),
                      per_b((t, SEG_W)), per_b((t, SEG_W))] + pages + pages,
            out_specs=per_b((t, SEG_W)),
            scratch_shapes=[pltpu.VMEM((2 * t * N_HEADS, 1), f32), pltpu.VMEM((2 * t * N_HEADS, 1), f32),
                            pltpu.VMEM((2 * t * N_HEADS, HEAD_W), f32)]),
        out_shape=jax.ShapeDtypeStruct((bd, t, SEG_W), f32),
        compiler_params=pltpu.CompilerParams(dimension_semantics=("arbitrary", "arbitrary"),
                                             vmem_limit_bytes=VMEM_LIMIT),
        name="attn_sample",
    )(page_table.reshape(-1), lam, qs, kn, vn, *([cache_k] * n_pg), *([cache_v] * n_pg))


def _route(logits):
    lane = lax.broadcasted_iota(jnp.int32, logits.shape, 1).astype(f32)
    big = float(ROUTER_W)
    is_g = lane < N_GROUPS
    m1 = jnp.max(jnp.where(is_g, logits, -jnp.inf), axis=-1, keepdims=True)
    grp = jnp.min(jnp.where(is_g & (logits == m1), lane, big), axis=-1, keepdims=True)
    p_grp = 1.0 / jnp.sum(jnp.where(is_g, jnp.exp(logits - m1), 0.0), axis=-1, keepdims=True)
    lo = GATE_LANE0 + EXPERTS_PER_GROUP * grp
    in_g = (lane >= lo) & (lane < lo + EXPERTS_PER_GROUP)
    v1 = jnp.max(jnp.where(in_g, logits, -jnp.inf), axis=-1, keepdims=True)
    i1 = jnp.min(jnp.where(in_g & (logits == v1), lane, big), axis=-1, keepdims=True)
    rest = in_g & (lane != i1)
    v2 = jnp.max(jnp.where(rest, logits, -jnp.inf), axis=-1, keepdims=True)
    i2 = jnp.min(jnp.where(rest & (logits == v2), lane, big), axis=-1, keepdims=True)
    e = jnp.exp(v2 - v1)
    w1 = 1.0 / (1.0 + e)
    w2 = e / (1.0 + e)
    gates = jnp.where(lane == i1, p_grp * w1, 0.0) + jnp.where(lane == i2, p_grp * w2, 0.0)
    return jnp.where(lane == 0.0, grp, gates)


def _mix_kernel(oh_ref, od_ref, gh_ref, h_ref, hnw_ref, dnw_ref, wo_ref, fnw_ref, wrh_ref, wrl_ref, br_ref,
                hx_ref, *, od_scale):
    parts = []
    for hh in range(N_HEADS):
        sl = slice(hh * HEAD_W, (hh + 1) * HEAD_W)
        gate = 1.0 / (1.0 + jnp.exp(-gh_ref[:, sl]))
        parts.append((_rms(oh_ref[:, sl], hnw_ref[...]) * gate).astype(bf16))
    for hh in range(N_HEADS):
        sl = slice(hh * HEAD_W, (hh + 1) * HEAD_W)
        parts.append((_rms(od_ref[:, sl], dnw_ref[...]) * od_scale).astype(bf16))
    h2 = h_ref[...] + _dot(jnp.concatenate(parts, axis=1), wo_ref[...])
    xn = _rms(h2, fnw_ref[...])
    x_hi = xn.astype(bf16)
    x_lo = (xn - x_hi.astype(f32)).astype(bf16)
    logits = _dot(x_hi, wrh_ref[...]) + _dot(x_lo, wrh_ref[...]) + _dot(x_hi, wrl_ref[...]) + br_ref[...]
    hx_ref[:, :D_MODEL] = h2
    hx_ref[:, D_MODEL:] = _route(logits)


def _mix(o_h, o_d, g_h, h, hnw, dnw, wo_bf, fnw, wr_hi, wr_lo, br, od_scale, tm):
    n = h.shape[0]
    row = lambda i: (i, 0)
    fixed = lambda i: (0, 0)
    seg = pl.BlockSpec((tm, SEG_W), row)
    return pl.pallas_call(
        functools.partial(_mix_kernel, od_scale=od_scale),
        grid=(n // tm,),
        in_specs=[seg, seg, seg, pl.BlockSpec((tm, D_MODEL), row),
                  pl.BlockSpec((1, HEAD_W), fixed), pl.BlockSpec((1, HEAD_W), fixed),
                  pl.BlockSpec((D_MODEL, D_MODEL), fixed), pl.BlockSpec((1, D_MODEL), fixed),
                  pl.BlockSpec((D_MODEL, ROUTER_W), fixed), pl.BlockSpec((D_MODEL, ROUTER_W), fixed),
                  pl.BlockSpec((1, ROUTER_W), fixed)],
        out_specs=pl.BlockSpec((tm, ROW_W), row),
        out_shape=jax.ShapeDtypeStruct((n, ROW_W), f32),
        compiler_params=pltpu.CompilerParams(dimension_semantics=("arbitrary",), vmem_limit_bytes=VMEM_LIMIT),
        name="mix",
    )(o_h, o_d, g_h, h, hnw, dnw, wo_bf, fnw, wr_hi, wr_lo, br)


def _moe_kernel(ids_ref, grp_ref, cnt_ref, hx_hbm, fnw_ref, onw_ref, wg_ref, wu_ref, wd_ref, out_hbm,
                xbuf, ybuf, gsem, ssem, *, tm, final_norm):
    i = pl.program_id(0)
    n_tiles = pl.num_programs(0)
    slot = i % 2

    def gather_copy(tile, sl, r):
        tok = ids_ref[tile * tm + r]
        return pltpu.make_async_copy(hx_hbm.at[pl.ds(tok, 1)], xbuf.at[sl, pl.ds(r, 1)], gsem.at[sl])

    def scatter_copy(tile, sl, r):
        tok = ids_ref[tile * tm + r]
        return pltpu.make_async_copy(ybuf.at[sl, pl.ds(r, 1)], out_hbm.at[pl.ds(tok, 1)], ssem.at[sl])

    def for_rows(n, fn):
        def body(r, carry):
            fn(r)
            return carry
        lax.fori_loop(0, n, body, 0)

    @pl.when(i == 0)
    def _():
        for_rows(tm, lambda r: gather_copy(0, 0, r).start())

    @pl.when(i + 1 < n_tiles)
    def _():
        for_rows(tm, lambda r: gather_copy(i + 1, 1 - slot, r).start())

    for_rows(tm, lambda r: gather_copy(i, slot, r).wait())

    @pl.when(i >= 2)
    def _():
        for_rows(cnt_ref[i - 2], lambda r: scatter_copy(i - 2, slot, r).wait())

    xb = xbuf[slot]
    h2 = xb[:, :D_MODEL]
    router = xb[:, D_MODEL:]
    xn = _rms(h2, fnw_ref[...]).astype(bf16)
    lane = lax.broadcasted_iota(jnp.int32, router.shape, 1)
    lane0 = GATE_LANE0 + EXPERTS_PER_GROUP * grp_ref[i]
    acc = h2
    for e in range(EXPERTS_PER_GROUP):
        gate = jnp.sum(jnp.where(lane == lane0 + e, router, 0.0), axis=-1, keepdims=True)
        a = _dot(xn, wg_ref[e])
        hid = a / (1.0 + jnp.exp(-a)) * _dot(xn, wu_ref[e])
        acc = acc + _dot((hid * gate).astype(bf16), wd_ref[e])
    if final_norm:
        acc = _rms(acc, onw_ref[...])
    ybuf[slot] = acc

    for_rows(cnt_ref[i], lambda r: scatter_copy(i, slot, r).start())

    @pl.when(i == n_tiles - 1)
    def _():
        @pl.when(i >= 1)
        def _():
            for_rows(cnt_ref[i - 1], lambda r: scatter_copy(i - 1, 1 - slot, r).wait())
        for_rows(cnt_ref[i], lambda r: scatter_copy(i, slot, r).wait())


def _dispatch(hx, tm):
    n = hx.shape[0]
    n_tiles = n // tm + N_GROUPS
    grp = hx[:, D_MODEL].astype(jnp.int32)
    onehot = (grp[:, None] == jnp.arange(N_GROUPS, dtype=jnp.int32)[None, :]).astype(jnp.int32)
    csum = jnp.cumsum(onehot, axis=0)
    cnt = csum[-1]
    rank = jnp.sum(csum * onehot, axis=1) - 1
    tiles_g = (cnt + tm - 1) // tm
    tile_end = jnp.cumsum(tiles_g)
    tile_start = tile_end - tiles_g
    pos = jnp.sum(onehot * (tile_start * tm)[None, :], axis=1) + rank
    ids = jnp.zeros((n_tiles * tm,), jnp.int32).at[pos].set(jnp.arange(n, dtype=jnp.int32))
    tile = jnp.arange(n_tiles, dtype=jnp.int32)
    tile_grp_raw = jnp.sum((tile[:, None] >= tile_end[None, :]).astype(jnp.int32), axis=1)
    last_grp = jnp.max(jnp.where(cnt > 0, jnp.arange(N_GROUPS, dtype=jnp.int32), 0))
    tile_grp = jnp.minimum(tile_grp_raw, last_grp)
    in_range = tile_grp_raw < N_GROUPS
    rows_left = cnt[tile_grp] - (tile - tile_start[tile_grp]) * tm
    tile_cnt = jnp.where(in_range, jnp.clip(rows_left, 0, tm), 0).astype(jnp.int32)
    return ids, tile_grp, tile_cnt


def _moe(hx, fnw, onw, wg_bf, wu_bf, wd_bf, layer, tm, final_norm):
    n = hx.shape[0]
    ids, tile_grp, tile_cnt = _dispatch(hx, tm)
    n_tiles = tile_grp.shape[0]
    fixed = lambda i, ids_r, grp_r, cnt_r: (0, 0)
    wspec = lambda shape: pl.BlockSpec((None, EXPERTS_PER_GROUP) + shape,
                                       lambda i, ids_r, grp_r, cnt_r: (layer, grp_r[i], 0, 0))
    return pl.pallas_call(
        functools.partial(_moe_kernel, tm=tm, final_norm=final_norm),
        grid_spec=pltpu.PrefetchScalarGridSpec(
            num_scalar_prefetch=3,
            grid=(n_tiles,),
            in_specs=[pl.BlockSpec(memory_space=pl.ANY),
                      pl.BlockSpec((1, D_MODEL), fixed), pl.BlockSpec((1, D_MODEL), fixed),
                      wspec((D_MODEL, EXPERT_FF)), wspec((D_MODEL, EXPERT_FF)), wspec((EXPERT_FF, D_MODEL))],
            out_specs=pl.BlockSpec(memory_space=pl.ANY),
            scratch_shapes=[pltpu.VMEM((2, tm, ROW_W), f32), pltpu.VMEM((2, tm, D_MODEL), f32),
                            pltpu.SemaphoreType.DMA((2,)), pltpu.SemaphoreType.DMA((2,))]),
        out_shape=jax.ShapeDtypeStruct((n, D_MODEL), f32),
        compiler_params=pltpu.CompilerParams(dimension_semantics=("arbitrary",), vmem_limit_bytes=VMEM_LIMIT),
        name="moe",
    )(ids, tile_grp, tile_cnt, hx, fnw, onw, wg_bf, wu_bf, wd_bf)


def _stacked_sample_queries(qd):
    bd, t = qd.shape[:2]
    q5 = qd.reshape(bd, t, N_HEADS, 2, DA_DK)
    eye = jnp.eye(2, dtype=qd.dtype)
    qz = q5[:, :, :, :, None, :] * eye[None, None, None, :, :, None]
    return qz.transpose(0, 2, 3, 1, 4, 5).reshape(bd, N_HEADS * 2 * t, HEAD_W)


def kernel(x_prompt, x_sample, cache_k, cache_v, state_hgrn, page_table, attn_norm_w, w_in, hgrn_lb, hgrn_norm_w,
           diff_lambda, diff_norm_w, w_o, ffn_norm_w, w_r1, b_r1, w_r2, b_r2, w_gate, w_up, w_down, final_norm_w):
    b, s = x_prompt.shape[:2]
    bd, t = x_sample.shape[:2]
    depth = w_in.shape[0]
    n_pages = page_table.shape[1]
    past_len = n_pages * PAGE_SIZE
    assert x_prompt.shape[2] == D_MODEL and w_in.shape[2] == N_SEG * SEG_W and t <= SAMPLE_PAD
    tp = SAMPLE_PAD

    tabs_p = _rope_tables(jnp.arange(s, dtype=jnp.int32))
    tabs_s = _rope_tables(jnp.tile(past_len + jnp.arange(t, dtype=jnp.int32), bd))
    p_lb = jax.nn.softmax(hgrn_lb.astype(f32), axis=0)
    lb_all = jnp.cumsum(p_lb, axis=0) - p_lb[0:1]

    w_in_bf = w_in.astype(bf16)
    w_o_bf = w_o.astype(bf16)
    wg_bf, wu_bf, wd_bf = w_gate.astype(bf16), w_up.astype(bf16), w_down.astype(bf16)
    pad_r = ROUTER_W - N_GROUPS - N_EXPERTS
    w_r = jnp.concatenate([w_r1, w_r2, jnp.zeros((depth, D_MODEL, pad_r), f32)], axis=2)
    w_r_hi = w_r.astype(bf16)
    w_r_lo = (w_r - w_r_hi.astype(f32)).astype(bf16)
    b_r = jnp.concatenate([b_r1, b_r2, jnp.zeros((depth, pad_r), f32)], axis=1)

    tm_p = 256 if (b * s) % 256 == 0 else b * s
    tq = 256 if s % 256 == 0 else s
    tb_p = 256 if s % 256 == 0 else s
    tm_moe_p = 512 if (b * s) % 512 == 0 else b * s
    n_s = bd * t
    n_pg = 8 if n_pages % 8 == 0 else 1

    hp = x_prompt.reshape(b * s, D_MODEL)
    hs = x_sample.reshape(n_s, D_MODEL)
    zeros_state = jnp.zeros((b, N_HEADS, HEAD_W, HEAD_W), f32)
    kp_l, vp_l, sp_l, ks_l, vs_l, ss_l = [], [], [], [], [], []
    for l in range(depth):
        lam_init = 0.8 - 0.6 * math.exp(-0.3 * l)
        dl = diff_lambda[l].astype(f32)
        lam = (jnp.exp(jnp.sum(dl[0] * dl[1])) - jnp.exp(jnp.sum(dl[2] * dl[3])) + lam_init).reshape(1)
        lb = lb_all[l].reshape(1, SEG_W)
        nw = attn_norm_w[l].reshape(1, D_MODEL)
        fnw = ffn_norm_w[l].reshape(1, D_MODEL)
        onw = final_norm_w.reshape(1, D_MODEL)
        hnw = hgrn_norm_w[l].reshape(1, HEAD_W)
        dnw = diff_norm_w[l].reshape(1, HEAD_W)
        last = l == depth - 1

        def tail(o_h, o_d, g_h, h, tm_mix, tm_moe):
            hx = _mix(o_h, o_d, g_h, h, hnw, dnw, w_o_bf[l], fnw, w_r_hi[l], w_r_lo[l], b_r[l].reshape(1, ROUTER_W),
                      1.0 - lam_init, tm_mix)
            return _moe(hx, fnw, onw, wg_bf, wu_bf, wd_bf, l, tm_moe, last)

        qh, kh, ih, lf, gh, qd, kd, vd, kb, vb = _proj(hp, nw, w_in_bf[l], lb, tabs_p, tm_p)
        o_h, s_fin = _hgrn(qh, kh, ih, lf, zeros_state, s, tb_p, HG_CHUNK, HG_SUB)
        o_d = _attn_prompt(lam, qd, kb, vb, b, s, tq)
        hp = tail(o_h, o_d, gh, hp, tm_p, tm_moe_p)
        kp_l.append(kd.reshape(b, s, N_HEADS, HEAD_W))
        vp_l.append(vd.reshape(b, s, N_HEADS, HEAD_W))
        sp_l.append(s_fin)

        qh, kh, ih, lf, gh, qd, kd, vd, kb, vb = _proj(hs, nw, w_in_bf[l], lb, tabs_s, n_s)
        pad = lambda a: jnp.pad(a.reshape(bd, t, SEG_W), ((0, 0), (0, tp - t), (0, 0)))
        flat = lambda a: pad(a).reshape(bd * tp, SEG_W)
        o_h, s_fin = _hgrn(flat(qh), flat(kh), flat(ih), flat(lf), state_hgrn[l].astype(f32), tp, tp, tp, tp)
        o_h = o_h.reshape(bd, tp, SEG_W)[:, :t].reshape(n_s, SEG_W)
        o_d = _attn_sample(page_table, lam, _stacked_sample_queries(pad(qd)), pad(kd), pad(vd),
                           cache_k, cache_v, l, n_pg)
        o_d = o_d[:, :t].reshape(n_s, SEG_W)
        hs = tail(o_h, o_d, gh, hs, n_s, n_s)
        ks_l.append(kd.reshape(bd, t, N_HEADS, HEAD_W))
        vs_l.append(vd.reshape(bd, t, N_HEADS, HEAD_W))
        ss_l.append(s_fin)

    return (hp.reshape(b, s, D_MODEL), hs.reshape(bd, t, D_MODEL), jnp.stack(kp_l), jnp.stack(vp_l),
            jnp.stack(sp_l), jnp.stack(ks_l), jnp.stack(vs_l), jnp.stack(ss_l))
```

```python
import functools
import math

import jax
import jax.numpy as jnp
from jax import lax
from jax.experimental import pallas as pl
from jax.experimental.pallas import tpu as pltpu

f32 = jnp.float32
bf16 = jnp.bfloat16

D_MODEL = 1024
N_HEADS = 4
HEAD_W = 128
SEG_W = N_HEADS * HEAD_W
N_SEG = 7
DA_DK = 64
ROT_DIM = DA_DK // 4
ROPE_THETA = 500000.0
PAGE_SIZE = 128
N_GROUPS = 4
EXPERTS_PER_GROUP = 4
N_EXPERTS = N_GROUPS * EXPERTS_PER_GROUP
EXPERT_FF = D_MODEL // 2
NORM_EPS = 1e-6
NEG = -1e30
F_MIN = 1e-30

HG_CHUNK = 64
HG_SUB = 16
SAMPLE_PAD = 8
GATE_LANE0 = N_GROUPS
ROUTER_W = 128
ROW_W = D_MODEL + ROUTER_W

VMEM_LIMIT = 48 * 1024 * 1024


def _dot(a, b):
    return jnp.dot(a, b, preferred_element_type=f32)


def _dot_nt(a, b):
    return lax.dot_general(a, b, (((1,), (1,)), ((), ())), preferred_element_type=f32)


def _dot_tn(a, b):
    return lax.dot_general(a, b, (((0,), (0,)), ((), ())), preferred_element_type=f32)


def _rms(x, w):
    return x * lax.rsqrt(jnp.mean(x * x, axis=-1, keepdims=True) + NORM_EPS) * w


def _split3(x):
    hi = x.astype(bf16)
    r1 = x - hi.astype(f32)
    mid = r1.astype(bf16)
    lo = (r1 - mid.astype(f32)).astype(bf16)
    return hi, mid, lo


def _proj_kernel(h_ref, nw_ref, w_ref, lb_ref, c_ref, sa_ref, sb_ref,
                 qh_ref, kh_ref, ih_ref, lf_ref, gh_ref, qd_ref, kd_ref, vd_ref, kb_ref, vb_ref, *, q_scale, transposed):
    xn = _rms(h_ref[...], nw_ref[...]).astype(bf16)

    def seg(i):
        return _dot(xn, w_ref[:, i * SEG_W:(i + 1) * SEG_W])

    qh_ref[...] = seg(0) * (HEAD_W ** -0.5)
    hf = seg(1)
    lb = lb_ref[...]
    e = jnp.exp(-jnp.abs(hf))
    r = 1.0 / (1.0 + e)
    pos = hf >= 0.0
    sig = jnp.where(pos, r, e * r)
    nsig = jnp.where(pos, e * r, r)
    f = lb + (1.0 - lb) * sig
    lf_ref[...] = jnp.log(jnp.maximum(f, F_MIN))
    kh_ref[...] = (1.0 - lb) * nsig
    ih_ref[...] = seg(2)
    gh_ref[...] = seg(3)

    c = c_ref[...]
    sa = sa_ref[...]
    sb = sb_ref[...]

    def rope(z, hh):
        zz = z[:, hh * HEAD_W:(hh + 1) * HEAD_W]
        return zz * c + pltpu.roll(zz, HEAD_W - ROT_DIM // 2, 1) * sa + pltpu.roll(zz, ROT_DIM // 2, 1) * sb

    zq = seg(4)
    zk = seg(5)
    for hh in range(N_HEADS):
        sl = slice(hh * HEAD_W, (hh + 1) * HEAD_W)
        qr = rope(zq, hh) * q_scale
        if transposed:
            qd_ref[sl, :] = qr.T.astype(bf16)
        else:
            qd_ref[:, sl] = qr.astype(bf16)
        kr = rope(zk, hh)
        kd_ref[:, sl] = kr
        kb_ref[:, sl] = kr.astype(bf16)
    vd = seg(6)
    vd_ref[...] = vd
    vb_ref[...] = vd.T.astype(bf16) if transposed else vd.astype(bf16)


def _proj(h, nw, w_bf, lb, tabs, tm, q_scale, seq=None):
    n = h.shape[0]
    npos = tabs[0].shape[0] // tm
    row = lambda i: (i, 0)
    fixed = lambda i: (0, 0)
    tab = lambda i: (i % npos, 0)
    seg_f32 = jax.ShapeDtypeStruct((n, SEG_W), f32)
    seg_bf = jax.ShapeDtypeStruct((n, SEG_W), bf16)
    seg_spec = pl.BlockSpec((tm, SEG_W), row)
    if seq is None:
        t_shape, t_spec = seg_bf, seg_spec
    else:
        nb = seq // tm
        t_shape = jax.ShapeDtypeStruct((n // seq, SEG_W, seq), bf16)
        t_spec = pl.BlockSpec((None, SEG_W, tm), lambda i: (i // nb, 0, i % nb))
    return pl.pallas_call(
        functools.partial(_proj_kernel, q_scale=q_scale, transposed=seq is not None),
        grid=(n // tm,),
        in_specs=[pl.BlockSpec((tm, D_MODEL), row),
                  pl.BlockSpec((1, D_MODEL), fixed),
                  pl.BlockSpec((D_MODEL, N_SEG * SEG_W), fixed),
                  pl.BlockSpec((1, SEG_W), fixed),
                  pl.BlockSpec((tm, HEAD_W), tab),
                  pl.BlockSpec((tm, HEAD_W), tab),
                  pl.BlockSpec((tm, HEAD_W), tab)],
        out_specs=[seg_spec] * 5 + [t_spec, seg_spec, seg_spec, seg_spec, t_spec],
        out_shape=[seg_f32] * 5 + [t_shape, seg_f32, seg_f32, seg_bf, t_shape],
        compiler_params=pltpu.CompilerParams(dimension_semantics=("arbitrary",), vmem_limit_bytes=VMEM_LIMIT),
        name="proj",
    )(h, nw, w_bf, lb, *tabs)


def _rope_tables(pos):
    half = ROT_DIM // 2
    inv = ROPE_THETA ** (-jnp.arange(0, ROT_DIM, 2, dtype=f32) / ROT_DIM)
    ang = pos.astype(f32)[:, None] * inv[None, :]
    cos, sin = jnp.cos(ang), jnp.sin(ang)
    t = pos.shape[0]
    rest = DA_DK - ROT_DIM
    c64 = jnp.concatenate([cos, cos, jnp.ones((t, rest), f32)], axis=1)
    sa64 = jnp.concatenate([-sin, jnp.zeros((t, half + rest), f32)], axis=1)
    sb64 = jnp.concatenate([jnp.zeros((t, half), f32), sin, jnp.zeros((t, rest), f32)], axis=1)
    return tuple(jnp.tile(a, (1, HEAD_W // DA_DK)) for a in (c64, sa64, sb64))


def _hgrn_chunk(q, k, v, lf, st, chunk, sub):
    r_i = lax.broadcasted_iota(jnp.int32, (chunk, chunk), 0)
    c_i = lax.broadcasted_iota(jnp.int32, (chunk, chunk), 1)
    tri = jnp.where(c_i <= r_i, 1.0, 0.0).astype(bf16)
    hi, mid, lo = _split3(lf)
    g = _dot(tri, hi) + _dot(tri, mid) + _dot(tri, lo)
    g_last = g[chunk - 1:chunk, :]
    o_inter = _dot_nt((q * jnp.exp(g)).astype(bf16), st.astype(bf16))
    v_bf = v.astype(bf16)
    ones = jnp.ones((HEAD_W, HEAD_W), bf16)
    row_s = lax.broadcasted_iota(jnp.int32, (sub, HEAD_W), 0)
    outs = []
    for i in range(chunk // sub):
        r0 = i * sub
        gi, qi, ki, vi = g[r0:r0 + sub], q[r0:r0 + sub], k[r0:r0 + sub], v[r0:r0 + sub]
        acc = o_inter[r0:r0 + sub]
        if i > 0:
            b = g[r0 - 1:r0, :]
            qt = (qi * jnp.exp(gi - b)).astype(bf16)
            kt = (k[0:r0] * jnp.exp(b - g[0:r0])).astype(bf16)
            a = _dot_nt(qt, kt)
            acc = acc + _dot(a.astype(bf16), v_bf[0:r0])
        ps = []
        for s in range(sub):
            d = gi - gi[s:s + 1, :]
            e = jnp.where(row_s >= s, jnp.exp(jnp.minimum(d, 0.0)), 0.0)
            ps.append(qi * e * ki[s:s + 1, :])
        a_rep = _dot(jnp.concatenate(ps, axis=0).astype(bf16), ones)
        for s in range(sub):
            acc = acc + a_rep[s * sub:(s + 1) * sub] * vi[s:s + 1, :]
        outs.append(acc)
    o = outs[0] if len(outs) == 1 else jnp.concatenate(outs, axis=0)
    kdec = (k * jnp.exp(g_last - g)).astype(bf16)
    st_new = st * jnp.exp(g_last) + _dot_tn(v_bf, kdec)
    return o, st_new


def _hgrn_kernel(q_ref, k_ref, v_ref, lf_ref, s0_ref, o_ref, sfin_ref, st_ref, *, chunk, sub, n_chunks):
    j = pl.program_id(1)

    @pl.when(j == 0)
    def _():
        for hh in range(N_HEADS):
            st_ref[hh] = s0_ref[hh].T

    def body(c, carry):
        r = pl.multiple_of(c * chunk, chunk)
        for hh in range(N_HEADS):
            sl = slice(hh * HEAD_W, (hh + 1) * HEAD_W)
            o, st_new = _hgrn_chunk(q_ref[pl.ds(r, chunk), sl], k_ref[pl.ds(r, chunk), sl],
                                    v_ref[pl.ds(r, chunk), sl], lf_ref[pl.ds(r, chunk), sl],
                                    st_ref[hh], chunk, sub)
            o_ref[pl.ds(r, chunk), sl] = o
            st_ref[hh] = st_new
        return carry

    lax.fori_loop(0, n_chunks, body, 0)

    @pl.when(j == pl.num_programs(1) - 1)
    def _():
        for hh in range(N_HEADS):
            sfin_ref[hh] = st_ref[hh].T


def _hgrn(q, k, v, lf, s0, t, tb, chunk, sub):
    n = q.shape[0]
    b = n // t
    nj = t // tb
    tok = pl.BlockSpec((tb, SEG_W), lambda bi, j: (bi * nj + j, 0))
    st = pl.BlockSpec((None, N_HEADS, HEAD_W, HEAD_W), lambda bi, j: (bi, 0, 0, 0))
    return pl.pallas_call(
        functools.partial(_hgrn_kernel, chunk=chunk, sub=sub, n_chunks=tb // chunk),
        grid=(b, nj),
        in_specs=[tok, tok, tok, tok, st],
        out_specs=[tok, st],
        out_shape=[jax.ShapeDtypeStruct((n, SEG_W), f32), jax.ShapeDtypeStruct(s0.shape, f32)],
        scratch_shapes=[pltpu.VMEM((N_HEADS, HEAD_W, HEAD_W), f32)],
        compiler_params=pltpu.CompilerParams(dimension_semantics=("arbitrary", "arbitrary"),
                                             vmem_limit_bytes=VMEM_LIMIT),
        name="hgrn",
    )(q, k, v, lf, s0)


def _stack_maps(q):
    lane = lax.broadcasted_iota(jnp.int32, q.shape, 1)
    zero = jnp.zeros_like(q)
    return jnp.concatenate([jnp.where(lane < DA_DK, q, zero), jnp.where(lane >= DA_DK, q, zero)], axis=0)


def _softmax_step(s, v_bf, m_ref, l_ref, acc_ref, rows=None):
    sl = slice(None) if rows is None else rows
    m_prev = m_ref[sl, :]
    m_new = jnp.maximum(m_prev, jnp.max(s, axis=-1, keepdims=True))
    alpha = jnp.exp2(m_prev - m_new)
    p = jnp.exp2(s - m_new)
    l_ref[sl, :] = alpha * l_ref[sl, :] + jnp.sum(p, axis=-1, keepdims=True)
    acc_ref[sl, :] = alpha * acc_ref[sl, :] + _dot(p.astype(bf16), v_bf)
    m_ref[sl, :] = m_new


def _attn_kernel(lam_ref, qt_ref, k_ref, vt_ref, o_ref, m_ref, l_ref, acc_ref, *, tq):
    i = pl.program_id(2)
    qt = qt_ref[...]
    sub = lax.broadcasted_iota(jnp.int32, qt.shape, 0)
    zero = jnp.zeros_like(qt)
    qs = jnp.concatenate([jnp.where(sub < DA_DK, qt, zero), jnp.where(sub >= DA_DK, qt, zero)], axis=1)
    m_ref[...] = jnp.full(m_ref.shape, NEG, f32)
    l_ref[...] = jnp.zeros(l_ref.shape, f32)
    acc_ref[...] = jnp.zeros(acc_ref.shape, f32)

    def step(j, masked):
        r = pl.multiple_of(j * tq, tq)
        s = _dot(k_ref[pl.ds(r, tq), :], qs)
        if masked:
            key = lax.broadcasted_iota(jnp.int32, s.shape, 0)
            qry = lax.broadcasted_iota(jnp.int32, s.shape, 1) & (tq - 1)
            s = jnp.where(key <= qry, s, NEG)
        m_prev = m_ref[...]
        m_new = jnp.maximum(m_prev, jnp.max(s, axis=0, keepdims=True))
        alpha = jnp.exp2(m_prev - m_new)
        p = jnp.exp2(s - m_new)
        l_ref[...] = alpha * l_ref[...] + jnp.sum(p, axis=0, keepdims=True)
        acc_ref[...] = alpha * acc_ref[...] + _dot(vt_ref[:, pl.ds(r, tq)], p.astype(bf16))
        m_ref[...] = m_new

    def body(j, carry):
        step(j, False)
        return carry

    lax.fori_loop(0, i, body, 0)
    step(i, True)
    o = acc_ref[...] / l_ref[...]
    o_ref[...] = (o[:, :tq] - lam_ref[0] * o[:, tq:]).T


def _attn_prompt(lam, qt_bf, k_bf, vt_bf, b, s, tq):
    k3 = k_bf.reshape(b, s, SEG_W)
    qspec = pl.BlockSpec((None, HEAD_W, tq), lambda bi, h, i: (bi, h, i))
    kspec = pl.BlockSpec((None, s, HEAD_W), lambda bi, h, i: (bi, 0, h))
    vspec = pl.BlockSpec((None, HEAD_W, s), lambda bi, h, i: (bi, h, 0))
    out = pl.pallas_call(
        functools.partial(_attn_kernel, tq=tq),
        grid=(b, N_HEADS, s // tq),
        in_specs=[pl.BlockSpec(memory_space=pltpu.SMEM), qspec, kspec, vspec],
        out_specs=pl.BlockSpec((None, tq, HEAD_W), lambda bi, h, i: (bi, i, h)),
        out_shape=jax.ShapeDtypeStruct((b, s, SEG_W), f32),
        scratch_shapes=[pltpu.VMEM((1, 2 * tq), f32), pltpu.VMEM((1, 2 * tq), f32),
                        pltpu.VMEM((HEAD_W, 2 * tq), f32)],
        compiler_params=pltpu.CompilerParams(dimension_semantics=("arbitrary",) * 3, vmem_limit_bytes=VMEM_LIMIT),
        name="attn_prompt",
    )(lam, qt_bf, k3, vt_bf)
    return out.reshape(b * s, SEG_W)


def _attn_sample_kernel(pt_ref, lam_ref, qs_ref, kn_ref, vn_ref, *rest, n_pg, layer):
    del pt_ref, layer
    k_refs = rest[:n_pg]
    v_refs = rest[n_pg:2 * n_pg]
    o_ref, m_ref, l_ref, acc_ref = rest[2 * n_pg:]
    j = pl.program_id(1)
    t = SAMPLE_PAD

    @pl.when(j == 0)
    def _():
        m_ref[...] = jnp.full(m_ref.shape, NEG, f32)
        l_ref[...] = jnp.zeros(l_ref.shape, f32)
        acc_ref[...] = jnp.zeros(acc_ref.shape, f32)

    kcat = jnp.concatenate([kr[...].astype(bf16) for kr in k_refs], axis=0)
    vcat = jnp.concatenate([vr[...].astype(bf16) for vr in v_refs], axis=0)
    s = _dot_nt(qs_ref[...], kcat)
    q_head = lax.broadcasted_iota(jnp.int32, s.shape, 0) // (2 * t)
    k_head = lax.broadcasted_iota(jnp.int32, s.shape, 1) & (N_HEADS - 1)
    _softmax_step(jnp.where(q_head == k_head, s, NEG), vcat, m_ref, l_ref, acc_ref)

    @pl.when(j == pl.num_programs(1) - 1)
    def _():
        for hh in range(N_HEADS):
            rows = slice(hh * 2 * t, (hh + 1) * 2 * t)
            sl = slice(hh * HEAD_W, (hh + 1) * HEAD_W)
            s = _dot_nt(qs_ref[rows, :], kn_ref[:, sl].astype(bf16))
            row = lax.broadcasted_iota(jnp.int32, s.shape, 0) & (t - 1)
            col = lax.broadcasted_iota(jnp.int32, s.shape, 1)
            s = jnp.where(col <= row, s, NEG)
            _softmax_step(s, vn_ref[:, sl].astype(bf16), m_ref, l_ref, acc_ref, rows)
            o = acc_ref[rows, :] / l_ref[rows, :]
            o_ref[:, sl] = o[:t] - lam_ref[0] * o[t:]


def _attn_sample(page_table, lam, qs, kn, vn, cache_k, cache_v, layer, n_pg):
    bd, n_pages = page_table.shape
    t = SAMPLE_PAD

    cache_k = cache_k.reshape(cache_k.shape[:2] + (PAGE_SIZE * N_HEADS, HEAD_W))
    cache_v = cache_v.reshape(cache_v.shape[:2] + (PAGE_SIZE * N_HEADS, HEAD_W))

    def page_spec(p):
        return pl.BlockSpec((None, None, PAGE_SIZE * N_HEADS, HEAD_W),
                            lambda bi, j, pt: (layer, pt[bi * n_pages + j * n_pg + p], 0, 0))

    per_b = lambda shape: pl.BlockSpec((None,) + shape, lambda bi, j, pt: (bi, 0, 0))
    pages = [page_spec(p) for p in range(n_pg)]
    return pl.pallas_call(
        functools.partial(_attn_sample_kernel, n_pg=n_pg, layer=layer),
        grid_spec=pltpu.PrefetchScalarGridSpec(
            num_scalar_prefetch=1,
            grid=(bd, n_pages // n_pg),
            in_specs=[pl.BlockSpec(memory_space=pltpu.SMEM), per_b((2 * t * N_HEADS, HEAD_W)),
                      per_b((t, SEG_W)), per_b((t, SEG_W))] + pages + pages,
            out_specs=per_b((t, SEG_W)),
            scratch_shapes=[pltpu.VMEM((2 * t * N_HEADS, 1), f32), pltpu.VMEM((2 * t * N_HEADS, 1), f32),
                            pltpu.VMEM((2 * t * N_HEADS, HEAD_W), f32)]),
        out_shape=jax.ShapeDtypeStruct((bd, t, SEG_W), f32),
        compiler_params=pltpu.CompilerParams(dimension_semantics=("arbitrary", "arbitrary"),
                                             vmem_limit_bytes=VMEM_LIMIT),
        name="attn_sample",
    )(page_table.reshape(-1), lam, qs, kn, vn, *([cache_k] * n_pg), *([cache_v] * n_pg))


def _route(logits):
    lane = lax.broadcasted_iota(jnp.int32, logits.shape, 1).astype(f32)
    big = float(ROUTER_W)
    is_g = lane < N_GROUPS
    m1 = jnp.max(jnp.where(is_g, logits, -jnp.inf), axis=-1, keepdims=True)
    grp = jnp.min(jnp.where(is_g & (logits == m1), lane, big), axis=-1, keepdims=True)
    p_grp = 1.0 / jnp.sum(jnp.where(is_g, jnp.exp(logits - m1), 0.0), axis=-1, keepdims=True)
    lo = GATE_LANE0 + EXPERTS_PER_GROUP * grp
    in_g = (lane >= lo) & (lane < lo + EXPERTS_PER_GROUP)
    v1 = jnp.max(jnp.where(in_g, logits, -jnp.inf), axis=-1, keepdims=True)
    i1 = jnp.min(jnp.where(in_g & (logits == v1), lane, big), axis=-1, keepdims=True)
    rest = in_g & (lane != i1)
    v2 = jnp.max(jnp.where(rest, logits, -jnp.inf), axis=-1, keepdims=True)
    i2 = jnp.min(jnp.where(rest & (logits == v2), lane, big), axis=-1, keepdims=True)
    e = jnp.exp(v2 - v1)
    w1 = 1.0 / (1.0 + e)
    w2 = e / (1.0 + e)
    gates = jnp.where(lane == i1, p_grp * w1, 0.0) + jnp.where(lane == i2, p_grp * w2, 0.0)
    return jnp.where(lane == 0.0, grp, gates)


def _mix_kernel(oh_ref, od_ref, gh_ref, h_ref, hnw_ref, dnw_ref, wo_ref, fnw_ref, wrh_ref, wrl_ref, br_ref,
                hx_ref, *, od_scale):
    parts = []
    for hh in range(N_HEADS):
        sl = slice(hh * HEAD_W, (hh + 1) * HEAD_W)
        gate = 1.0 / (1.0 + jnp.exp(-gh_ref[:, sl]))
        parts.append((_rms(oh_ref[:, sl], hnw_ref[...]) * gate).astype(bf16))
    for hh in range(N_HEADS):
        sl = slice(hh * HEAD_W, (hh + 1) * HEAD_W)
        parts.append((_rms(od_ref[:, sl], dnw_ref[...]) * od_scale).astype(bf16))
    h2 = h_ref[...] + _dot(jnp.concatenate(parts, axis=1), wo_ref[...])
    xn = _rms(h2, fnw_ref[...])
    x_hi = xn.astype(bf16)
    x_lo = (xn - x_hi.astype(f32)).astype(bf16)
    logits = _dot(x_hi, wrh_ref[...]) + _dot(x_lo, wrh_ref[...]) + _dot(x_hi, wrl_ref[...]) + br_ref[...]
    hx_ref[:, :D_MODEL] = h2
    hx_ref[:, D_MODEL:] = _route(logits)


def _mix(o_h, o_d, g_h, h, hnw, dnw, wo_bf, fnw, wr_hi, wr_lo, br, od_scale, tm):
    n = h.shape[0]
    row = lambda i: (i, 0)
    fixed = lambda i: (0, 0)
    seg = pl.BlockSpec((tm, SEG_W), row)
    return pl.pallas_call(
        functools.partial(_mix_kernel, od_scale=od_scale),
        grid=(n // tm,),
        in_specs=[seg, seg, seg, pl.BlockSpec((tm, D_MODEL), row),
                  pl.BlockSpec((1, HEAD_W), fixed), pl.BlockSpec((1, HEAD_W), fixed),
                  pl.BlockSpec((D_MODEL, D_MODEL), fixed), pl.BlockSpec((1, D_MODEL), fixed),
                  pl.BlockSpec((D_MODEL, ROUTER_W), fixed), pl.BlockSpec((D_MODEL, ROUTER_W), fixed),
                  pl.BlockSpec((1, ROUTER_W), fixed)],
        out_specs=pl.BlockSpec((tm, ROW_W), row),
        out_shape=jax.ShapeDtypeStruct((n, ROW_W), f32),
        compiler_params=pltpu.CompilerParams(dimension_semantics=("arbitrary",), vmem_limit_bytes=VMEM_LIMIT),
        name="mix",
    )(o_h, o_d, g_h, h, hnw, dnw, wo_bf, fnw, wr_hi, wr_lo, br)


def _moe_kernel(ids_ref, grp_ref, cnt_ref, hx_hbm, fnw_ref, onw_ref, wg_ref, wu_ref, wd_ref, out_hbm,
                xbuf, ybuf, gsem, ssem, *, tm, final_norm):
    i = pl.program_id(0)
    n_tiles = pl.num_programs(0)
    slot = i % 2

    def gather_copy(tile, sl, r):
        tok = ids_ref[tile * tm + r]
        return pltpu.make_async_copy(hx_hbm.at[pl.ds(tok, 1)], xbuf.at[sl, pl.ds(r, 1)], gsem.at[sl])

    def scatter_copy(tile, sl, r):
        tok = ids_ref[tile * tm + r]
        return pltpu.make_async_copy(ybuf.at[sl, pl.ds(r, 1)], out_hbm.at[pl.ds(tok, 1)], ssem.at[sl])

    def for_rows(n, fn):
        def body(r, carry):
            fn(r)
            return carry
        if isinstance(n, int):
            lax.fori_loop(0, n, body, 0, unroll=8)
        else:
            lax.fori_loop(0, n, body, 0)

    def gather_wait(sl):
        pltpu.make_async_copy(hx_hbm.at[pl.ds(0, tm)], xbuf.at[sl], gsem.at[sl]).wait()

    def scatter_start(tile, sl):
        cnt = cnt_ref[tile]

        @pl.when(cnt == tm)
        def _():
            for_rows(tm, lambda r: scatter_copy(tile, sl, r).start())

        @pl.when(cnt < tm)
        def _():
            for_rows(cnt, lambda r: scatter_copy(tile, sl, r).start())

    def scatter_wait(tile, sl):
        cnt = cnt_ref[tile]

        @pl.when(cnt == tm)
        def _():
            pltpu.make_async_copy(ybuf.at[sl], out_hbm.at[pl.ds(0, tm)], ssem.at[sl]).wait()

        @pl.when(cnt < tm)
        def _():
            for_rows(cnt, lambda r: scatter_copy(tile, sl, r).wait())

    @pl.when(i == 0)
    def _():
        for_rows(tm, lambda r: gather_copy(0, 0, r).start())

    @pl.when(i + 1 < n_tiles)
    def _():
        for_rows(tm, lambda r: gather_copy(i + 1, 1 - slot, r).start())

    gather_wait(slot)

    @pl.when(i >= 2)
    def _():
        scatter_wait(i - 2, slot)

    xb = xbuf[slot]
    h2 = xb[:, :D_MODEL]
    router = xb[:, D_MODEL:]
    xn = _rms(h2, fnw_ref[...]).astype(bf16)
    lane = lax.broadcasted_iota(jnp.int32, router.shape, 1)
    lane0 = GATE_LANE0 + EXPERTS_PER_GROUP * grp_ref[i]
    acc = h2
    for e in range(EXPERTS_PER_GROUP):
        gate = jnp.sum(jnp.where(lane == lane0 + e, router, 0.0), axis=-1, keepdims=True)
        a = _dot(xn, wg_ref[e])
        hid = a / (1.0 + jnp.exp(-a)) * _dot(xn, wu_ref[e])
        acc = acc + _dot((hid * gate).astype(bf16), wd_ref[e])
    if final_norm:
        acc = _rms(acc, onw_ref[...])
    ybuf[slot] = acc

    scatter_start(i, slot)

    @pl.when(i == n_tiles - 1)
    def _():
        @pl.when(i >= 1)
        def _():
            scatter_wait(i - 1, 1 - slot)
        scatter_wait(i, slot)


def _dispatch(hx, tm):
    n = hx.shape[0]
    n_tiles = n // tm + N_GROUPS
    grp = hx[:, D_MODEL].astype(jnp.int32)
    onehot = (grp[:, None] == jnp.arange(N_GROUPS, dtype=jnp.int32)[None, :]).astype(jnp.int32)
    csum = jnp.cumsum(onehot, axis=0)
    cnt = csum[-1]
    rank = jnp.sum(csum * onehot, axis=1) - 1
    tiles_g = (cnt + tm - 1) // tm
    tile_end = jnp.cumsum(tiles_g)
    tile_start = tile_end - tiles_g
    pos = jnp.sum(onehot * (tile_start * tm)[None, :], axis=1) + rank
    ids = jnp.zeros((n_tiles * tm,), jnp.int32).at[pos].set(jnp.arange(n, dtype=jnp.int32))
    tile = jnp.arange(n_tiles, dtype=jnp.int32)
    tile_grp_raw = jnp.sum((tile[:, None] >= tile_end[None, :]).astype(jnp.int32), axis=1)
    last_grp = jnp.max(jnp.where(cnt > 0, jnp.arange(N_GROUPS, dtype=jnp.int32), 0))
    tile_grp = jnp.minimum(tile_grp_raw, last_grp)
    in_range = tile_grp_raw < N_GROUPS
    rows_left = cnt[tile_grp] - (tile - tile_start[tile_grp]) * tm
    tile_cnt = jnp.where(in_range, jnp.clip(rows_left, 0, tm), 0).astype(jnp.int32)
    return ids, tile_grp, tile_cnt


def _moe(hx, fnw, onw, wg_bf, wu_bf, wd_bf, layer, tm, final_norm):
    n = hx.shape[0]
    ids, tile_grp, tile_cnt = _dispatch(hx, tm)
    n_tiles = tile_grp.shape[0]
    fixed = lambda i, ids_r, grp_r, cnt_r: (0, 0)
    wspec = lambda shape: pl.BlockSpec((None, EXPERTS_PER_GROUP) + shape,
                                       lambda i, ids_r, grp_r, cnt_r: (layer, grp_r[i], 0, 0))
    return pl.pallas_call(
        functools.partial(_moe_kernel, tm=tm, final_norm=final_norm),
        grid_spec=pltpu.PrefetchScalarGridSpec(
            num_scalar_prefetch=3,
            grid=(n_tiles,),
            in_specs=[pl.BlockSpec(memory_space=pl.ANY),
                      pl.BlockSpec((1, D_MODEL), fixed), pl.BlockSpec((1, D_MODEL), fixed),
                      wspec((D_MODEL, EXPERT_FF)), wspec((D_MODEL, EXPERT_FF)), wspec((EXPERT_FF, D_MODEL))],
            out_specs=pl.BlockSpec(memory_space=pl.ANY),
            scratch_shapes=[pltpu.VMEM((2, tm, ROW_W), f32), pltpu.VMEM((2, tm, D_MODEL), f32),
                            pltpu.SemaphoreType.DMA((2,)), pltpu.SemaphoreType.DMA((2,))]),
        out_shape=jax.ShapeDtypeStruct((n, D_MODEL), f32),
        compiler_params=pltpu.CompilerParams(dimension_semantics=("arbitrary",), vmem_limit_bytes=VMEM_LIMIT),
        name="moe",
    )(ids, tile_grp, tile_cnt, hx, fnw, onw, wg_bf, wu_bf, wd_bf)


def _stacked_sample_queries(qd):
    bd, t = qd.shape[:2]
    q5 = qd.reshape(bd, t, N_HEADS, 2, DA_DK)
    eye = jnp.eye(2, dtype=qd.dtype)
    qz = q5[:, :, :, :, None, :] * eye[None, None, None, :, :, None]
    return qz.transpose(0, 2, 3, 1, 4, 5).reshape(bd, N_HEADS * 2 * t, HEAD_W)


def kernel(x_prompt, x_sample, cache_k, cache_v, state_hgrn, page_table, attn_norm_w, w_in, hgrn_lb, hgrn_norm_w,
           diff_lambda, diff_norm_w, w_o, ffn_norm_w, w_r1, b_r1, w_r2, b_r2, w_gate, w_up, w_down, final_norm_w):
    b, s = x_prompt.shape[:2]
    bd, t = x_sample.shape[:2]
    depth = w_in.shape[0]
    n_pages = page_table.shape[1]
    past_len = n_pages * PAGE_SIZE
    assert x_prompt.shape[2] == D_MODEL and w_in.shape[2] == N_SEG * SEG_W and t <= SAMPLE_PAD
    tp = SAMPLE_PAD

    tabs_p = _rope_tables(jnp.arange(s, dtype=jnp.int32))
    tabs_s = _rope_tables(jnp.tile(past_len + jnp.arange(t, dtype=jnp.int32), bd))
    p_lb = jax.nn.softmax(hgrn_lb.astype(f32), axis=0)
    lb_all = jnp.cumsum(p_lb, axis=0) - p_lb[0:1]

    w_in_bf = w_in.astype(bf16)
    w_o_bf = w_o.astype(bf16)
    wg_bf, wu_bf, wd_bf = w_gate.astype(bf16), w_up.astype(bf16), w_down.astype(bf16)
    pad_r = ROUTER_W - N_GROUPS - N_EXPERTS
    w_r = jnp.concatenate([w_r1, w_r2, jnp.zeros((depth, D_MODEL, pad_r), f32)], axis=2)
    w_r_hi = w_r.astype(bf16)
    w_r_lo = (w_r - w_r_hi.astype(f32)).astype(bf16)
    b_r = jnp.concatenate([b_r1, b_r2, jnp.zeros((depth, pad_r), f32)], axis=1)

    tm_p = 256 if (b * s) % 256 == 0 else b * s
    tq = 256 if s % 256 == 0 else s
    tb_p = 256 if s % 256 == 0 else s
    tm_moe_p = 512 if (b * s) % 512 == 0 else b * s
    n_s = bd * t
    n_pg = 8 if n_pages % 8 == 0 else 1
    q_scale = DA_DK ** -0.5 * math.log2(math.e)

    hp = x_prompt.reshape(b * s, D_MODEL)
    hs = x_sample.reshape(n_s, D_MODEL)
    zeros_state = jnp.zeros((b, N_HEADS, HEAD_W, HEAD_W), f32)
    kp_l, vp_l, sp_l, ks_l, vs_l, ss_l = [], [], [], [], [], []
    for l in range(depth):
        lam_init = 0.8 - 0.6 * math.exp(-0.3 * l)
        dl = diff_lambda[l].astype(f32)
        lam = (jnp.exp(jnp.sum(dl[0] * dl[1])) - jnp.exp(jnp.sum(dl[2] * dl[3])) + lam_init).reshape(1)
        lb = lb_all[l].reshape(1, SEG_W)
        nw = attn_norm_w[l].reshape(1, D_MODEL)
        fnw = ffn_norm_w[l].reshape(1, D_MODEL)
        onw = final_norm_w.reshape(1, D_MODEL)
        hnw = hgrn_norm_w[l].reshape(1, HEAD_W)
        dnw = diff_norm_w[l].reshape(1, HEAD_W)
        last = l == depth - 1

        def tail(o_h, o_d, g_h, h, tm_mix, tm_moe):
            hx = _mix(o_h, o_d, g_h, h, hnw, dnw, w_o_bf[l], fnw, w_r_hi[l], w_r_lo[l], b_r[l].reshape(1, ROUTER_W),
                      1.0 - lam_init, tm_mix)
            return _moe(hx, fnw, onw, wg_bf, wu_bf, wd_bf, l, tm_moe, last)

        qh, kh, ih, lf, gh, qd, kd, vd, kb, vb = _proj(hp, nw, w_in_bf[l], lb, tabs_p, tm_p, q_scale, seq=s)
        o_h, s_fin = _hgrn(qh, kh, ih, lf, zeros_state, s, tb_p, HG_CHUNK, HG_SUB)
        o_d = _attn_prompt(lam, qd, kb, vb, b, s, tq)
        hp = tail(o_h, o_d, gh, hp, tm_p, tm_moe_p)
        kp_l.append(kd.reshape(b, s, N_HEADS, HEAD_W))
        vp_l.append(vd.reshape(b, s, N_HEADS, HEAD_W))
        sp_l.append(s_fin)

        qh, kh, ih, lf, gh, qd, kd, vd, kb, vb = _proj(hs, nw, w_in_bf[l], lb, tabs_s, n_s, q_scale)
        pad = lambda a: jnp.pad(a.reshape(bd, t, SEG_W), ((0, 0), (0, tp - t), (0, 0)))
        flat = lambda a: pad(a).reshape(bd * tp, SEG_W)
        o_h, s_fin = _hgrn(flat(qh), flat(kh), flat(ih), flat(lf), state_hgrn[l].astype(f32), tp, tp, tp, tp)
        o_h = o_h.reshape(bd, tp, SEG_W)[:, :t].reshape(n_s, SEG_W)
        o_d = _attn_sample(page_table, lam, _stacked_sample_queries(pad(qd)), pad(kd), pad(vd),
                           cache_k, cache_v, l, n_pg)
        o_d = o_d[:, :t].reshape(n_s, SEG_W)
        hs = tail(o_h, o_d, gh, hs, n_s, n_s)
        ks_l.append(kd.reshape(bd, t, N_HEADS, HEAD_W))
        vs_l.append(vd.reshape(bd, t, N_HEADS, HEAD_W))
        ss_l.append(s_fin)

    return (hp.reshape(b, s, D_MODEL), hs.reshape(bd, t, D_MODEL), jnp.stack(kp_l), jnp.stack(vp_l),
            jnp.stack(sp_l), jnp.stack(ks_l), jnp.stack(vs_l), jnp.stack(ss_l))
```

```python
import functools
import math

import jax
import jax.numpy as jnp
from jax import lax
from jax.experimental import pallas as pl
from jax.experimental.pallas import tpu as pltpu

f32 = jnp.float32
bf16 = jnp.bfloat16

D_MODEL = 1024
N_HEADS = 4
HEAD_W = 128
SEG_W = N_HEADS * HEAD_W
N_SEG = 7
DA_DK = 64
ROT_DIM = DA_DK // 4
ROPE_THETA = 500000.0
PAGE_SIZE = 128
N_GROUPS = 4
EXPERTS_PER_GROUP = 4
N_EXPERTS = N_GROUPS * EXPERTS_PER_GROUP
EXPERT_FF = D_MODEL // 2
NORM_EPS = 1e-6
NEG = -1e30
F_MIN = 1e-30

HG_CHUNK = 64
LOG2E = math.log2(math.e)
SAMPLE_PAD = 8
GATE_LANE0 = N_GROUPS
ROUTER_W = 128
ROW_W = D_MODEL + ROUTER_W

VMEM_LIMIT = 48 * 1024 * 1024


def _dot(a, b):
    return jnp.dot(a, b, preferred_element_type=f32)


def _dot_nt(a, b):
    return lax.dot_general(a, b, (((1,), (1,)), ((), ())), preferred_element_type=f32)


def _dot_tn(a, b):
    return lax.dot_general(a, b, (((0,), (0,)), ((), ())), preferred_element_type=f32)


def _rms(x, w):
    return x * lax.rsqrt(jnp.mean(x * x, axis=-1, keepdims=True) + NORM_EPS) * w


def _split3(x):
    hi = x.astype(bf16)
    r1 = x - hi.astype(f32)
    mid = r1.astype(bf16)
    lo = (r1 - mid.astype(f32)).astype(bf16)
    return hi, mid, lo


def _proj_kernel(h_ref, nw_ref, w_ref, lb_ref, c_ref, sa_ref, sb_ref,
                 qh_ref, kh_ref, ih_ref, lf_ref, gh_ref, qd_ref, kd_ref, vd_ref, kb_ref, vb_ref, *, q_scale, transposed):
    tm = h_ref.shape[0]
    xn = _rms(h_ref[...], nw_ref[...]).astype(bf16)

    def seg(i):
        return _dot(xn, w_ref[:, i * SEG_W:(i + 1) * SEG_W])

    qh_ref[...] = seg(0) * (HEAD_W ** -0.5)
    hf = seg(1)
    lb = lb_ref[...]
    e = jnp.exp(-jnp.abs(hf))
    r = 1.0 / (1.0 + e)
    pos = hf >= 0.0
    sig = jnp.where(pos, r, e * r)
    nsig = jnp.where(pos, e * r, r)
    f = lb + (1.0 - lb) * sig
    lf_ref[...] = jnp.log(jnp.maximum(f, F_MIN)) * LOG2E
    kh_ref[...] = (1.0 - lb) * nsig
    ih_ref[...] = seg(2)
    gh_ref[...] = seg(3)

    c = c_ref[...]
    sa = sa_ref[...]
    sb = sb_ref[...]

    def rope(z, hh):
        zz = z[:, hh * HEAD_W:(hh + 1) * HEAD_W]
        return zz * c + pltpu.roll(zz, HEAD_W - ROT_DIM // 2, 1) * sa + pltpu.roll(zz, ROT_DIM // 2, 1) * sb

    zq = seg(4)
    zk = seg(5)
    for hh in range(N_HEADS):
        sl = slice(hh * HEAD_W, (hh + 1) * HEAD_W)
        qr = rope(zq, hh) * q_scale
        if transposed:
            qd_ref[sl, :] = qr.T.astype(bf16)
        else:
            qd_ref[:, sl] = qr.astype(bf16)
        kr = rope(zk, hh)
        kd_ref[pl.ds(hh, tm, stride=N_HEADS), :] = kr
        kb_ref[:, sl] = kr.astype(bf16)
    vd = seg(6)
    for hh in range(N_HEADS):
        vd_ref[pl.ds(hh, tm, stride=N_HEADS), :] = vd[:, hh * HEAD_W:(hh + 1) * HEAD_W]
    vb_ref[...] = vd.T.astype(bf16) if transposed else vd.astype(bf16)


def _proj(h, n, nw, w_bf, lb, tabs, tm, q_scale, seq=None):
    npos = tabs[0].shape[0] // tm
    row = lambda i: (i, 0)
    fixed = lambda i: (0, 0)
    tab = lambda i: (i % npos, 0)
    seg_f32 = jax.ShapeDtypeStruct((n, SEG_W), f32)
    seg_bf = jax.ShapeDtypeStruct((n, SEG_W), bf16)
    seg_spec = pl.BlockSpec((tm, SEG_W), row)
    rows_shape = jax.ShapeDtypeStruct((n * N_HEADS, HEAD_W), f32)
    rows_spec = pl.BlockSpec((tm * N_HEADS, HEAD_W), row)
    if seq is None:
        t_shape, t_spec = seg_bf, seg_spec
    else:
        nb = seq // tm
        t_shape = jax.ShapeDtypeStruct((n // seq, SEG_W, seq), bf16)
        t_spec = pl.BlockSpec((None, SEG_W, tm), lambda i: (i // nb, 0, i % nb))
    return pl.pallas_call(
        functools.partial(_proj_kernel, q_scale=q_scale, transposed=seq is not None),
        grid=(n // tm,),
        in_specs=[pl.BlockSpec((tm, D_MODEL), row),
                  pl.BlockSpec((1, D_MODEL), fixed),
                  pl.BlockSpec((D_MODEL, N_SEG * SEG_W), fixed),
                  pl.BlockSpec((1, SEG_W), fixed),
                  pl.BlockSpec((tm, HEAD_W), tab),
                  pl.BlockSpec((tm, HEAD_W), tab),
                  pl.BlockSpec((tm, HEAD_W), tab)],
        out_specs=[seg_spec] * 5 + [t_spec, rows_spec, rows_spec, seg_spec, t_spec],
        out_shape=[seg_f32] * 5 + [t_shape, rows_shape, rows_shape, seg_bf, t_shape],
        compiler_params=pltpu.CompilerParams(dimension_semantics=("arbitrary",), vmem_limit_bytes=VMEM_LIMIT),
        name="proj",
    )(h, nw, w_bf, lb, *tabs)


def _rope_tables(pos):
    half = ROT_DIM // 2
    inv = ROPE_THETA ** (-jnp.arange(0, ROT_DIM, 2, dtype=f32) / ROT_DIM)
    ang = pos.astype(f32)[:, None] * inv[None, :]
    cos, sin = jnp.cos(ang), jnp.sin(ang)
    t = pos.shape[0]
    rest = DA_DK - ROT_DIM
    c64 = jnp.concatenate([cos, cos, jnp.ones((t, rest), f32)], axis=1)
    sa64 = jnp.concatenate([-sin, jnp.zeros((t, half + rest), f32)], axis=1)
    sb64 = jnp.concatenate([jnp.zeros((t, half), f32), sin, jnp.zeros((t, rest), f32)], axis=1)
    return tuple(jnp.tile(a, (1, HEAD_W // DA_DK)) for a in (c64, sa64, sb64))


def _hgrn_chunk(q, k, v, lf, st, chunk, consts):
    sel, pair_masks, halves = consts
    heads = [slice(hh * HEAD_W, (hh + 1) * HEAD_W) for hh in range(N_HEADS)]
    terms = jnp.concatenate(_split3(lf), axis=1)
    gs3 = _dot(sel, terms)
    gs = gs3[:, :SEG_W] + gs3[:, SEG_W:2 * SEG_W] + gs3[:, 2 * SEG_W:]
    g = gs[:chunk]
    g_last = g[chunk - 1:chunk, :]
    qx = (q * jnp.exp2(g)).astype(bf16)
    q_bf, k_bf, v_bf = q.astype(bf16), k.astype(bf16), v.astype(bf16)
    st_bf = [s.astype(bf16) for s in st]
    o = [_dot_nt(qx[:, sl], st_bf[hh]) for hh, sl in enumerate(heads)]
    a = [_dot_nt(q_bf[:, sl], k_bf[:, sl]) * pair_masks[0] for sl in heads]
    n_mm = 1
    for lvl, h in enumerate(halves, start=1):
        if h % 8 == 0:
            ref = jnp.concatenate([jnp.broadcast_to(g[r0 + h - 1:r0 + h, :], (2 * h, SEG_W))
                                   for r0 in range(0, chunk, 2 * h)], axis=0)
        else:
            ref = gs[n_mm * chunk:(n_mm + 1) * chunk]
            n_mm += 1
        e = jnp.exp2(-jnp.abs(g - ref))
        qe, ke = (q * e).astype(bf16), (k * e).astype(bf16)
        a = [a[hh] + _dot_nt(qe[:, sl], ke[:, sl]) * pair_masks[lvl] for hh, sl in enumerate(heads)]
    kdec = (k * jnp.exp2(g_last - g)).astype(bf16)
    decay = jnp.exp2(g_last)
    o = jnp.concatenate([o[hh] + _dot(a[hh].astype(bf16), v_bf[:, sl]) for hh, sl in enumerate(heads)], axis=1)
    st_new = [st[hh] * decay[:, sl] + _dot_tn(v_bf[:, sl], kdec[:, sl]) for hh, sl in enumerate(heads)]
    return o, st_new


def _hgrn_consts(chunk):
    r = lax.broadcasted_iota(jnp.int32, (chunk, chunk), 0)
    c = lax.broadcasted_iota(jnp.int32, (chunk, chunk), 1)
    sels = [jnp.where(c <= r, 1.0, 0.0)]
    masks = [jnp.where(c == r, 1.0, 0.0)]
    halves = []
    h = chunk // 2
    while h >= 1:
        blk = -(2 * h)
        if h % 8:
            sels.append(jnp.where(c <= (r & blk) + (h - 1), 1.0, 0.0))
        same_block = (r & blk) == (c & blk)
        masks.append(jnp.where(same_block & ((r & h) != 0) & ((c & h) == 0), 1.0, 0.0))
        halves.append(h)
        h //= 2
    return jnp.concatenate(sels, axis=0).astype(bf16), masks, halves


def _hgrn_kernel(q_ref, k_ref, v_ref, lf_ref, s0_ref, o_ref, sfin_ref, st_ref, *, chunk, n_chunks):
    j = pl.program_id(1)
    consts = _hgrn_consts(chunk)

    @pl.when(j == 0)
    def _():
        for hh in range(N_HEADS):
            st_ref[hh] = s0_ref[hh].T

    def body(c, carry):
        rows = pl.ds(pl.multiple_of(c * chunk, chunk), chunk)
        o, st_new = _hgrn_chunk(q_ref[rows, :], k_ref[rows, :], v_ref[rows, :], lf_ref[rows, :],
                                [st_ref[hh] for hh in range(N_HEADS)], chunk, consts)
        o_ref[rows, :] = o
        for hh in range(N_HEADS):
            st_ref[hh] = st_new[hh]
        return carry

    lax.fori_loop(0, n_chunks, body, 0)

    @pl.when(j == pl.num_programs(1) - 1)
    def _():
        for hh in range(N_HEADS):
            sfin_ref[hh] = st_ref[hh].T


def _hgrn(q, k, v, lf, s0, t, tb, chunk):
    n = q.shape[0]
    b = n // t
    nj = t // tb
    tok = pl.BlockSpec((tb, SEG_W), lambda bi, j: (bi * nj + j, 0))
    st = pl.BlockSpec((None, N_HEADS, HEAD_W, HEAD_W), lambda bi, j: (bi, 0, 0, 0))
    return pl.pallas_call(
        functools.partial(_hgrn_kernel, chunk=chunk,n_chunks=tb // chunk),
        grid=(b, nj),
        in_specs=[tok, tok, tok, tok, st],
        out_specs=[tok, st],
        out_shape=[jax.ShapeDtypeStruct((n, SEG_W), f32), jax.ShapeDtypeStruct(s0.shape, f32)],
        scratch_shapes=[pltpu.VMEM((N_HEADS, HEAD_W, HEAD_W), f32)],
        compiler_params=pltpu.CompilerParams(dimension_semantics=("arbitrary", "arbitrary"),
                                             vmem_limit_bytes=VMEM_LIMIT),
        name="hgrn",
    )(q, k, v, lf, s0)


def _stack_maps(q):
    lane = lax.broadcasted_iota(jnp.int32, q.shape, 1)
    zero = jnp.zeros_like(q)
    return jnp.concatenate([jnp.where(lane < DA_DK, q, zero), jnp.where(lane >= DA_DK, q, zero)], axis=0)


def _softmax_step(s, v_bf, m_ref, l_ref, acc_ref, rows=None):
    sl = slice(None) if rows is None else rows
    m_prev = m_ref[sl, :]
    m_new = jnp.maximum(m_prev, jnp.max(s, axis=-1, keepdims=True))
    alpha = jnp.exp2(m_prev - m_new)
    p = jnp.exp2(s - m_new)
    l_ref[sl, :] = alpha * l_ref[sl, :] + jnp.sum(p, axis=-1, keepdims=True)
    acc_ref[sl, :] = alpha * acc_ref[sl, :] + _dot(p.astype(bf16), v_bf)
    m_ref[sl, :] = m_new


def _attn_kernel(lam_ref, qt_ref, k_ref, vt_ref, o_ref, m_ref, l_ref, acc_ref, *, tq):
    i = pl.program_id(2)
    qt = qt_ref[...]
    sub = lax.broadcasted_iota(jnp.int32, qt.shape, 0)
    zero = jnp.zeros_like(qt)
    qs = jnp.concatenate([jnp.where(sub < DA_DK, qt, zero), jnp.where(sub >= DA_DK, qt, zero)], axis=1)
    m_ref[...] = jnp.full(m_ref.shape, NEG, f32)
    l_ref[...] = jnp.zeros(l_ref.shape, f32)
    acc_ref[...] = jnp.zeros(acc_ref.shape, f32)

    def step(j, masked):
        r = pl.multiple_of(j * tq, tq)
        s = _dot(k_ref[pl.ds(r, tq), :], qs)
        if masked:
            key = lax.broadcasted_iota(jnp.int32, s.shape, 0)
            qry = lax.broadcasted_iota(jnp.int32, s.shape, 1) & (tq - 1)
            s = jnp.where(key <= qry, s, NEG)
        m_prev = m_ref[...]
        m_new = jnp.maximum(m_prev, jnp.max(s, axis=0, keepdims=True))
        alpha = jnp.exp2(m_prev - m_new)
        p = jnp.exp2(s - m_new)
        l_ref[...] = alpha * l_ref[...] + jnp.sum(p, axis=0, keepdims=True)
        acc_ref[...] = alpha * acc_ref[...] + _dot(vt_ref[:, pl.ds(r, tq)], p.astype(bf16))
        m_ref[...] = m_new

    def body(j, carry):
        step(j, False)
        return carry

    lax.fori_loop(0, i, body, 0)
    step(i, True)
    o = acc_ref[...] / l_ref[...]
    o_ref[...] = (o[:, :tq] - lam_ref[0] * o[:, tq:]).T


def _attn_prompt(lam, qt_bf, k_bf, vt_bf, b, s, tq):
    k3 = k_bf.reshape(b, s, SEG_W)
    qspec = pl.BlockSpec((None, HEAD_W, tq), lambda bi, h, i: (bi, h, i))
    kspec = pl.BlockSpec((None, s, HEAD_W), lambda bi, h, i: (bi, 0, h))
    vspec = pl.BlockSpec((None, HEAD_W, s), lambda bi, h, i: (bi, h, 0))
    out = pl.pallas_call(
        functools.partial(_attn_kernel, tq=tq),
        grid=(b, N_HEADS, s // tq),
        in_specs=[pl.BlockSpec(memory_space=pltpu.SMEM), qspec, kspec, vspec],
        out_specs=pl.BlockSpec((None, tq, HEAD_W), lambda bi, h, i: (bi, i, h)),
        out_shape=jax.ShapeDtypeStruct((b, s, SEG_W), f32),
        scratch_shapes=[pltpu.VMEM((1, 2 * tq), f32), pltpu.VMEM((1, 2 * tq), f32),
                        pltpu.VMEM((HEAD_W, 2 * tq), f32)],
        compiler_params=pltpu.CompilerParams(dimension_semantics=("arbitrary",) * 3, vmem_limit_bytes=VMEM_LIMIT),
        name="attn_prompt",
    )(lam, qt_bf, k3, vt_bf)
    return out.reshape(b * s, SEG_W)


def _attn_sample_kernel(pt_ref, lam_ref, qs_ref, kn_ref, vn_ref, *rest, n_pg, layer):
    del pt_ref, layer
    k_refs = rest[:n_pg]
    v_refs = rest[n_pg:2 * n_pg]
    o_ref, m_ref, l_ref, acc_ref = rest[2 * n_pg:]
    j = pl.program_id(1)
    t = SAMPLE_PAD

    @pl.when(j == 0)
    def _():
        m_ref[...] = jnp.full(m_ref.shape, NEG, f32)
        l_ref[...] = jnp.zeros(l_ref.shape, f32)
        acc_ref[...] = jnp.zeros(acc_ref.shape, f32)

    for hh in range(N_HEADS):
        rows = slice(hh * 2 * t, (hh + 1) * 2 * t)
        head = pl.ds(hh, PAGE_SIZE, stride=N_HEADS)
        kcat = jnp.concatenate([kr[head, :].astype(bf16) for kr in k_refs], axis=0)
        vcat = jnp.concatenate([vr[head, :].astype(bf16) for vr in v_refs], axis=0)
        _softmax_step(_dot_nt(qs_ref[rows, :], kcat), vcat, m_ref, l_ref, acc_ref, rows)

    @pl.when(j == pl.num_programs(1) - 1)
    def _():
        for hh in range(N_HEADS):
            rows = slice(hh * 2 * t, (hh + 1) * 2 * t)
            sl = slice(hh * HEAD_W, (hh + 1) * HEAD_W)
            s = _dot_nt(qs_ref[rows, :], kn_ref[:, sl].astype(bf16))
            row = lax.broadcasted_iota(jnp.int32, s.shape, 0) & (t - 1)
            col = lax.broadcasted_iota(jnp.int32, s.shape, 1)
            s = jnp.where(col <= row, s, NEG)
            _softmax_step(s, vn_ref[:, sl].astype(bf16), m_ref, l_ref, acc_ref, rows)
            o = acc_ref[rows, :] / l_ref[rows, :]
            o_ref[:, sl] = o[:t] - lam_ref[0] * o[t:]


def _attn_sample(page_table, lam, qs, kn, vn, cache_k, cache_v, layer, n_pg):
    bd, n_pages = page_table.shape
    t = SAMPLE_PAD

    cache_k = cache_k.reshape(cache_k.shape[:2] + (PAGE_SIZE * N_HEADS, HEAD_W))
    cache_v = cache_v.reshape(cache_v.shape[:2] + (PAGE_SIZE * N_HEADS, HEAD_W))

    def page_spec(p):
        return pl.BlockSpec((None, None, PAGE_SIZE * N_HEADS, HEAD_W),
                            lambda bi, j, pt: (layer, pt[bi * n_pages + j * n_pg + p], 0, 0))

    per_b = lambda shape: pl.BlockSpec((None,) + shape, lambda bi, j, pt: (bi, 0, 0))
    pages = [page_spec(p) for p in range(n_pg)]
    return pl.pallas_call(
        functools.partial(_attn_sample_kernel, n_pg=n_pg, layer=layer),
        grid_spec=pltpu.PrefetchScalarGridSpec(
            num_scalar_prefetch=1,
            grid=(bd, n_pages // n_pg),
            in_specs=[pl.BlockSpec(memory_space=pltpu.SMEM), per_b((2 * t * N_HEADS, HEAD_W)),
                      per_b((t, SEG_W)), per_b((t, SEG_W))] + pages + pages,
            out_specs=per_b((t, SEG_W)),
            scratch_shapes=[pltpu.VMEM((2 * t * N_HEADS, 1), f32), pltpu.VMEM((2 * t * N_HEADS, 1), f32),
                            pltpu.VMEM((2 * t * N_HEADS, HEAD_W), f32)]),
        out_shape=jax.ShapeDtypeStruct((bd, t, SEG_W), f32),
        compiler_params=pltpu.CompilerParams(dimension_semantics=("arbitrary", "arbitrary"),
                                             vmem_limit_bytes=VMEM_LIMIT),
        name="attn_sample",
    )(page_table.reshape(-1), lam, qs, kn, vn, *([cache_k] * n_pg), *([cache_v] * n_pg))


def _route(logits):
    lane = lax.broadcasted_iota(jnp.int32, logits.shape, 1).astype(f32)
    big = float(ROUTER_W)
    is_g = lane < N_GROUPS
    m1 = jnp.max(jnp.where(is_g, logits, -jnp.inf), axis=-1, keepdims=True)
    grp = jnp.min(jnp.where(is_g & (logits == m1), lane, big), axis=-1, keepdims=True)
    p_grp = 1.0 / jnp.sum(jnp.where(is_g, jnp.exp(logits - m1), 0.0), axis=-1, keepdims=True)
    lo = GATE_LANE0 + EXPERTS_PER_GROUP * grp
    in_g = (lane >= lo) & (lane < lo + EXPERTS_PER_GROUP)
    v1 = jnp.max(jnp.where(in_g, logits, -jnp.inf), axis=-1, keepdims=True)
    i1 = jnp.min(jnp.where(in_g & (logits == v1), lane, big), axis=-1, keepdims=True)
    rest = in_g & (lane != i1)
    v2 = jnp.max(jnp.where(rest, logits, -jnp.inf), axis=-1, keepdims=True)
    i2 = jnp.min(jnp.where(rest & (logits == v2), lane, big), axis=-1, keepdims=True)
    e = jnp.exp(v2 - v1)
    w1 = 1.0 / (1.0 + e)
    w2 = e / (1.0 + e)
    gates = jnp.where(lane == i1, p_grp * w1, 0.0) + jnp.where(lane == i2, p_grp * w2, 0.0)
    return jnp.where(lane == 0.0, grp, gates)


def _mix_kernel(oh_ref, od_ref, gh_ref, h_ref, hnw_ref, dnw_ref, wo_ref, fnw_ref, wrh_ref, wrl_ref, br_ref,
                hx_ref, *, od_scale):
    parts = []
    for hh in range(N_HEADS):
        sl = slice(hh * HEAD_W, (hh + 1) * HEAD_W)
        gate = 1.0 / (1.0 + jnp.exp(-gh_ref[:, sl]))
        parts.append((_rms(oh_ref[:, sl], hnw_ref[...]) * gate).astype(bf16))
    for hh in range(N_HEADS):
        sl = slice(hh * HEAD_W, (hh + 1) * HEAD_W)
        parts.append((_rms(od_ref[:, sl], dnw_ref[...]) * od_scale).astype(bf16))
    h2 = h_ref[...] + _dot(jnp.concatenate(parts, axis=1), wo_ref[...])
    xn = _rms(h2, fnw_ref[...])
    x_hi = xn.astype(bf16)
    x_lo = (xn - x_hi.astype(f32)).astype(bf16)
    logits = _dot(x_hi, wrh_ref[...]) + _dot(x_lo, wrh_ref[...]) + _dot(x_hi, wrl_ref[...]) + br_ref[...]
    hx_ref[:, :D_MODEL] = h2
    hx_ref[:, D_MODEL:] = _route(logits)


def _mix(o_h, o_d, g_h, h, hnw, dnw, wo_bf, fnw, wr_hi, wr_lo, br, od_scale, tm):
    n = o_h.shape[0]
    row = lambda i: (i, 0)
    fixed = lambda i: (0, 0)
    seg = pl.BlockSpec((tm, SEG_W), row)
    return pl.pallas_call(
        functools.partial(_mix_kernel, od_scale=od_scale),
        grid=(n // tm,),
        in_specs=[seg, seg, seg, pl.BlockSpec((tm, D_MODEL), row),
                  pl.BlockSpec((1, HEAD_W), fixed), pl.BlockSpec((1, HEAD_W), fixed),
                  pl.BlockSpec((D_MODEL, D_MODEL), fixed), pl.BlockSpec((1, D_MODEL), fixed),
                  pl.BlockSpec((D_MODEL, ROUTER_W), fixed), pl.BlockSpec((D_MODEL, ROUTER_W), fixed),
                  pl.BlockSpec((1, ROUTER_W), fixed)],
        out_specs=pl.BlockSpec((tm, ROW_W), row),
        out_shape=jax.ShapeDtypeStruct((n, ROW_W), f32),
        compiler_params=pltpu.CompilerParams(dimension_semantics=("arbitrary",), vmem_limit_bytes=VMEM_LIMIT),
        name="mix",
    )(o_h, o_d, g_h, h, hnw, dnw, wo_bf, fnw, wr_hi, wr_lo, br)


def _moe_kernel(gid_ref, sid_ref, grp_ref, hx_hbm, fnw_ref, onw_ref, wg_ref, wu_ref, wd_ref, out_hbm,
                xbuf, ybuf, gsem, ssem, *, tm, final_norm):
    i = pl.program_id(0)
    last = pl.num_programs(0) - 1

    def gather_start(tile, sl, r):
        tok = gid_ref[tile * tm + r]
        pltpu.make_async_copy(hx_hbm.at[pl.ds(tok, 1)], xbuf.at[sl, pl.ds(r, 1)], gsem.at[sl]).start()

    def scatter_start(tile, sl, r):
        row = sid_ref[tile * tm + r]
        pltpu.make_async_copy(ybuf.at[sl, pl.ds(r, 1)], out_hbm.at[pl.ds(row, 1)], ssem.at[sl]).start()

    def gather_wait(sl):
        pltpu.make_async_copy(hx_hbm.at[pl.ds(0, tm)], xbuf.at[sl], gsem.at[sl]).wait()

    def scatter_wait(sl):
        pltpu.make_async_copy(ybuf.at[sl], out_hbm.at[pl.ds(0, tm)], ssem.at[sl]).wait()

    def for_rows(fn):
        def body(r, carry):
            fn(r)
            return carry
        lax.fori_loop(0, tm, body, 0, unroll=8)

    @pl.when(i == 0)
    def _():
        for_rows(lambda r: gather_start(1, 0, r))
        ybuf[1] = jnp.zeros(ybuf.shape[1:], f32)

    def step(sl):
        gather_wait(sl)
        xb = xbuf[sl]
        h2 = xb[:, :D_MODEL]
        router = xb[:, D_MODEL:]
        xn = _rms(h2, fnw_ref[...]).astype(bf16)
        lane = lax.broadcasted_iota(jnp.int32, router.shape, 1)
        lane0 = GATE_LANE0 + EXPERTS_PER_GROUP * grp_ref[i]
        acc = h2
        per_e = tm // EXPERTS_PER_GROUP
        for e in range(EXPERTS_PER_GROUP):
            for r in range(e * per_e, (e + 1) * per_e):
                gather_start(i + 2, 1 - sl, r)
                scatter_start(i, 1 - sl, r)
            gate = jnp.sum(jnp.where(lane == lane0 + e, router, 0.0), axis=-1, keepdims=True)
            a = _dot(xn, wg_ref[e])
            hid = a / (1.0 + jnp.exp(-a)) * _dot(xn, wu_ref[e])
            acc = acc + _dot((hid * gate).astype(bf16), wd_ref[e])
        if final_norm:
            acc = _rms(acc, onw_ref[...])

        @pl.when(i >= 1)
        def _():
            scatter_wait(sl)
        ybuf[sl] = acc

        @pl.when(i == last)
        def _():
            for_rows(lambda r: scatter_start(i + 1, sl, r))
            scatter_wait(1 - sl)
            scatter_wait(sl)
            gather_wait(1 - sl)

    @pl.when(i % 2 == 0)
    def _():
        step(0)

    @pl.when(i % 2 == 1)
    def _():
        step(1)


def _dispatch(hx, tm):
    n = hx.shape[0]
    n_tiles = n // tm + N_GROUPS
    grp = hx[:, D_MODEL].astype(jnp.int32)
    onehot = (grp[:, None] == jnp.arange(N_GROUPS, dtype=jnp.int32)[None, :]).astype(jnp.int32)
    csum = jnp.cumsum(onehot, axis=0)
    cnt = csum[-1]
    rank = jnp.sum(csum * onehot, axis=1) - 1
    tiles_g = (cnt + tm - 1) // tm
    tile_end = jnp.cumsum(tiles_g)
    tile_start = tile_end - tiles_g
    pos = jnp.sum(onehot * (tile_start * tm)[None, :], axis=1) + rank
    tile = jnp.arange(n_tiles, dtype=jnp.int32)
    tile_grp_raw = jnp.sum((tile[:, None] >= tile_end[None, :]).astype(jnp.int32), axis=1)
    last_grp = jnp.max(jnp.where(cnt > 0, jnp.arange(N_GROUPS, dtype=jnp.int32), 0))
    tile_grp = jnp.minimum(tile_grp_raw, last_grp)
    in_range = tile_grp_raw < N_GROUPS
    rows_left = cnt[tile_grp] - (tile - tile_start[tile_grp]) * tm
    tile_cnt = jnp.where(in_range, jnp.clip(rows_left, 0, tm), 0).astype(jnp.int32)
    n_slots = (n_tiles + 2) * tm
    gather_ids = jnp.zeros((n_slots,), jnp.int32).at[pos + tm].set(jnp.arange(n, dtype=jnp.int32))
    real = (jnp.arange(tm, dtype=jnp.int32)[None, :] < jnp.pad(tile_cnt, (1, 1))[:, None]).reshape(-1)
    n_pad = (N_GROUPS + 1) * tm
    pad_rank = jnp.minimum(jnp.cumsum(jnp.where(real, 0, 1)) - 1, n_pad - 1)
    scatter_ids = jnp.where(real, gather_ids, n + pad_rank).astype(jnp.int32)
    return gather_ids, scatter_ids, tile_grp


def _moe(hx, fnw, onw, wg_bf, wu_bf, wd_bf, layer, tm, final_norm):
    n = hx.shape[0]
    gather_ids, scatter_ids, tile_grp = _dispatch(hx, tm)
    n_tiles = tile_grp.shape[0]
    fixed = lambda i, gid_r, sid_r, grp_r: (0, 0)
    wspec = lambda shape: pl.BlockSpec((None, EXPERTS_PER_GROUP) + shape,
                                       lambda i, gid_r, sid_r, grp_r: (layer, grp_r[i], 0, 0))
    return pl.pallas_call(
        functools.partial(_moe_kernel, tm=tm, final_norm=final_norm),
        grid_spec=pltpu.PrefetchScalarGridSpec(
            num_scalar_prefetch=3,
            grid=(n_tiles,),
            in_specs=[pl.BlockSpec(memory_space=pl.ANY),
                      pl.BlockSpec((1, D_MODEL), fixed), pl.BlockSpec((1, D_MODEL), fixed),
                      wspec((D_MODEL, EXPERT_FF)), wspec((D_MODEL, EXPERT_FF)), wspec((EXPERT_FF, D_MODEL))],
            out_specs=pl.BlockSpec(memory_space=pl.ANY),
            scratch_shapes=[pltpu.VMEM((2, tm, ROW_W), f32), pltpu.VMEM((2, tm, D_MODEL), f32),
                            pltpu.SemaphoreType.DMA((2,)), pltpu.SemaphoreType.DMA((2,))]),
        out_shape=jax.ShapeDtypeStruct((n + (N_GROUPS + 1) * tm, D_MODEL), f32),
        compiler_params=pltpu.CompilerParams(dimension_semantics=("arbitrary",), vmem_limit_bytes=VMEM_LIMIT),
        name="moe",
    )(gather_ids, scatter_ids, tile_grp, hx, fnw, onw, wg_bf, wu_bf, wd_bf)


def _stacked_sample_queries(qd):
    bd, t = qd.shape[:2]
    q5 = qd.reshape(bd, t, N_HEADS, 2, DA_DK)
    eye = jnp.eye(2, dtype=qd.dtype)
    qz = q5[:, :, :, :, None, :] * eye[None, None, None, :, :, None]
    return qz.transpose(0, 2, 3, 1, 4, 5).reshape(bd, N_HEADS * 2 * t, HEAD_W)


def kernel(x_prompt, x_sample, cache_k, cache_v, state_hgrn, page_table, attn_norm_w, w_in, hgrn_lb, hgrn_norm_w,
           diff_lambda, diff_norm_w, w_o, ffn_norm_w, w_r1, b_r1, w_r2, b_r2, w_gate, w_up, w_down, final_norm_w):
    b, s = x_prompt.shape[:2]
    bd, t = x_sample.shape[:2]
    depth = w_in.shape[0]
    n_pages = page_table.shape[1]
    past_len = n_pages * PAGE_SIZE
    assert x_prompt.shape[2] == D_MODEL and w_in.shape[2] == N_SEG * SEG_W and t <= SAMPLE_PAD
    tp = SAMPLE_PAD

    tabs_p = _rope_tables(jnp.arange(s, dtype=jnp.int32))
    tabs_s = _rope_tables(jnp.tile(past_len + jnp.arange(t, dtype=jnp.int32), bd))
    p_lb = jax.nn.softmax(hgrn_lb.astype(f32), axis=0)
    lb_all = jnp.cumsum(p_lb, axis=0) - p_lb[0:1]

    w_in_bf = w_in.astype(bf16)
    w_o_bf = w_o.astype(bf16)
    wg_bf, wu_bf, wd_bf = w_gate.astype(bf16), w_up.astype(bf16), w_down.astype(bf16)
    pad_r = ROUTER_W - N_GROUPS - N_EXPERTS
    w_r = jnp.concatenate([w_r1, w_r2, jnp.zeros((depth, D_MODEL, pad_r), f32)], axis=2)
    w_r_hi = w_r.astype(bf16)
    w_r_lo = (w_r - w_r_hi.astype(f32)).astype(bf16)
    b_r = jnp.concatenate([b_r1, b_r2, jnp.zeros((depth, pad_r), f32)], axis=1)

    tm_p = 256 if (b * s) % 256 == 0 else b * s
    tq = 512 if s % 512 == 0 else s
    tb_p = 256 if s % 256 == 0 else s
    tm_moe_p = 512 if (b * s) % 512 == 0 else b * s
    n_s = bd * t
    n_pg = 8 if n_pages % 8 == 0 else 1
    q_scale = DA_DK ** -0.5 * LOG2E

    hp = x_prompt.reshape(b * s, D_MODEL)
    hs = x_sample.reshape(n_s, D_MODEL)
    zeros_state = jnp.zeros((b, N_HEADS, HEAD_W, HEAD_W), f32)
    kp_l, vp_l, sp_l, ks_l, vs_l, ss_l = [], [], [], [], [], []
    for l in range(depth):
        lam_init = 0.8 - 0.6 * math.exp(-0.3 * l)
        dl = diff_lambda[l].astype(f32)
        lam = (jnp.exp(jnp.sum(dl[0] * dl[1])) - jnp.exp(jnp.sum(dl[2] * dl[3])) + lam_init).reshape(1)
        lb = lb_all[l].reshape(1, SEG_W)
        nw = attn_norm_w[l].reshape(1, D_MODEL)
        fnw = ffn_norm_w[l].reshape(1, D_MODEL)
        onw = final_norm_w.reshape(1, D_MODEL)
        hnw = hgrn_norm_w[l].reshape(1, HEAD_W)
        dnw = diff_norm_w[l].reshape(1, HEAD_W)
        last = l == depth - 1

        def tail(o_h, o_d, g_h, h, tm_mix, tm_moe):
            hx = _mix(o_h, o_d, g_h, h, hnw, dnw, w_o_bf[l], fnw, w_r_hi[l], w_r_lo[l], b_r[l].reshape(1, ROUTER_W),
                      1.0 - lam_init, tm_mix)
            return _moe(hx, fnw, onw, wg_bf, wu_bf, wd_bf, l, tm_moe, last)

        qh, kh, ih, lf, gh, qd, kd, vd, kb, vb = _proj(hp, b * s, nw, w_in_bf[l], lb, tabs_p, tm_p, q_scale, seq=s)
        o_h, s_fin = _hgrn(qh, kh, ih, lf, zeros_state, s, tb_p, HG_CHUNK)
        o_d = _attn_prompt(lam, qd, kb, vb, b, s, tq)
        hp = tail(o_h, o_d, gh, hp, tm_p, tm_moe_p)
        kp_l.append(kd.reshape(b, s, N_HEADS, HEAD_W))
        vp_l.append(vd.reshape(b, s, N_HEADS, HEAD_W))
        sp_l.append(s_fin)

        qh, kh, ih, lf, gh, qd, kd, vd, kb, vb = _proj(hs, n_s, nw, w_in_bf[l], lb, tabs_s, n_s, q_scale)
        pad = lambda a: jnp.pad(a.reshape(bd, t, SEG_W), ((0, 0), (0, tp - t), (0, 0)))
        flat = lambda a: pad(a).reshape(bd * tp, SEG_W)
        o_h, s_fin = _hgrn(flat(qh), flat(kh), flat(ih), flat(lf), state_hgrn[l].astype(f32), tp, tp, tp)
        o_h = o_h.reshape(bd, tp, SEG_W)[:, :t].reshape(n_s, SEG_W)
        o_d = _attn_sample(page_table, lam, _stacked_sample_queries(pad(qd)), pad(kb), pad(vb),
                           cache_k, cache_v, l, n_pg)
        o_d = o_d[:, :t].reshape(n_s, SEG_W)
        hs = tail(o_h, o_d, gh, hs, n_s, n_s)
        ks_l.append(kd.reshape(bd, t, N_HEADS, HEAD_W))
        vs_l.append(vd.reshape(bd, t, N_HEADS, HEAD_W))
        ss_l.append(s_fin)

    return (hp[:b * s].reshape(b, s, D_MODEL), hs[:n_s].reshape(bd, t, D_MODEL), jnp.stack(kp_l), jnp.stack(vp_l),
            jnp.stack(sp_l), jnp.stack(ks_l), jnp.stack(vs_l), jnp.stack(ss_l))
```

```python
import functools
import math

import jax
import jax.numpy as jnp
from jax import lax
from jax.experimental import pallas as pl
from jax.experimental.pallas import tpu as pltpu

f32 = jnp.float32
bf16 = jnp.bfloat16

D_MODEL = 1024
N_HEADS = 4
HEAD_W = 128
SEG_W = N_HEADS * HEAD_W
N_SEG = 7
DA_DK = 64
ROT_DIM = DA_DK // 4
ROPE_THETA = 500000.0
PAGE_SIZE = 128
N_GROUPS = 4
EXPERTS_PER_GROUP = 4
N_EXPERTS = N_GROUPS * EXPERTS_PER_GROUP
EXPERT_FF = D_MODEL // 2
NORM_EPS = 1e-6
NEG = -1e30
F_MIN = 1e-30

HG_CHUNK = 64
LOG2E = math.log2(math.e)
SAMPLE_PAD = 8
GATE_LANE0 = N_GROUPS
ROUTER_W = 128
ROW_CHUNKS = (D_MODEL + ROUTER_W) // HEAD_W

VMEM_LIMIT = 48 * 1024 * 1024


def _dot(a, b):
    return jnp.dot(a, b, preferred_element_type=f32)


def _dot_nt(a, b):
    return lax.dot_general(a, b, (((1,), (1,)), ((), ())), preferred_element_type=f32)


def _dot_tn(a, b):
    return lax.dot_general(a, b, (((0,), (0,)), ((), ())), preferred_element_type=f32)


def _rms(x, w):
    return x * lax.rsqrt(jnp.mean(x * x, axis=-1, keepdims=True) + NORM_EPS) * w


def _split3(x):
    hi = x.astype(bf16)
    r1 = x - hi.astype(f32)
    mid = r1.astype(bf16)
    lo = (r1 - mid.astype(f32)).astype(bf16)
    return hi, mid, lo


def _proj_kernel(h_ref, nw_ref, w_ref, lb_ref, c_ref, sa_ref, sb_ref,
                 qh_ref, kh_ref, ih_ref, lf_ref, gh_ref, qd_ref, kd_ref, vd_ref, kb_ref, vb_ref, *, q_scale, transposed):
    tm = h_ref.shape[0]
    xn = _rms(h_ref[...], nw_ref[...]).astype(bf16)

    def seg(i):
        return _dot(xn, w_ref[:, i * SEG_W:(i + 1) * SEG_W])

    qh_ref[...] = seg(0) * (HEAD_W ** -0.5)
    hf = seg(1)
    lb = lb_ref[...]
    e = jnp.exp(-jnp.abs(hf))
    r = 1.0 / (1.0 + e)
    pos = hf >= 0.0
    sig = jnp.where(pos, r, e * r)
    nsig = jnp.where(pos, e * r, r)
    f = lb + (1.0 - lb) * sig
    lf_ref[...] = jnp.log(jnp.maximum(f, F_MIN)) * LOG2E
    kh_ref[...] = (1.0 - lb) * nsig
    ih_ref[...] = seg(2)
    gh_ref[...] = seg(3)

    c = c_ref[...]
    sa = sa_ref[...]
    sb = sb_ref[...]

    def rope(z, hh):
        zz = z[:, hh * HEAD_W:(hh + 1) * HEAD_W]
        return zz * c + pltpu.roll(zz, HEAD_W - ROT_DIM // 2, 1) * sa + pltpu.roll(zz, ROT_DIM // 2, 1) * sb

    zq = seg(4)
    zk = seg(5)
    for hh in range(N_HEADS):
        sl = slice(hh * HEAD_W, (hh + 1) * HEAD_W)
        qr = rope(zq, hh) * q_scale
        if transposed:
            qd_ref[sl, :] = qr.T.astype(bf16)
        else:
            qd_ref[:, sl] = qr.astype(bf16)
        kr = rope(zk, hh)
        kd_ref[pl.ds(hh, tm, stride=N_HEADS), :] = kr
        kb_ref[:, sl] = kr.astype(bf16)
    vd = seg(6)
    for hh in range(N_HEADS):
        vd_ref[pl.ds(hh, tm, stride=N_HEADS), :] = vd[:, hh * HEAD_W:(hh + 1) * HEAD_W]
    vb_ref[...] = vd.T.astype(bf16) if transposed else vd.astype(bf16)


def _proj(h, n, nw, w_bf, lb, tabs, tm, q_scale, seq=None):
    npos = tabs[0].shape[0] // tm
    row = lambda i: (i, 0)
    fixed = lambda i: (0, 0)
    tab = lambda i: (i % npos, 0)
    seg_f32 = jax.ShapeDtypeStruct((n, SEG_W), f32)
    seg_bf = jax.ShapeDtypeStruct((n, SEG_W), bf16)
    seg_spec = pl.BlockSpec((tm, SEG_W), row)
    rows_shape = jax.ShapeDtypeStruct((n * N_HEADS, HEAD_W), f32)
    rows_spec = pl.BlockSpec((tm * N_HEADS, HEAD_W), row)
    if seq is None:
        t_shape, t_spec = seg_bf, seg_spec
    else:
        nb = seq // tm
        t_shape = jax.ShapeDtypeStruct((n // seq, SEG_W, seq), bf16)
        t_spec = pl.BlockSpec((None, SEG_W, tm), lambda i: (i // nb, 0, i % nb))
    return pl.pallas_call(
        functools.partial(_proj_kernel, q_scale=q_scale, transposed=seq is not None),
        grid=(n // tm,),
        in_specs=[pl.BlockSpec((tm, D_MODEL), row),
                  pl.BlockSpec((1, D_MODEL), fixed),
                  pl.BlockSpec((D_MODEL, N_SEG * SEG_W), fixed),
                  pl.BlockSpec((1, SEG_W), fixed),
                  pl.BlockSpec((tm, HEAD_W), tab),
                  pl.BlockSpec((tm, HEAD_W), tab),
                  pl.BlockSpec((tm, HEAD_W), tab)],
        out_specs=[seg_spec] * 5 + [t_spec, rows_spec, rows_spec, seg_spec, t_spec],
        out_shape=[seg_f32] * 5 + [t_shape, rows_shape, rows_shape, seg_bf, t_shape],
        compiler_params=pltpu.CompilerParams(dimension_semantics=("arbitrary",), vmem_limit_bytes=VMEM_LIMIT),
        name="proj",
    )(h, nw, w_bf, lb, *tabs)


def _rope_tables(pos):
    half = ROT_DIM // 2
    inv = ROPE_THETA ** (-jnp.arange(0, ROT_DIM, 2, dtype=f32) / ROT_DIM)
    ang = pos.astype(f32)[:, None] * inv[None, :]
    cos, sin = jnp.cos(ang), jnp.sin(ang)
    t = pos.shape[0]
    rest = DA_DK - ROT_DIM
    c64 = jnp.concatenate([cos, cos, jnp.ones((t, rest), f32)], axis=1)
    sa64 = jnp.concatenate([-sin, jnp.zeros((t, half + rest), f32)], axis=1)
    sb64 = jnp.concatenate([jnp.zeros((t, half), f32), sin, jnp.zeros((t, rest), f32)], axis=1)
    return tuple(jnp.tile(a, (1, HEAD_W // DA_DK)) for a in (c64, sa64, sb64))


def _hgrn_chunk(q, k, v, lf, st, chunk, consts):
    sel, pair_masks, halves = consts
    heads = [slice(hh * HEAD_W, (hh + 1) * HEAD_W) for hh in range(N_HEADS)]
    terms = jnp.concatenate(_split3(lf), axis=1)
    gs3 = _dot(sel, terms)
    gs = gs3[:, :SEG_W] + gs3[:, SEG_W:2 * SEG_W] + gs3[:, 2 * SEG_W:]
    g = gs[:chunk]
    g_last = g[chunk - 1:chunk, :]
    qx = (q * jnp.exp2(g)).astype(bf16)
    q_bf, k_bf, v_bf = q.astype(bf16), k.astype(bf16), v.astype(bf16)
    st_bf = [s.astype(bf16) for s in st]
    o = [_dot_nt(qx[:, sl], st_bf[hh]) for hh, sl in enumerate(heads)]
    a = [_dot_nt(q_bf[:, sl], k_bf[:, sl]) * pair_masks[0] for sl in heads]
    n_mm = 1
    for lvl, h in enumerate(halves, start=1):
        if h % 8 == 0:
            ref = jnp.concatenate([jnp.broadcast_to(g[r0 + h - 1:r0 + h, :], (2 * h, SEG_W))
                                   for r0 in range(0, chunk, 2 * h)], axis=0)
        else:
            ref = gs[n_mm * chunk:(n_mm + 1) * chunk]
            n_mm += 1
        e = jnp.exp2(-jnp.abs(g - ref))
        qe, ke = (q * e).astype(bf16), (k * e).astype(bf16)
        a = [a[hh] + _dot_nt(qe[:, sl], ke[:, sl]) * pair_masks[lvl] for hh, sl in enumerate(heads)]
    kdec = (k * jnp.exp2(g_last - g)).astype(bf16)
    decay = jnp.exp2(g_last)
    o = jnp.concatenate([o[hh] + _dot(a[hh].astype(bf16), v_bf[:, sl]) for hh, sl in enumerate(heads)], axis=1)
    st_new = [st[hh] * decay[:, sl] + _dot_tn(v_bf[:, sl], kdec[:, sl]) for hh, sl in enumerate(heads)]
    return o, st_new


def _hgrn_consts(chunk):
    r = lax.broadcasted_iota(jnp.int32, (chunk, chunk), 0)
    c = lax.broadcasted_iota(jnp.int32, (chunk, chunk), 1)
    sels = [jnp.where(c <= r, 1.0, 0.0)]
    masks = [jnp.where(c == r, 1.0, 0.0)]
    halves = []
    h = chunk // 2
    while h >= 1:
        blk = -(2 * h)
        if h % 8:
            sels.append(jnp.where(c <= (r & blk) + (h - 1), 1.0, 0.0))
        same_block = (r & blk) == (c & blk)
        masks.append(jnp.where(same_block & ((r & h) != 0) & ((c & h) == 0), 1.0, 0.0))
        halves.append(h)
        h //= 2
    return jnp.concatenate(sels, axis=0).astype(bf16), masks, halves


def _hgrn_kernel(q_ref, k_ref, v_ref, lf_ref, s0_ref, o_ref, sfin_ref, st_ref, *, chunk, n_chunks):
    j = pl.program_id(1)
    consts = _hgrn_consts(chunk)

    @pl.when(j == 0)
    def _():
        for hh in range(N_HEADS):
            st_ref[hh] = s0_ref[hh].T

    def body(c, carry):
        rows = pl.ds(pl.multiple_of(c * chunk, chunk), chunk)
        o, st_new = _hgrn_chunk(q_ref[rows, :], k_ref[rows, :], v_ref[rows, :], lf_ref[rows, :],
                                [st_ref[hh] for hh in range(N_HEADS)], chunk, consts)
        o_ref[rows, :] = o
        for hh in range(N_HEADS):
            st_ref[hh] = st_new[hh]
        return carry

    lax.fori_loop(0, n_chunks, body, 0)

    @pl.when(j == pl.num_programs(1) - 1)
    def _():
        for hh in range(N_HEADS):
            sfin_ref[hh] = st_ref[hh].T


def _hgrn(q, k, v, lf, s0, t, tb, chunk):
    n = q.shape[0]
    b = n // t
    nj = t // tb
    tok = pl.BlockSpec((tb, SEG_W), lambda bi, j: (bi * nj + j, 0))
    st = pl.BlockSpec((None, N_HEADS, HEAD_W, HEAD_W), lambda bi, j: (bi, 0, 0, 0))
    return pl.pallas_call(
        functools.partial(_hgrn_kernel, chunk=chunk,n_chunks=tb // chunk),
        grid=(b, nj),
        in_specs=[tok, tok, tok, tok, st],
        out_specs=[tok, st],
        out_shape=[jax.ShapeDtypeStruct((n, SEG_W), f32), jax.ShapeDtypeStruct(s0.shape, f32)],
        scratch_shapes=[pltpu.VMEM((N_HEADS, HEAD_W, HEAD_W), f32)],
        compiler_params=pltpu.CompilerParams(dimension_semantics=("arbitrary", "arbitrary"),
                                             vmem_limit_bytes=VMEM_LIMIT),
        name="hgrn",
    )(q, k, v, lf, s0)


def _stack_maps(q):
    lane = lax.broadcasted_iota(jnp.int32, q.shape, 1)
    zero = jnp.zeros_like(q)
    return jnp.concatenate([jnp.where(lane < DA_DK, q, zero), jnp.where(lane >= DA_DK, q, zero)], axis=0)


def _softmax_step(s, v_bf, m_ref, l_ref, acc_ref, rows=None):
    sl = slice(None) if rows is None else rows
    m_prev = m_ref[sl, :]
    m_new = jnp.maximum(m_prev, jnp.max(s, axis=-1, keepdims=True))
    alpha = jnp.exp2(m_prev - m_new)
    p = jnp.exp2(s - m_new)
    l_ref[sl, :] = alpha * l_ref[sl, :] + jnp.sum(p, axis=-1, keepdims=True)
    acc_ref[sl, :] = alpha * acc_ref[sl, :] + _dot(p.astype(bf16), v_bf)
    m_ref[sl, :] = m_new


def _attn_kernel(lam_ref, qt_ref, k_ref, vt_ref, o_ref, m_ref, l_ref, acc_ref, *, tq):
    i = pl.program_id(2)
    qt = qt_ref[...]
    sub = lax.broadcasted_iota(jnp.int32, qt.shape, 0)
    zero = jnp.zeros_like(qt)
    qs = jnp.concatenate([jnp.where(sub < DA_DK, qt, zero), jnp.where(sub >= DA_DK, qt, zero)], axis=1)
    m_ref[...] = jnp.full(m_ref.shape, NEG, f32)
    l_ref[...] = jnp.zeros(l_ref.shape, f32)
    acc_ref[...] = jnp.zeros(acc_ref.shape, f32)

    def step(j, masked):
        r = pl.multiple_of(j * tq, tq)
        s = _dot(k_ref[pl.ds(r, tq), :], qs)
        if masked:
            key = lax.broadcasted_iota(jnp.int32, s.shape, 0)
            qry = lax.broadcasted_iota(jnp.int32, s.shape, 1) & (tq - 1)
            s = jnp.where(key <= qry, s, NEG)
        m_prev = m_ref[...]
        m_new = jnp.maximum(m_prev, jnp.max(s, axis=0, keepdims=True))
        alpha = jnp.exp2(m_prev - m_new)
        p = jnp.exp2(s - m_new)
        l_ref[...] = alpha * l_ref[...] + jnp.sum(p, axis=0, keepdims=True)
        acc_ref[...] = alpha * acc_ref[...] + _dot(vt_ref[:, pl.ds(r, tq)], p.astype(bf16))
        m_ref[...] = m_new

    def body(j, carry):
        step(j, False)
        return carry

    lax.fori_loop(0, i, body, 0)
    step(i, True)
    o = acc_ref[...] / l_ref[...]
    o_ref[...] = (o[:, :tq] - lam_ref[0] * o[:, tq:]).T


def _attn_prompt(lam, qt_bf, k_bf, vt_bf, b, s, tq):
    k3 = k_bf.reshape(b, s, SEG_W)
    qspec = pl.BlockSpec((None, HEAD_W, tq), lambda bi, h, i: (bi, h, i))
    kspec = pl.BlockSpec((None, s, HEAD_W), lambda bi, h, i: (bi, 0, h))
    vspec = pl.BlockSpec((None, HEAD_W, s), lambda bi, h, i: (bi, h, 0))
    out = pl.pallas_call(
        functools.partial(_attn_kernel, tq=tq),
        grid=(b, N_HEADS, s // tq),
        in_specs=[pl.BlockSpec(memory_space=pltpu.SMEM), qspec, kspec, vspec],
        out_specs=pl.BlockSpec((None, tq, HEAD_W), lambda bi, h, i: (bi, i, h)),
        out_shape=jax.ShapeDtypeStruct((b, s, SEG_W), f32),
        scratch_shapes=[pltpu.VMEM((1, 2 * tq), f32), pltpu.VMEM((1, 2 * tq), f32),
                        pltpu.VMEM((HEAD_W, 2 * tq), f32)],
        compiler_params=pltpu.CompilerParams(dimension_semantics=("arbitrary",) * 3, vmem_limit_bytes=VMEM_LIMIT),
        name="attn_prompt",
    )(lam, qt_bf, k3, vt_bf)
    return out.reshape(b * s, SEG_W)


def _attn_sample_kernel(pt_ref, lam_ref, qs_ref, kn_ref, vn_ref, *rest, n_pg, layer):
    del pt_ref, layer
    k_refs = rest[:n_pg]
    v_refs = rest[n_pg:2 * n_pg]
    o_ref, m_ref, l_ref, acc_ref = rest[2 * n_pg:]
    j = pl.program_id(1)
    t = SAMPLE_PAD

    @pl.when(j == 0)
    def _():
        m_ref[...] = jnp.full(m_ref.shape, NEG, f32)
        l_ref[...] = jnp.zeros(l_ref.shape, f32)
        acc_ref[...] = jnp.zeros(acc_ref.shape, f32)

    head_rows = [slice(hh * 2 * t, (hh + 1) * 2 * t) for hh in range(N_HEADS)]
    head_toks = [pl.ds(hh, PAGE_SIZE, stride=N_HEADS) for hh in range(N_HEADS)]
    qs = qs_ref[...]
    s = [_dot_nt(qs[rows, :], jnp.concatenate([kr[toks, :].astype(bf16) for kr in k_refs], axis=0))
         for rows, toks in zip(head_rows, head_toks)]
    m_prev = m_ref[...]
    m_new = jnp.maximum(m_prev, jnp.concatenate([jnp.max(sh, axis=-1, keepdims=True) for sh in s], axis=0))
    alpha = jnp.exp2(m_prev - m_new)
    p = [jnp.exp2(sh - m_new[rows, :]) for sh, rows in zip(s, head_rows)]
    l_ref[...] = alpha * l_ref[...] + jnp.concatenate([jnp.sum(ph, axis=-1, keepdims=True) for ph in p], axis=0)
    pv = [_dot(ph.astype(bf16), jnp.concatenate([vr[toks, :].astype(bf16) for vr in v_refs], axis=0))
          for ph, toks in zip(p, head_toks)]
    acc_ref[...] = alpha * acc_ref[...] + jnp.concatenate(pv, axis=0)
    m_ref[...] = m_new

    @pl.when(j == pl.num_programs(1) - 1)
    def _():
        for hh in range(N_HEADS):
            rows = slice(hh * 2 * t, (hh + 1) * 2 * t)
            sl = slice(hh * HEAD_W, (hh + 1) * HEAD_W)
            s = _dot_nt(qs_ref[rows, :], kn_ref[:, sl].astype(bf16))
            row = lax.broadcasted_iota(jnp.int32, s.shape, 0) & (t - 1)
            col = lax.broadcasted_iota(jnp.int32, s.shape, 1)
            s = jnp.where(col <= row, s, NEG)
            _softmax_step(s, vn_ref[:, sl].astype(bf16), m_ref, l_ref, acc_ref, rows)
            o = acc_ref[rows, :] / l_ref[rows, :]
            o_ref[:, sl] = o[:t] - lam_ref[0] * o[t:]


def _attn_sample(page_table, lam, qs, kn, vn, cache_k, cache_v, layer, n_pg):
    bd, n_pages = page_table.shape
    t = SAMPLE_PAD

    cache_k = cache_k.reshape(cache_k.shape[:2] + (PAGE_SIZE * N_HEADS, HEAD_W))
    cache_v = cache_v.reshape(cache_v.shape[:2] + (PAGE_SIZE * N_HEADS, HEAD_W))

    def page_spec(p):
        return pl.BlockSpec((None, None, PAGE_SIZE * N_HEADS, HEAD_W),
                            lambda bi, j, pt: (layer, pt[bi * n_pages + j * n_pg + p], 0, 0))

    per_b = lambda shape: pl.BlockSpec((None,) + shape, lambda bi, j, pt: (bi, 0, 0))
    pages = [page_spec(p) for p in range(n_pg)]
    return pl.pallas_call(
        functools.partial(_attn_sample_kernel, n_pg=n_pg, layer=layer),
        grid_spec=pltpu.PrefetchScalarGridSpec(
            num_scalar_prefetch=1,
            grid=(bd, n_pages // n_pg),
            in_specs=[pl.BlockSpec(memory_space=pltpu.SMEM), per_b((2 * t * N_HEADS, HEAD_W)),
                      per_b((t, SEG_W)), per_b((t, SEG_W))] + pages + pages,
            out_specs=per_b((t, SEG_W)),
            scratch_shapes=[pltpu.VMEM((2 * t * N_HEADS, 1), f32), pltpu.VMEM((2 * t * N_HEADS, 1), f32),
                            pltpu.VMEM((2 * t * N_HEADS, HEAD_W), f32)]),
        out_shape=jax.ShapeDtypeStruct((bd, t, SEG_W), f32),
        compiler_params=pltpu.CompilerParams(dimension_semantics=("arbitrary", "arbitrary"),
                                             vmem_limit_bytes=VMEM_LIMIT),
        name="attn_sample",
    )(page_table.reshape(-1), lam, qs, kn, vn, *([cache_k] * n_pg), *([cache_v] * n_pg))


def _route(logits):
    lane = lax.broadcasted_iota(jnp.int32, logits.shape, 1).astype(f32)
    big = float(ROUTER_W)
    is_g = lane < N_GROUPS
    m1 = jnp.max(jnp.where(is_g, logits, -jnp.inf), axis=-1, keepdims=True)
    grp = jnp.min(jnp.where(is_g & (logits == m1), lane, big), axis=-1, keepdims=True)
    p_grp = 1.0 / jnp.sum(jnp.where(is_g, jnp.exp(logits - m1), 0.0), axis=-1, keepdims=True)
    lo = GATE_LANE0 + EXPERTS_PER_GROUP * grp
    in_g = (lane >= lo) & (lane < lo + EXPERTS_PER_GROUP)
    v1 = jnp.max(jnp.where(in_g, logits, -jnp.inf), axis=-1, keepdims=True)
    i1 = jnp.min(jnp.where(in_g & (logits == v1), lane, big), axis=-1, keepdims=True)
    rest = in_g & (lane != i1)
    v2 = jnp.max(jnp.where(rest, logits, -jnp.inf), axis=-1, keepdims=True)
    i2 = jnp.min(jnp.where(rest & (logits == v2), lane, big), axis=-1, keepdims=True)
    e = jnp.exp(v2 - v1)
    w1 = 1.0 / (1.0 + e)
    w2 = e / (1.0 + e)
    gates = jnp.where(lane == i1, p_grp * w1, 0.0) + jnp.where(lane == i2, p_grp * w2, 0.0)
    return jnp.where(lane == 0.0, grp, gates)


def _mix_kernel(oh_ref, od_ref, gh_ref, h_ref, hnw_ref, dnw_ref, wo_ref, fnw_ref, wrh_ref, wrl_ref, br_ref,
                hx_ref, *, od_scale):
    parts = []
    for hh in range(N_HEADS):
        sl = slice(hh * HEAD_W, (hh + 1) * HEAD_W)
        gate = 1.0 / (1.0 + jnp.exp(-gh_ref[:, sl]))
        parts.append((_rms(oh_ref[:, sl], hnw_ref[...]) * gate).astype(bf16))
    for hh in range(N_HEADS):
        sl = slice(hh * HEAD_W, (hh + 1) * HEAD_W)
        parts.append((_rms(od_ref[:, sl], dnw_ref[...]) * od_scale).astype(bf16))
    h2 = h_ref[...] + _dot(jnp.concatenate(parts, axis=1), wo_ref[...])
    xn = _rms(h2, fnw_ref[...])
    x_hi = xn.astype(bf16)
    x_lo = (xn - x_hi.astype(f32)).astype(bf16)
    logits = _dot(x_hi, wrh_ref[...]) + _dot(x_lo, wrh_ref[...]) + _dot(x_hi, wrl_ref[...]) + br_ref[...]
    tm = h2.shape[0]
    for c in range(D_MODEL // HEAD_W):
        hx_ref[pl.ds(c, tm, stride=ROW_CHUNKS), :] = h2[:, c * HEAD_W:(c + 1) * HEAD_W]
    hx_ref[pl.ds(D_MODEL // HEAD_W, tm, stride=ROW_CHUNKS), :] = _route(logits)


def _mix(o_h, o_d, g_h, h, hnw, dnw, wo_bf, fnw, wr_hi, wr_lo, br, od_scale, tm):
    n = o_h.shape[0]
    row = lambda i: (i, 0)
    fixed = lambda i: (0, 0)
    seg = pl.BlockSpec((tm, SEG_W), row)
    return pl.pallas_call(
        functools.partial(_mix_kernel, od_scale=od_scale),
        grid=(n // tm,),
        in_specs=[seg, seg, seg, pl.BlockSpec((tm, D_MODEL), row),
                  pl.BlockSpec((1, HEAD_W), fixed), pl.BlockSpec((1, HEAD_W), fixed),
                  pl.BlockSpec((D_MODEL, D_MODEL), fixed), pl.BlockSpec((1, D_MODEL), fixed),
                  pl.BlockSpec((D_MODEL, ROUTER_W), fixed), pl.BlockSpec((D_MODEL, ROUTER_W), fixed),
                  pl.BlockSpec((1, ROUTER_W), fixed)],
        out_specs=pl.BlockSpec((tm * ROW_CHUNKS, HEAD_W), row),
        out_shape=jax.ShapeDtypeStruct((n * ROW_CHUNKS, HEAD_W), f32),
        compiler_params=pltpu.CompilerParams(dimension_semantics=("arbitrary",), vmem_limit_bytes=VMEM_LIMIT),
        name="mix",
    )(o_h, o_d, g_h, h, hnw, dnw, wo_bf, fnw, wr_hi, wr_lo, br)


def _moe_kernel(ids_ref, grp_ref, cnt_ref, hx_hbm, fnw_ref, onw_ref, wg_ref, wu_ref, wd_ref, out_hbm,
                xbuf, ybuf, gsem, ssem, *, tm, final_norm):
    i = pl.program_id(0)
    n_tiles = pl.num_programs(0)
    slot = i % 2

    def gather_copy(tile, sl, r):
        tok = ids_ref[tile * tm + r]
        return pltpu.make_async_copy(hx_hbm.at[pl.ds(tok * ROW_CHUNKS, ROW_CHUNKS)],
                                     xbuf.at[sl, pl.ds(r * ROW_CHUNKS, ROW_CHUNKS)], gsem.at[sl])

    def scatter_copy(tile, sl, r):
        tok = ids_ref[tile * tm + r]
        return pltpu.make_async_copy(ybuf.at[sl, pl.ds(r, 1)], out_hbm.at[pl.ds(tok, 1)], ssem.at[sl])

    def for_rows(n, fn):
        def body(r, carry):
            fn(r)
            return carry
        if isinstance(n, int):
            lax.fori_loop(0, n, body, 0, unroll=8)
        else:
            lax.fori_loop(0, n, body, 0)

    def gather_wait(sl):
        pltpu.make_async_copy(hx_hbm.at[pl.ds(0, tm * ROW_CHUNKS)], xbuf.at[sl], gsem.at[sl]).wait()

    def scatter_start(tile, sl):
        cnt = cnt_ref[tile]

        @pl.when(cnt == tm)
        def _():
            for_rows(tm, lambda r: scatter_copy(tile, sl, r).start())

        @pl.when(cnt < tm)
        def _():
            for_rows(cnt, lambda r: scatter_copy(tile, sl, r).start())

    def scatter_wait(tile, sl):
        cnt = cnt_ref[tile]

        @pl.when(cnt == tm)
        def _():
            pltpu.make_async_copy(ybuf.at[sl], out_hbm.at[pl.ds(0, tm)], ssem.at[sl]).wait()

        @pl.when(cnt < tm)
        def _():
            for_rows(cnt, lambda r: scatter_copy(tile, sl, r).wait())

    @pl.when(i == 0)
    def _():
        for_rows(tm, lambda r: gather_copy(0, 0, r).start())

    @pl.when(i + 1 < n_tiles)
    def _():
        for_rows(tm, lambda r: gather_copy(i + 1, 1 - slot, r).start())

    gather_wait(slot)

    @pl.when(i >= 2)
    def _():
        scatter_wait(i - 2, slot)

    xs = xbuf.at[slot]
    chunk = lambda c: xs[pl.ds(c, tm, stride=ROW_CHUNKS), :]
    h2 = jnp.concatenate([chunk(c) for c in range(D_MODEL // HEAD_W)], axis=1)
    router = chunk(D_MODEL // HEAD_W)
    xn = _rms(h2, fnw_ref[...]).astype(bf16)
    lane = lax.broadcasted_iota(jnp.int32, router.shape, 1)
    lane0 = GATE_LANE0 + EXPERTS_PER_GROUP * grp_ref[i]
    acc = h2
    for e in range(EXPERTS_PER_GROUP):
        gate = jnp.sum(jnp.where(lane == lane0 + e, router, 0.0), axis=-1, keepdims=True)
        a = _dot(xn, wg_ref[e])
        hid = a / (1.0 + jnp.exp(-a)) * _dot(xn, wu_ref[e])
        acc = acc + _dot((hid * gate).astype(bf16), wd_ref[e])
    if final_norm:
        acc = _rms(acc, onw_ref[...])
    ybuf[slot] = acc

    scatter_start(i, slot)

    @pl.when(i == n_tiles - 1)
    def _():
        @pl.when(i >= 1)
        def _():
            scatter_wait(i - 1, 1 - slot)
        scatter_wait(i, slot)


def _dispatch(grp_f32, tm):
    n = grp_f32.shape[0]
    n_tiles = n // tm + N_GROUPS
    grp = grp_f32.astype(jnp.int32)
    onehot = (grp[:, None] == jnp.arange(N_GROUPS, dtype=jnp.int32)[None, :]).astype(jnp.int32)
    csum = jnp.cumsum(onehot, axis=0)
    cnt = csum[-1]
    rank = jnp.sum(csum * onehot, axis=1) - 1
    tiles_g = (cnt + tm - 1) // tm
    tile_end = jnp.cumsum(tiles_g)
    tile_start = tile_end - tiles_g
    pos = jnp.sum(onehot * (tile_start * tm)[None, :], axis=1) + rank
    tile = jnp.arange(n_tiles, dtype=jnp.int32)
    tile_grp_raw = jnp.sum((tile[:, None] >= tile_end[None, :]).astype(jnp.int32), axis=1)
    last_grp = jnp.max(jnp.where(cnt > 0, jnp.arange(N_GROUPS, dtype=jnp.int32), 0))
    tile_grp = jnp.minimum(tile_grp_raw, last_grp)
    in_range = tile_grp_raw < N_GROUPS
    rows_left = cnt[tile_grp] - (tile - tile_start[tile_grp]) * tm
    tile_cnt = jnp.where(in_range, jnp.clip(rows_left, 0, tm), 0).astype(jnp.int32)
    ids = jnp.zeros((n_tiles * tm,), jnp.int32).at[pos].set(jnp.arange(n, dtype=jnp.int32))
    return ids, tile_grp, tile_cnt


def _moe(hx, fnw, onw, wg_bf, wu_bf, wd_bf, layer, tm, final_norm):
    n = hx.shape[0] // ROW_CHUNKS
    ids, tile_grp, tile_cnt = _dispatch(hx[D_MODEL // HEAD_W::ROW_CHUNKS, 0], tm)
    n_tiles = tile_grp.shape[0]
    fixed = lambda i, ids_r, grp_r, cnt_r: (0, 0)
    wspec = lambda shape: pl.BlockSpec((None, EXPERTS_PER_GROUP) + shape,
                                       lambda i, ids_r, grp_r, cnt_r: (layer, grp_r[i], 0, 0))
    return pl.pallas_call(
        functools.partial(_moe_kernel, tm=tm, final_norm=final_norm),
        grid_spec=pltpu.PrefetchScalarGridSpec(
            num_scalar_prefetch=3,
            grid=(n_tiles,),
            in_specs=[pl.BlockSpec(memory_space=pl.ANY),
                      pl.BlockSpec((1, D_MODEL), fixed), pl.BlockSpec((1, D_MODEL), fixed),
                      wspec((D_MODEL, EXPERT_FF)), wspec((D_MODEL, EXPERT_FF)), wspec((EXPERT_FF, D_MODEL))],
            out_specs=pl.BlockSpec(memory_space=pl.ANY),
            scratch_shapes=[pltpu.VMEM((2, tm * ROW_CHUNKS, HEAD_W), f32), pltpu.VMEM((2, tm, D_MODEL), f32),
                            pltpu.SemaphoreType.DMA((2,)), pltpu.SemaphoreType.DMA((2,))]),
        out_shape=jax.ShapeDtypeStruct((n, D_MODEL), f32),
        compiler_params=pltpu.CompilerParams(dimension_semantics=("arbitrary",), vmem_limit_bytes=VMEM_LIMIT),
        name="moe",
    )(ids, tile_grp, tile_cnt, hx, fnw, onw, wg_bf, wu_bf, wd_bf)


def _stacked_sample_queries(qd):
    bd, t = qd.shape[:2]
    q5 = qd.reshape(bd, t, N_HEADS, 2, DA_DK)
    eye = jnp.eye(2, dtype=qd.dtype)
    qz = q5[:, :, :, :, None, :] * eye[None, None, None, :, :, None]
    return qz.transpose(0, 2, 3, 1, 4, 5).reshape(bd, N_HEADS * 2 * t, HEAD_W)


def kernel(x_prompt, x_sample, cache_k, cache_v, state_hgrn, page_table, attn_norm_w, w_in, hgrn_lb, hgrn_norm_w,
           diff_lambda, diff_norm_w, w_o, ffn_norm_w, w_r1, b_r1, w_r2, b_r2, w_gate, w_up, w_down, final_norm_w):
    b, s = x_prompt.shape[:2]
    bd, t = x_sample.shape[:2]
    depth = w_in.shape[0]
    n_pages = page_table.shape[1]
    past_len = n_pages * PAGE_SIZE
    assert x_prompt.shape[2] == D_MODEL and w_in.shape[2] == N_SEG * SEG_W and t <= SAMPLE_PAD
    tp = SAMPLE_PAD

    tabs_p = _rope_tables(jnp.arange(s, dtype=jnp.int32))
    tabs_s = _rope_tables(jnp.tile(past_len + jnp.arange(t, dtype=jnp.int32), bd))
    p_lb = jax.nn.softmax(hgrn_lb.astype(f32), axis=0)
    lb_all = jnp.cumsum(p_lb, axis=0) - p_lb[0:1]

    w_in_bf = w_in.astype(bf16)
    w_o_bf = w_o.astype(bf16)
    wg_bf, wu_bf, wd_bf = w_gate.astype(bf16), w_up.astype(bf16), w_down.astype(bf16)
    pad_r = ROUTER_W - N_GROUPS - N_EXPERTS
    w_r = jnp.concatenate([w_r1, w_r2, jnp.zeros((depth, D_MODEL, pad_r), f32)], axis=2)
    w_r_hi = w_r.astype(bf16)
    w_r_lo = (w_r - w_r_hi.astype(f32)).astype(bf16)
    b_r = jnp.concatenate([b_r1, b_r2, jnp.zeros((depth, pad_r), f32)], axis=1)

    tm_p = 256 if (b * s) % 256 == 0 else b * s
    tq = 512 if s % 512 == 0 else s
    tb_p = 512 if s % 512 == 0 else s
    tm_moe_p = 512 if (b * s) % 512 == 0 else b * s
    n_s = bd * t
    n_pg = next(p for p in (16, 8, 4, 2, 1) if n_pages % p == 0)
    tm_proj = 512 if s % 512 == 0 else tm_p
    q_scale = DA_DK ** -0.5 * LOG2E

    hp = x_prompt.reshape(b * s, D_MODEL)
    hs = x_sample.reshape(n_s, D_MODEL)
    zeros_state = jnp.zeros((b, N_HEADS, HEAD_W, HEAD_W), f32)
    kp_l, vp_l, sp_l, ks_l, vs_l, ss_l = [], [], [], [], [], []
    for l in range(depth):
        lam_init = 0.8 - 0.6 * math.exp(-0.3 * l)
        dl = diff_lambda[l].astype(f32)
        lam = (jnp.exp(jnp.sum(dl[0] * dl[1])) - jnp.exp(jnp.sum(dl[2] * dl[3])) + lam_init).reshape(1)
        lb = lb_all[l].reshape(1, SEG_W)
        nw = attn_norm_w[l].reshape(1, D_MODEL)
        fnw = ffn_norm_w[l].reshape(1, D_MODEL)
        onw = final_norm_w.reshape(1, D_MODEL)
        hnw = hgrn_norm_w[l].reshape(1, HEAD_W)
        dnw = diff_norm_w[l].reshape(1, HEAD_W)
        last = l == depth - 1

        def tail(o_h, o_d, g_h, h, tm_mix, tm_moe):
            hx = _mix(o_h, o_d, g_h, h, hnw, dnw, w_o_bf[l], fnw, w_r_hi[l], w_r_lo[l], b_r[l].reshape(1, ROUTER_W),
                      1.0 - lam_init, tm_mix)
            return _moe(hx, fnw, onw, wg_bf, wu_bf, wd_bf, l, tm_moe, last)

        qh, kh, ih, lf, gh, qd, kd, vd, kb, vb = _proj(hp, b * s, nw, w_in_bf[l], lb, tabs_p, tm_proj, q_scale, seq=s)
        o_h, s_fin = _hgrn(qh, kh, ih, lf, zeros_state, s, tb_p, HG_CHUNK)
        o_d = _attn_prompt(lam, qd, kb, vb, b, s, tq)
        hp = tail(o_h, o_d, gh, hp, tm_p, tm_moe_p)
        kp_l.append(kd.reshape(b, s, N_HEADS, HEAD_W))
        vp_l.append(vd.reshape(b, s, N_HEADS, HEAD_W))
        sp_l.append(s_fin)

        qh, kh, ih, lf, gh, qd, kd, vd, kb, vb = _proj(hs, n_s, nw, w_in_bf[l], lb, tabs_s, n_s, q_scale)
        pad = lambda a: jnp.pad(a.reshape(bd, t, SEG_W), ((0, 0), (0, tp - t), (0, 0)))
        flat = lambda a: pad(a).reshape(bd * tp, SEG_W)
        o_h, s_fin = _hgrn(flat(qh), flat(kh), flat(ih), flat(lf), state_hgrn[l].astype(f32), tp, tp, tp)
        o_h = o_h.reshape(bd, tp, SEG_W)[:, :t].reshape(n_s, SEG_W)
        o_d = _attn_sample(page_table, lam, _stacked_sample_queries(pad(qd)), pad(kb), pad(vb),
                           cache_k, cache_v, l, n_pg)
        o_d = o_d[:, :t].reshape(n_s, SEG_W)
        hs = tail(o_h, o_d, gh, hs, n_s, n_s)
        ks_l.append(kd.reshape(bd, t, N_HEADS, HEAD_W))
        vs_l.append(vd.reshape(bd, t, N_HEADS, HEAD_W))
        ss_l.append(s_fin)

    return (hp.reshape(b, s, D_MODEL), hs.reshape(bd, t, D_MODEL), jnp.stack(kp_l), jnp.stack(vp_l),
            jnp.stack(sp_l), jnp.stack(ks_l), jnp.stack(vs_l), jnp.stack(ss_l))
```

```python
import functools
import math

import jax
import jax.numpy as jnp
from jax import lax
from jax.experimental import pallas as pl
from jax.experimental.pallas import tpu as pltpu

f32 = jnp.float32
bf16 = jnp.bfloat16

D_MODEL = 1024
N_HEADS = 4
HEAD_W = 128
SEG_W = N_HEADS * HEAD_W
N_SEG = 7
DA_DK = 64
ROT_DIM = DA_DK // 4
ROPE_THETA = 500000.0
PAGE_SIZE = 128
N_GROUPS = 4
EXPERTS_PER_GROUP = 4
N_EXPERTS = N_GROUPS * EXPERTS_PER_GROUP
EXPERT_FF = D_MODEL // 2
NORM_EPS = 1e-6
NEG = -1e30
F_MIN = 1e-30

HG_CHUNK = 64
LOG2E = math.log2(math.e)
SAMPLE_PAD = 8
GATE_LANE0 = N_GROUPS
ROUTER_W = 128
ROW_CHUNKS = (D_MODEL + ROUTER_W) // HEAD_W

VMEM_LIMIT = 48 * 1024 * 1024


def _dot(a, b):
    return jnp.dot(a, b, preferred_element_type=f32)


def _dot_nt(a, b):
    return lax.dot_general(a, b, (((1,), (1,)), ((), ())), preferred_element_type=f32)


def _dot_tn(a, b):
    return lax.dot_general(a, b, (((0,), (0,)), ((), ())), preferred_element_type=f32)


def _rms(x, w):
    return x * lax.rsqrt(jnp.mean(x * x, axis=-1, keepdims=True) + NORM_EPS) * w


def _split3(x):
    hi = x.astype(bf16)
    r1 = x - hi.astype(f32)
    mid = r1.astype(bf16)
    lo = (r1 - mid.astype(f32)).astype(bf16)
    return hi, mid, lo


def _proj_kernel(h_ref, nw_ref, w_ref, lb_ref, c_ref, sa_ref, sb_ref,
                 qh_ref, kh_ref, ih_ref, lf_ref, gh_ref, qd_ref, kd_ref, vd_ref, kb_ref, vb_ref, *, q_scale, transposed):
    tm = h_ref.shape[0]
    xn = _rms(h_ref[...], nw_ref[...]).astype(bf16)

    def seg(i):
        return _dot(xn, w_ref[:, i * SEG_W:(i + 1) * SEG_W])

    qh_ref[...] = seg(0) * (HEAD_W ** -0.5)
    hf = seg(1)
    lb = lb_ref[...]
    e = jnp.exp(-jnp.abs(hf))
    r = 1.0 / (1.0 + e)
    pos = hf >= 0.0
    sig = jnp.where(pos, r, e * r)
    nsig = jnp.where(pos, e * r, r)
    f = lb + (1.0 - lb) * sig
    lf_ref[...] = jnp.log(jnp.maximum(f, F_MIN)) * LOG2E
    kh_ref[...] = (1.0 - lb) * nsig
    ih_ref[...] = seg(2)
    gh_ref[...] = seg(3)

    c = c_ref[...]
    sa = sa_ref[...]
    sb = sb_ref[...]

    def rope(z, hh):
        zz = z[:, hh * HEAD_W:(hh + 1) * HEAD_W]
        return zz * c + pltpu.roll(zz, HEAD_W - ROT_DIM // 2, 1) * sa + pltpu.roll(zz, ROT_DIM // 2, 1) * sb

    zq = seg(4)
    zk = seg(5)
    for hh in range(N_HEADS):
        sl = slice(hh * HEAD_W, (hh + 1) * HEAD_W)
        qr = rope(zq, hh) * q_scale
        if transposed:
            qd_ref[sl, :] = qr.T.astype(bf16)
        else:
            qd_ref[:, sl] = qr.astype(bf16)
        kr = rope(zk, hh)
        kd_ref[pl.ds(hh, tm, stride=N_HEADS), :] = kr
        kb_ref[:, sl] = kr.astype(bf16)
    vd = seg(6)
    for hh in range(N_HEADS):
        vd_ref[pl.ds(hh, tm, stride=N_HEADS), :] = vd[:, hh * HEAD_W:(hh + 1) * HEAD_W]
    vb_ref[...] = vd.T.astype(bf16) if transposed else vd.astype(bf16)


def _proj(h, n, nw, w_bf, lb, tabs, tm, q_scale, seq=None):
    npos = tabs[0].shape[0] // tm
    row = lambda i: (i, 0)
    fixed = lambda i: (0, 0)
    tab = lambda i: (i % npos, 0)
    seg_f32 = jax.ShapeDtypeStruct((n, SEG_W), f32)
    seg_bf = jax.ShapeDtypeStruct((n, SEG_W), bf16)
    seg_spec = pl.BlockSpec((tm, SEG_W), row)
    rows_shape = jax.ShapeDtypeStruct((n * N_HEADS, HEAD_W), f32)
    rows_spec = pl.BlockSpec((tm * N_HEADS, HEAD_W), row)
    if seq is None:
        t_shape, t_spec = seg_bf, seg_spec
    else:
        nb = seq // tm
        t_shape = jax.ShapeDtypeStruct((n // seq, SEG_W, seq), bf16)
        t_spec = pl.BlockSpec((None, SEG_W, tm), lambda i: (i // nb, 0, i % nb))
    return pl.pallas_call(
        functools.partial(_proj_kernel, q_scale=q_scale, transposed=seq is not None),
        grid=(n // tm,),
        in_specs=[pl.BlockSpec((tm, D_MODEL), row),
                  pl.BlockSpec((1, D_MODEL), fixed),
                  pl.BlockSpec((D_MODEL, N_SEG * SEG_W), fixed),
                  pl.BlockSpec((1, SEG_W), fixed),
                  pl.BlockSpec((tm, HEAD_W), tab),
                  pl.BlockSpec((tm, HEAD_W), tab),
                  pl.BlockSpec((tm, HEAD_W), tab)],
        out_specs=[seg_spec] * 5 + [t_spec, rows_spec, rows_spec, seg_spec, t_spec],
        out_shape=[seg_f32] * 5 + [t_shape, rows_shape, rows_shape, seg_bf, t_shape],
        compiler_params=pltpu.CompilerParams(dimension_semantics=("arbitrary",), vmem_limit_bytes=VMEM_LIMIT),
        name="proj",
    )(h, nw, w_bf, lb, *tabs)


def _rope_tables(pos):
    half = ROT_DIM // 2
    inv = ROPE_THETA ** (-jnp.arange(0, ROT_DIM, 2, dtype=f32) / ROT_DIM)
    ang = pos.astype(f32)[:, None] * inv[None, :]
    cos, sin = jnp.cos(ang), jnp.sin(ang)
    t = pos.shape[0]
    rest = DA_DK - ROT_DIM
    c64 = jnp.concatenate([cos, cos, jnp.ones((t, rest), f32)], axis=1)
    sa64 = jnp.concatenate([-sin, jnp.zeros((t, half + rest), f32)], axis=1)
    sb64 = jnp.concatenate([jnp.zeros((t, half), f32), sin, jnp.zeros((t, rest), f32)], axis=1)
    return tuple(jnp.tile(a, (1, HEAD_W // DA_DK)) for a in (c64, sa64, sb64))


def _hgrn_chunk(q, k, v, lf, st, chunk, consts):
    sel, pair_masks, halves = consts
    heads = [slice(hh * HEAD_W, (hh + 1) * HEAD_W) for hh in range(N_HEADS)]
    terms = jnp.concatenate(_split3(lf), axis=1)
    gs3 = _dot(sel, terms)
    gs = gs3[:, :SEG_W] + gs3[:, SEG_W:2 * SEG_W] + gs3[:, 2 * SEG_W:]
    g = gs[:chunk]
    g_last = g[chunk - 1:chunk, :]
    qx = (q * jnp.exp2(g)).astype(bf16)
    q_bf, k_bf, v_bf = q.astype(bf16), k.astype(bf16), v.astype(bf16)
    st_bf = [s.astype(bf16) for s in st]
    o = [_dot_nt(qx[:, sl], st_bf[hh]) for hh, sl in enumerate(heads)]
    a = [_dot_nt(q_bf[:, sl], k_bf[:, sl]) * pair_masks[0] for sl in heads]
    n_mm = 1
    for lvl, h in enumerate(halves, start=1):
        if h % 8 == 0:
            ref = jnp.concatenate([jnp.broadcast_to(g[r0 + h - 1:r0 + h, :], (2 * h, SEG_W))
                                   for r0 in range(0, chunk, 2 * h)], axis=0)
        else:
            ref = gs[n_mm * chunk:(n_mm + 1) * chunk]
            n_mm += 1
        e = jnp.exp2(-jnp.abs(g - ref))
        qe, ke = (q * e).astype(bf16), (k * e).astype(bf16)
        a = [a[hh] + _dot_nt(qe[:, sl], ke[:, sl]) * pair_masks[lvl] for hh, sl in enumerate(heads)]
    kdec = (k * jnp.exp2(g_last - g)).astype(bf16)
    decay = jnp.exp2(g_last)
    o = jnp.concatenate([o[hh] + _dot(a[hh].astype(bf16), v_bf[:, sl]) for hh, sl in enumerate(heads)], axis=1)
    st_new = [st[hh] * decay[:, sl] + _dot_tn(v_bf[:, sl], kdec[:, sl]) for hh, sl in enumerate(heads)]
    return o, st_new


def _hgrn_consts(chunk):
    r = lax.broadcasted_iota(jnp.int32, (chunk, chunk), 0)
    c = lax.broadcasted_iota(jnp.int32, (chunk, chunk), 1)
    sels = [jnp.where(c <= r, 1.0, 0.0)]
    masks = [jnp.where(c == r, 1.0, 0.0)]
    halves = []
    h = chunk // 2
    while h >= 1:
        blk = -(2 * h)
        if h % 8:
            sels.append(jnp.where(c <= (r & blk) + (h - 1), 1.0, 0.0))
        same_block = (r & blk) == (c & blk)
        masks.append(jnp.where(same_block & ((r & h) != 0) & ((c & h) == 0), 1.0, 0.0))
        halves.append(h)
        h //= 2
    return jnp.concatenate(sels, axis=0).astype(bf16), masks, halves


def _hgrn_kernel(q_ref, k_ref, v_ref, lf_ref, s0_ref, o_ref, sfin_ref, st_ref, *, chunk, n_chunks):
    j = pl.program_id(1)
    consts = _hgrn_consts(chunk)

    @pl.when(j == 0)
    def _():
        for hh in range(N_HEADS):
            st_ref[hh] = s0_ref[hh].T

    def body(c, carry):
        rows = pl.ds(pl.multiple_of(c * chunk, chunk), chunk)
        o, st_new = _hgrn_chunk(q_ref[rows, :], k_ref[rows, :], v_ref[rows, :], lf_ref[rows, :],
                                [st_ref[hh] for hh in range(N_HEADS)], chunk, consts)
        o_ref[rows, :] = o
        for hh in range(N_HEADS):
            st_ref[hh] = st_new[hh]
        return carry

    lax.fori_loop(0, n_chunks, body, 0, unroll=2 if n_chunks % 2 == 0 else 1)

    @pl.when(j == pl.num_programs(1) - 1)
    def _():
        for hh in range(N_HEADS):
            sfin_ref[hh] = st_ref[hh].T


def _hgrn(q, k, v, lf, s0, t, tb, chunk):
    n = q.shape[0]
    b = n // t
    nj = t // tb
    tok = pl.BlockSpec((tb, SEG_W), lambda bi, j: (bi * nj + j, 0))
    st = pl.BlockSpec((None, N_HEADS, HEAD_W, HEAD_W), lambda bi, j: (bi, 0, 0, 0))
    return pl.pallas_call(
        functools.partial(_hgrn_kernel, chunk=chunk,n_chunks=tb // chunk),
        grid=(b, nj),
        in_specs=[tok, tok, tok, tok, st],
        out_specs=[tok, st],
        out_shape=[jax.ShapeDtypeStruct((n, SEG_W), f32), jax.ShapeDtypeStruct(s0.shape, f32)],
        scratch_shapes=[pltpu.VMEM((N_HEADS, HEAD_W, HEAD_W), f32)],
        compiler_params=pltpu.CompilerParams(dimension_semantics=("arbitrary", "arbitrary"),
                                             vmem_limit_bytes=VMEM_LIMIT),
        name="hgrn",
    )(q, k, v, lf, s0)


def _stack_maps(q):
    lane = lax.broadcasted_iota(jnp.int32, q.shape, 1)
    zero = jnp.zeros_like(q)
    return jnp.concatenate([jnp.where(lane < DA_DK, q, zero), jnp.where(lane >= DA_DK, q, zero)], axis=0)


def _softmax_step(s, v_bf, m_ref, l_ref, acc_ref, rows=None):
    sl = slice(None) if rows is None else rows
    m_prev = m_ref[sl, :]
    m_new = jnp.maximum(m_prev, jnp.max(s, axis=-1, keepdims=True))
    alpha = jnp.exp2(m_prev - m_new)
    p = jnp.exp2(s - m_new)
    l_ref[sl, :] = alpha * l_ref[sl, :] + jnp.sum(p, axis=-1, keepdims=True)
    acc_ref[sl, :] = alpha * acc_ref[sl, :] + _dot(p.astype(bf16), v_bf)
    m_ref[sl, :] = m_new


def _attn_kernel(lam_ref, qt_ref, k_ref, vt_ref, o_ref, m_ref, l_ref, acc_ref, s_ref, *, tq, n_q):
    i = pl.program_id(2)
    qt = qt_ref[...]
    sub = lax.broadcasted_iota(jnp.int32, qt.shape, 0)
    zero = jnp.zeros_like(qt)
    qs = jnp.concatenate([jnp.where(sub < DA_DK, qt, zero), jnp.where(sub >= DA_DK, qt, zero)], axis=1)
    m_ref[...] = jnp.full(m_ref.shape, NEG, f32)
    l_ref[...] = jnp.zeros(l_ref.shape, f32)
    acc_ref[...] = jnp.zeros(acc_ref.shape, f32)

    def scores(j):
        return _dot(k_ref[j * tq:(j + 1) * tq, :], qs)

    def update(s, vt):
        m_prev = m_ref[...]
        m_new = jnp.maximum(m_prev, jnp.max(s, axis=0, keepdims=True))
        alpha = jnp.exp2(m_prev - m_new)
        p = jnp.exp2(s - m_new)
        l_ref[...] = alpha * l_ref[...] + jnp.sum(p, axis=0, keepdims=True)
        acc_ref[...] = alpha * acc_ref[...] + _dot(vt, p.astype(bf16))
        m_ref[...] = m_new

    s_ref[0] = scores(0)
    for j in range(n_q - 1):
        @pl.when(j < i)
        def _():
            s_ref[(j + 1) % 2] = scores(j + 1)
            update(s_ref[j % 2], vt_ref[:, j * tq:(j + 1) * tq])

    s = s_ref[i % 2]
    key = lax.broadcasted_iota(jnp.int32, s.shape, 0)
    qry = lax.broadcasted_iota(jnp.int32, s.shape, 1) & (tq - 1)
    update(jnp.where(key <= qry, s, NEG), vt_ref[:, pl.ds(pl.multiple_of(i * tq, tq), tq)])
    o = acc_ref[...] / l_ref[...]
    o_ref[...] = (o[:, :tq] - lam_ref[0] * o[:, tq:]).T


def _attn_prompt(lam, qt_bf, k_bf, vt_bf, b, s, tq):
    k3 = k_bf.reshape(b, s, SEG_W)
    qspec = pl.BlockSpec((None, HEAD_W, tq), lambda bi, h, i: (bi, h, i))
    kspec = pl.BlockSpec((None, s, HEAD_W), lambda bi, h, i: (bi, 0, h))
    vspec = pl.BlockSpec((None, HEAD_W, s), lambda bi, h, i: (bi, h, 0))
    out = pl.pallas_call(
        functools.partial(_attn_kernel, tq=tq, n_q=s // tq),
        grid=(b, N_HEADS, s // tq),
        in_specs=[pl.BlockSpec(memory_space=pltpu.SMEM), qspec, kspec, vspec],
        out_specs=pl.BlockSpec((None, tq, HEAD_W), lambda bi, h, i: (bi, i, h)),
        out_shape=jax.ShapeDtypeStruct((b, s, SEG_W), f32),
        scratch_shapes=[pltpu.VMEM((1, 2 * tq), f32), pltpu.VMEM((1, 2 * tq), f32),
                        pltpu.VMEM((HEAD_W, 2 * tq), f32), pltpu.VMEM((2, tq, 2 * tq), f32)],
        compiler_params=pltpu.CompilerParams(dimension_semantics=("arbitrary",) * 3, vmem_limit_bytes=VMEM_LIMIT),
        name="attn_prompt",
    )(lam, qt_bf, k3, vt_bf)
    return out.reshape(b * s, SEG_W)


def _attn_sample_kernel(pt_ref, lam_ref, qs_ref, kn_ref, vn_ref, *rest, n_pg, layer):
    del pt_ref, layer
    k_refs = rest[:n_pg]
    v_refs = rest[n_pg:2 * n_pg]
    o_ref, m_ref, l_ref, acc_ref = rest[2 * n_pg:]
    j = pl.program_id(1)
    t = SAMPLE_PAD

    @pl.when(j == 0)
    def _():
        m_ref[...] = jnp.full(m_ref.shape, NEG, f32)
        l_ref[...] = jnp.zeros(l_ref.shape, f32)
        acc_ref[...] = jnp.zeros(acc_ref.shape, f32)

    head_rows = [slice(hh * 2 * t, (hh + 1) * 2 * t) for hh in range(N_HEADS)]
    head_toks = [pl.ds(hh, PAGE_SIZE, stride=N_HEADS) for hh in range(N_HEADS)]
    qs = qs_ref[...]
    s = [_dot_nt(qs[rows, :], jnp.concatenate([kr[toks, :].astype(bf16) for kr in k_refs], axis=0))
         for rows, toks in zip(head_rows, head_toks)]
    m_prev = m_ref[...]
    m_new = jnp.maximum(m_prev, jnp.concatenate([jnp.max(sh, axis=-1, keepdims=True) for sh in s], axis=0))
    alpha = jnp.exp2(m_prev - m_new)
    p = [jnp.exp2(sh - m_new[rows, :]) for sh, rows in zip(s, head_rows)]
    l_ref[...] = alpha * l_ref[...] + jnp.concatenate([jnp.sum(ph, axis=-1, keepdims=True) for ph in p], axis=0)
    pv = [_dot(ph.astype(bf16), jnp.concatenate([vr[toks, :].astype(bf16) for vr in v_refs], axis=0))
          for ph, toks in zip(p, head_toks)]
    acc_ref[...] = alpha * acc_ref[...] + jnp.concatenate(pv, axis=0)
    m_ref[...] = m_new

    @pl.when(j == pl.num_programs(1) - 1)
    def _():
        for hh in range(N_HEADS):
            rows = slice(hh * 2 * t, (hh + 1) * 2 * t)
            sl = slice(hh * HEAD_W, (hh + 1) * HEAD_W)
            s = _dot_nt(qs_ref[rows, :], kn_ref[:, sl].astype(bf16))
            row = lax.broadcasted_iota(jnp.int32, s.shape, 0) & (t - 1)
            col = lax.broadcasted_iota(jnp.int32, s.shape, 1)
            s = jnp.where(col <= row, s, NEG)
            _softmax_step(s, vn_ref[:, sl].astype(bf16), m_ref, l_ref, acc_ref, rows)
            o = acc_ref[rows, :] / l_ref[rows, :]
            o_ref[:, sl] = o[:t] - lam_ref[0] * o[t:]


def _attn_sample(page_table, lam, qs, kn, vn, cache_k, cache_v, layer, n_pg):
    bd, n_pages = page_table.shape
    t = SAMPLE_PAD

    cache_k = cache_k.reshape(cache_k.shape[:2] + (PAGE_SIZE * N_HEADS, HEAD_W))
    cache_v = cache_v.reshape(cache_v.shape[:2] + (PAGE_SIZE * N_HEADS, HEAD_W))

    def page_spec(p):
        return pl.BlockSpec((None, None, PAGE_SIZE * N_HEADS, HEAD_W),
                            lambda bi, j, pt: (layer, pt[bi * n_pages + j * n_pg + p], 0, 0))

    per_b = lambda shape: pl.BlockSpec((None,) + shape, lambda bi, j, pt: (bi, 0, 0))
    pages = [page_spec(p) for p in range(n_pg)]
    return pl.pallas_call(
        functools.partial(_attn_sample_kernel, n_pg=n_pg, layer=layer),
        grid_spec=pltpu.PrefetchScalarGridSpec(
            num_scalar_prefetch=1,
            grid=(bd, n_pages // n_pg),
            in_specs=[pl.BlockSpec(memory_space=pltpu.SMEM), per_b((2 * t * N_HEADS, HEAD_W)),
                      per_b((t, SEG_W)), per_b((t, SEG_W))] + pages + pages,
            out_specs=per_b((t, SEG_W)),
            scratch_shapes=[pltpu.VMEM((2 * t * N_HEADS, 1), f32), pltpu.VMEM((2 * t * N_HEADS, 1), f32),
                            pltpu.VMEM((2 * t * N_HEADS, HEAD_W), f32)]),
        out_shape=jax.ShapeDtypeStruct((bd, t, SEG_W), f32),
        compiler_params=pltpu.CompilerParams(dimension_semantics=("arbitrary", "arbitrary"),
                                             vmem_limit_bytes=VMEM_LIMIT),
        name="attn_sample",
    )(page_table.reshape(-1), lam, qs, kn, vn, *([cache_k] * n_pg), *([cache_v] * n_pg))


def _route(logits):
    lane = lax.broadcasted_iota(jnp.int32, logits.shape, 1).astype(f32)
    big = float(ROUTER_W)
    is_g = lane < N_GROUPS
    m1 = jnp.max(jnp.where(is_g, logits, -jnp.inf), axis=-1, keepdims=True)
    grp = jnp.min(jnp.where(is_g & (logits == m1), lane, big), axis=-1, keepdims=True)
    p_grp = 1.0 / jnp.sum(jnp.where(is_g, jnp.exp(logits - m1), 0.0), axis=-1, keepdims=True)
    lo = GATE_LANE0 + EXPERTS_PER_GROUP * grp
    in_g = (lane >= lo) & (lane < lo + EXPERTS_PER_GROUP)
    v1 = jnp.max(jnp.where(in_g, logits, -jnp.inf), axis=-1, keepdims=True)
    i1 = jnp.min(jnp.where(in_g & (logits == v1), lane, big), axis=-1, keepdims=True)
    rest = in_g & (lane != i1)
    v2 = jnp.max(jnp.where(rest, logits, -jnp.inf), axis=-1, keepdims=True)
    i2 = jnp.min(jnp.where(rest & (logits == v2), lane, big), axis=-1, keepdims=True)
    e = jnp.exp(v2 - v1)
    w1 = 1.0 / (1.0 + e)
    w2 = e / (1.0 + e)
    gates = jnp.where(lane == i1, p_grp * w1, 0.0) + jnp.where(lane == i2, p_grp * w2, 0.0)
    return jnp.where(lane == 0.0, grp, gates)


def _mix_kernel(oh_ref, od_ref, gh_ref, h_ref, hnw_ref, dnw_ref, wo_ref, fnw_ref, wrh_ref, wrl_ref, br_ref,
                hx_ref, rt_ref, *, od_scale):
    parts = []
    for hh in range(N_HEADS):
        sl = slice(hh * HEAD_W, (hh + 1) * HEAD_W)
        gate = 1.0 / (1.0 + jnp.exp(-gh_ref[:, sl]))
        parts.append((_rms(oh_ref[:, sl], hnw_ref[...]) * gate).astype(bf16))
    for hh in range(N_HEADS):
        sl = slice(hh * HEAD_W, (hh + 1) * HEAD_W)
        parts.append((_rms(od_ref[:, sl], dnw_ref[...]) * od_scale).astype(bf16))
    h2 = h_ref[...] + _dot(jnp.concatenate(parts, axis=1), wo_ref[...])
    xn = _rms(h2, fnw_ref[...])
    x_hi = xn.astype(bf16)
    x_lo = (xn - x_hi.astype(f32)).astype(bf16)
    logits = _dot(x_hi, wrh_ref[...]) + _dot(x_lo, wrh_ref[...]) + _dot(x_hi, wrl_ref[...]) + br_ref[...]
    tm = h2.shape[0]
    for c in range(D_MODEL // HEAD_W):
        hx_ref[pl.ds(c, tm, stride=ROW_CHUNKS), :] = h2[:, c * HEAD_W:(c + 1) * HEAD_W]
    router = _route(logits)
    hx_ref[pl.ds(D_MODEL // HEAD_W, tm, stride=ROW_CHUNKS), :] = router
    rt_ref[...] = router


def _mix(o_h, o_d, g_h, h, hnw, dnw, wo_bf, fnw, wr_hi, wr_lo, br, od_scale, tm):
    n = o_h.shape[0]
    row = lambda i: (i, 0)
    fixed = lambda i: (0, 0)
    seg = pl.BlockSpec((tm, SEG_W), row)
    return pl.pallas_call(
        functools.partial(_mix_kernel, od_scale=od_scale),
        grid=(n // tm,),
        in_specs=[seg, seg, seg, pl.BlockSpec((tm, D_MODEL), row),
                  pl.BlockSpec((1, HEAD_W), fixed), pl.BlockSpec((1, HEAD_W), fixed),
                  pl.BlockSpec((D_MODEL, D_MODEL), fixed), pl.BlockSpec((1, D_MODEL), fixed),
                  pl.BlockSpec((D_MODEL, ROUTER_W), fixed), pl.BlockSpec((D_MODEL, ROUTER_W), fixed),
                  pl.BlockSpec((1, ROUTER_W), fixed)],
        out_specs=[pl.BlockSpec((tm * ROW_CHUNKS, HEAD_W), row), pl.BlockSpec((tm, ROUTER_W), row)],
        out_shape=[jax.ShapeDtypeStruct((n * ROW_CHUNKS, HEAD_W), f32), jax.ShapeDtypeStruct((n, ROUTER_W), f32)],
        compiler_params=pltpu.CompilerParams(dimension_semantics=("arbitrary",), vmem_limit_bytes=VMEM_LIMIT),
        name="mix",
    )(o_h, o_d, g_h, h, hnw, dnw, wo_bf, fnw, wr_hi, wr_lo, br)


def _moe_kernel(ids_ref, grp_ref, cnt_ref, hx_hbm, fnw_ref, onw_ref, wg_ref, wu_ref, wd_ref, out_hbm,
                xbuf, ybuf, gsem, ssem, *, tm, final_norm):
    i = pl.program_id(0)
    n_tiles = pl.num_programs(0)
    slot = i % 2

    def gather_copy(tile, sl, r):
        tok = ids_ref[tile * tm + r]
        return pltpu.make_async_copy(hx_hbm.at[pl.ds(tok * ROW_CHUNKS, ROW_CHUNKS)],
                                     xbuf.at[sl, pl.ds(r * ROW_CHUNKS, ROW_CHUNKS)], gsem.at[sl])

    def scatter_copy(tile, sl, r):
        tok = ids_ref[tile * tm + r]
        return pltpu.make_async_copy(ybuf.at[sl, pl.ds(r, 1)], out_hbm.at[pl.ds(tok, 1)], ssem.at[sl])

    def for_rows(n, fn):
        def body(r, carry):
            fn(r)
            return carry
        if isinstance(n, int):
            lax.fori_loop(0, n, body, 0, unroll=8)
        else:
            lax.fori_loop(0, n, body, 0)

    def gather_wait(sl):
        pltpu.make_async_copy(hx_hbm.at[pl.ds(0, tm * ROW_CHUNKS)], xbuf.at[sl], gsem.at[sl]).wait()

    def scatter_start(tile, sl):
        cnt = cnt_ref[tile]

        @pl.when(cnt == tm)
        def _():
            for_rows(tm, lambda r: scatter_copy(tile, sl, r).start())

        @pl.when(cnt < tm)
        def _():
            for_rows(cnt, lambda r: scatter_copy(tile, sl, r).start())

    def scatter_wait(tile, sl):
        cnt = cnt_ref[tile]

        @pl.when(cnt == tm)
        def _():
            pltpu.make_async_copy(ybuf.at[sl], out_hbm.at[pl.ds(0, tm)], ssem.at[sl]).wait()

        @pl.when(cnt < tm)
        def _():
            for_rows(cnt, lambda r: scatter_copy(tile, sl, r).wait())

    @pl.when(i == 0)
    def _():
        for_rows(tm, lambda r: gather_copy(0, 0, r).start())

    @pl.when(i + 1 < n_tiles)
    def _():
        for_rows(tm, lambda r: gather_copy(i + 1, 1 - slot, r).start())

    gather_wait(slot)

    @pl.when(i >= 2)
    def _():
        scatter_wait(i - 2, slot)

    xs = xbuf.at[slot]
    chunk = lambda c: xs[pl.ds(c, tm, stride=ROW_CHUNKS), :]
    h2 = jnp.concatenate([chunk(c) for c in range(D_MODEL // HEAD_W)], axis=1)
    router = chunk(D_MODEL // HEAD_W)
    xn = _rms(h2, fnw_ref[...]).astype(bf16)
    lane = lax.broadcasted_iota(jnp.int32, router.shape, 1)
    lane0 = GATE_LANE0 + EXPERTS_PER_GROUP * grp_ref[i]
    acc = h2
    for e in range(EXPERTS_PER_GROUP):
        gate = jnp.sum(jnp.where(lane == lane0 + e, router, 0.0), axis=-1, keepdims=True)
        a = _dot(xn, wg_ref[e])
        hid = a / (1.0 + jnp.exp(-a)) * _dot(xn, wu_ref[e])
        acc = acc + _dot((hid * gate).astype(bf16), wd_ref[e])
    if final_norm:
        acc = _rms(acc, onw_ref[...])
    ybuf[slot] = acc

    scatter_start(i, slot)

    @pl.when(i == n_tiles - 1)
    def _():
        @pl.when(i >= 1)
        def _():
            scatter_wait(i - 1, 1 - slot)
        scatter_wait(i, slot)


def _dispatch(grp_f32, tm):
    n = grp_f32.shape[0]
    n_tiles = n // tm + N_GROUPS
    grp = grp_f32.astype(jnp.int32)
    onehot = (grp[:, None] == jnp.arange(N_GROUPS, dtype=jnp.int32)[None, :]).astype(jnp.int32)
    csum = jnp.cumsum(onehot, axis=0)
    cnt = csum[-1]
    rank = jnp.sum(csum * onehot, axis=1) - 1
    tiles_g = (cnt + tm - 1) // tm
    tile_end = jnp.cumsum(tiles_g)
    tile_start = tile_end - tiles_g
    pos = jnp.sum(onehot * (tile_start * tm)[None, :], axis=1) + rank
    tile = jnp.arange(n_tiles, dtype=jnp.int32)
    tile_grp_raw = jnp.sum((tile[:, None] >= tile_end[None, :]).astype(jnp.int32), axis=1)
    last_grp = jnp.max(jnp.where(cnt > 0, jnp.arange(N_GROUPS, dtype=jnp.int32), 0))
    tile_grp = jnp.minimum(tile_grp_raw, last_grp)
    in_range = tile_grp_raw < N_GROUPS
    rows_left = cnt[tile_grp] - (tile - tile_start[tile_grp]) * tm
    tile_cnt = jnp.where(in_range, jnp.clip(rows_left, 0, tm), 0).astype(jnp.int32)
    ids = jnp.zeros((n_tiles * tm,), jnp.int32).at[pos].set(jnp.arange(n, dtype=jnp.int32))
    return ids, tile_grp, tile_cnt


def _moe(hx, router, fnw, onw, wg_bf, wu_bf, wd_bf, layer, tm, final_norm):
    n = router.shape[0]
    ids, tile_grp, tile_cnt = _dispatch(router[:, 0], tm)
    n_tiles = tile_grp.shape[0]
    fixed = lambda i, ids_r, grp_r, cnt_r: (0, 0)
    wspec = lambda shape: pl.BlockSpec((None, EXPERTS_PER_GROUP) + shape,
                                       lambda i, ids_r, grp_r, cnt_r: (layer, grp_r[i], 0, 0))
    return pl.pallas_call(
        functools.partial(_moe_kernel, tm=tm, final_norm=final_norm),
        grid_spec=pltpu.PrefetchScalarGridSpec(
            num_scalar_prefetch=3,
            grid=(n_tiles,),
            in_specs=[pl.BlockSpec(memory_space=pl.ANY),
                      pl.BlockSpec((1, D_MODEL), fixed), pl.BlockSpec((1, D_MODEL), fixed),
                      wspec((D_MODEL, EXPERT_FF)), wspec((D_MODEL, EXPERT_FF)), wspec((EXPERT_FF, D_MODEL))],
            out_specs=pl.BlockSpec(memory_space=pl.ANY),
            scratch_shapes=[pltpu.VMEM((2, tm * ROW_CHUNKS, HEAD_W), f32), pltpu.VMEM((2, tm, D_MODEL), f32),
                            pltpu.SemaphoreType.DMA((2,)), pltpu.SemaphoreType.DMA((2,))]),
        out_shape=jax.ShapeDtypeStruct((n, D_MODEL), f32),
        compiler_params=pltpu.CompilerParams(dimension_semantics=("arbitrary",), vmem_limit_bytes=VMEM_LIMIT),
        name="moe",
    )(ids, tile_grp, tile_cnt, hx, fnw, onw, wg_bf, wu_bf, wd_bf)


def _stacked_sample_queries(qd):
    bd, t = qd.shape[:2]
    q5 = qd.reshape(bd, t, N_HEADS, 2, DA_DK)
    eye = jnp.eye(2, dtype=qd.dtype)
    qz = q5[:, :, :, :, None, :] * eye[None, None, None, :, :, None]
    return qz.transpose(0, 2, 3, 1, 4, 5).reshape(bd, N_HEADS * 2 * t, HEAD_W)


def kernel(x_prompt, x_sample, cache_k, cache_v, state_hgrn, page_table, attn_norm_w, w_in, hgrn_lb, hgrn_norm_w,
           diff_lambda, diff_norm_w, w_o, ffn_norm_w, w_r1, b_r1, w_r2, b_r2, w_gate, w_up, w_down, final_norm_w):
    b, s = x_prompt.shape[:2]
    bd, t = x_sample.shape[:2]
    depth = w_in.shape[0]
    n_pages = page_table.shape[1]
    past_len = n_pages * PAGE_SIZE
    assert x_prompt.shape[2] == D_MODEL and w_in.shape[2] == N_SEG * SEG_W and t <= SAMPLE_PAD
    tp = SAMPLE_PAD

    tabs_p = _rope_tables(jnp.arange(s, dtype=jnp.int32))
    tabs_s = _rope_tables(jnp.tile(past_len + jnp.arange(t, dtype=jnp.int32), bd))
    p_lb = jax.nn.softmax(hgrn_lb.astype(f32), axis=0)
    lb_all = jnp.cumsum(p_lb, axis=0) - p_lb[0:1]

    w_in_bf = w_in.astype(bf16)
    w_o_bf = w_o.astype(bf16)
    wg_bf, wu_bf, wd_bf = w_gate.astype(bf16), w_up.astype(bf16), w_down.astype(bf16)
    pad_r = ROUTER_W - N_GROUPS - N_EXPERTS
    w_r = jnp.concatenate([w_r1, w_r2, jnp.zeros((depth, D_MODEL, pad_r), f32)], axis=2)
    w_r_hi = w_r.astype(bf16)
    w_r_lo = (w_r - w_r_hi.astype(f32)).astype(bf16)
    b_r = jnp.concatenate([b_r1, b_r2, jnp.zeros((depth, pad_r), f32)], axis=1)

    tm_p = 256 if (b * s) % 256 == 0 else b * s
    tq = 512 if s % 512 == 0 else s
    tb_p = 512 if s % 512 == 0 else s
    tm_moe_p = 512 if (b * s) % 512 == 0 else b * s
    n_s = bd * t
    n_pg = next(p for p in (16, 8, 4, 2, 1) if n_pages % p == 0)
    tm_proj = 512 if s % 512 == 0 else tm_p
    q_scale = DA_DK ** -0.5 * LOG2E

    hp = x_prompt.reshape(b * s, D_MODEL)
    hs = x_sample.reshape(n_s, D_MODEL)
    zeros_state = jnp.zeros((b, N_HEADS, HEAD_W, HEAD_W), f32)
    kp_l, vp_l, sp_l, ks_l, vs_l, ss_l = [], [], [], [], [], []
    for l in range(depth):
        lam_init = 0.8 - 0.6 * math.exp(-0.3 * l)
        dl = diff_lambda[l].astype(f32)
        lam = (jnp.exp(jnp.sum(dl[0] * dl[1])) - jnp.exp(jnp.sum(dl[2] * dl[3])) + lam_init).reshape(1)
        lb = lb_all[l].reshape(1, SEG_W)
        nw = attn_norm_w[l].reshape(1, D_MODEL)
        fnw = ffn_norm_w[l].reshape(1, D_MODEL)
        onw = final_norm_w.reshape(1, D_MODEL)
        hnw = hgrn_norm_w[l].reshape(1, HEAD_W)
        dnw = diff_norm_w[l].reshape(1, HEAD_W)
        last = l == depth - 1

        def tail(o_h, o_d, g_h, h, tm_mix, tm_moe):
            hx, router = _mix(o_h, o_d, g_h, h, hnw, dnw, w_o_bf[l], fnw, w_r_hi[l], w_r_lo[l],
                              b_r[l].reshape(1, ROUTER_W), 1.0 - lam_init, tm_mix)
            return _moe(hx, router, fnw, onw, wg_bf, wu_bf, wd_bf, l, tm_moe, last)

        qh, kh, ih, lf, gh, qd, kd, vd, kb, vb = _proj(hp, b * s, nw, w_in_bf[l], lb, tabs_p, tm_proj, q_scale, seq=s)
        o_h, s_fin = _hgrn(qh, kh, ih, lf, zeros_state, s, tb_p, HG_CHUNK)
        o_d = _attn_prompt(lam, qd, kb, vb, b, s, tq)
        hp = tail(o_h, o_d, gh, hp, tm_p, tm_moe_p)
        kp_l.append(kd.reshape(b, s, N_HEADS, HEAD_W))
        vp_l.append(vd.reshape(b, s, N_HEADS, HEAD_W))
        sp_l.append(s_fin)

        qh, kh, ih, lf, gh, qd, kd, vd, kb, vb = _proj(hs, n_s, nw, w_in_bf[l], lb, tabs_s, n_s, q_scale)
        pad = lambda a: jnp.pad(a.reshape(bd, t, SEG_W), ((0, 0), (0, tp - t), (0, 0)))
        flat = lambda a: pad(a).reshape(bd * tp, SEG_W)
        o_h, s_fin = _hgrn(flat(qh), flat(kh), flat(ih), flat(lf), state_hgrn[l].astype(f32), tp, tp, tp)
        o_h = o_h.reshape(bd, tp, SEG_W)[:, :t].reshape(n_s, SEG_W)
        o_d = _attn_sample(page_table, lam, _stacked_sample_queries(pad(qd)), pad(kb), pad(vb),
                           cache_k, cache_v, l, n_pg)
        o_d = o_d[:, :t].reshape(n_s, SEG_W)
        hs = tail(o_h, o_d, gh, hs, n_s, n_s)
        ks_l.append(kd.reshape(bd, t, N_HEADS, HEAD_W))
        vs_l.append(vd.reshape(bd, t, N_HEADS, HEAD_W))
        ss_l.append(s_fin)

    return (hp.reshape(b, s, D_MODEL), hs.reshape(bd, t, D_MODEL), jnp.stack(kp_l), jnp.stack(vp_l),
            jnp.stack(sp_l), jnp.stack(ks_l), jnp.stack(vs_l), jnp.stack(ss_l))
```

```python
import functools
import math

import jax
import jax.numpy as jnp
from jax import lax
from jax.experimental import pallas as pl
from jax.experimental.pallas import tpu as pltpu

f32 = jnp.float32
bf16 = jnp.bfloat16

D_MODEL = 1024
N_HEADS = 4
HEAD_W = 128
SEG_W = N_HEADS * HEAD_W
N_SEG = 7
DA_DK = 64
ROT_DIM = DA_DK // 4
ROPE_THETA = 500000.0
PAGE_SIZE = 128
N_GROUPS = 4
EXPERTS_PER_GROUP = 4
N_EXPERTS = N_GROUPS * EXPERTS_PER_GROUP
PAIR_LO = (0, 0, 0, 1, 1, 2)
PAIR_HI = (1, 2, 3, 2, 3, 3)
EXPERT_FF = D_MODEL // 2
NORM_EPS = 1e-6
NEG = -1e30
F_MIN = 1e-30

HG_CHUNK = 64
LOG2E = math.log2(math.e)
SAMPLE_PAD = 8
GATE_LANE0 = N_GROUPS
ROUTER_W = 128
ROW_CHUNKS = (D_MODEL + ROUTER_W) // HEAD_W

VMEM_LIMIT = 48 * 1024 * 1024


def _dot(a, b):
    return jnp.dot(a, b, preferred_element_type=f32)


def _dot_nt(a, b):
    return lax.dot_general(a, b, (((1,), (1,)), ((), ())), preferred_element_type=f32)


def _dot_tn(a, b):
    return lax.dot_general(a, b, (((0,), (0,)), ((), ())), preferred_element_type=f32)


def _rms(x, w):
    return x * lax.rsqrt(jnp.mean(x * x, axis=-1, keepdims=True) + NORM_EPS) * w


def _split3(x):
    hi = x.astype(bf16)
    r1 = x - hi.astype(f32)
    mid = r1.astype(bf16)
    lo = (r1 - mid.astype(f32)).astype(bf16)
    return hi, mid, lo


def _proj_kernel(h_ref, nw_ref, w_ref, lb_ref, c_ref, sa_ref, sb_ref,
                 qh_ref, kh_ref, ih_ref, lf_ref, gh_ref, qd_ref, kd_ref, vd_ref, kb_ref, vb_ref, *, q_scale, transposed):
    tm = h_ref.shape[0]
    xn = _rms(h_ref[...], nw_ref[...]).astype(bf16)

    def seg(i):
        return _dot(xn, w_ref[:, i * SEG_W:(i + 1) * SEG_W])

    qh_ref[...] = seg(0) * (HEAD_W ** -0.5)
    hf = seg(1)
    lb = lb_ref[...]
    e = jnp.exp(-jnp.abs(hf))
    r = 1.0 / (1.0 + e)
    pos = hf >= 0.0
    sig = jnp.where(pos, r, e * r)
    nsig = jnp.where(pos, e * r, r)
    f = lb + (1.0 - lb) * sig
    lf_ref[...] = jnp.log(jnp.maximum(f, F_MIN)) * LOG2E
    kh_ref[...] = (1.0 - lb) * nsig
    ih_ref[...] = seg(2)
    gh_ref[...] = seg(3)

    c = c_ref[...]
    sa = sa_ref[...]
    sb = sb_ref[...]

    def rope(z, hh):
        zz = z[:, hh * HEAD_W:(hh + 1) * HEAD_W]
        return zz * c + pltpu.roll(zz, HEAD_W - ROT_DIM // 2, 1) * sa + pltpu.roll(zz, ROT_DIM // 2, 1) * sb

    zq = seg(4)
    zk = seg(5)
    for hh in range(N_HEADS):
        sl = slice(hh * HEAD_W, (hh + 1) * HEAD_W)
        qr = rope(zq, hh) * q_scale
        if transposed:
            qd_ref[sl, :] = qr.T.astype(bf16)
        else:
            qd_ref[:, sl] = qr.astype(bf16)
        kr = rope(zk, hh)
        kd_ref[pl.ds(hh, tm, stride=N_HEADS), :] = kr
        kb_ref[:, sl] = kr.astype(bf16)
    vd = seg(6)
    for hh in range(N_HEADS):
        vd_ref[pl.ds(hh, tm, stride=N_HEADS), :] = vd[:, hh * HEAD_W:(hh + 1) * HEAD_W]
    vb_ref[...] = vd.T.astype(bf16) if transposed else vd.astype(bf16)


def _proj(h, n, nw, w_bf, lb, tabs, tm, q_scale, seq=None):
    npos = tabs[0].shape[0] // tm
    row = lambda i: (i, 0)
    fixed = lambda i: (0, 0)
    tab = lambda i: (i % npos, 0)
    seg_f32 = jax.ShapeDtypeStruct((n, SEG_W), f32)
    seg_bf = jax.ShapeDtypeStruct((n, SEG_W), bf16)
    seg_spec = pl.BlockSpec((tm, SEG_W), row)
    rows_shape = jax.ShapeDtypeStruct((n * N_HEADS, HEAD_W), f32)
    rows_spec = pl.BlockSpec((tm * N_HEADS, HEAD_W), row)
    if seq is None:
        t_shape, t_spec = seg_bf, seg_spec
    else:
        nb = seq // tm
        t_shape = jax.ShapeDtypeStruct((n // seq, SEG_W, seq), bf16)
        t_spec = pl.BlockSpec((None, SEG_W, tm), lambda i: (i // nb, 0, i % nb))
    return pl.pallas_call(
        functools.partial(_proj_kernel, q_scale=q_scale, transposed=seq is not None),
        grid=(n // tm,),
        in_specs=[pl.BlockSpec((tm, D_MODEL), row),
                  pl.BlockSpec((1, D_MODEL), fixed),
                  pl.BlockSpec((D_MODEL, N_SEG * SEG_W), fixed),
                  pl.BlockSpec((1, SEG_W), fixed),
                  pl.BlockSpec((tm, HEAD_W), tab),
                  pl.BlockSpec((tm, HEAD_W), tab),
                  pl.BlockSpec((tm, HEAD_W), tab)],
        out_specs=[seg_spec] * 5 + [t_spec, rows_spec, rows_spec, seg_spec, t_spec],
        out_shape=[seg_f32] * 5 + [t_shape, rows_shape, rows_shape, seg_bf, t_shape],
        compiler_params=pltpu.CompilerParams(dimension_semantics=("arbitrary",), vmem_limit_bytes=VMEM_LIMIT),
        name="proj",
    )(h, nw, w_bf, lb, *tabs)


def _rope_tables(pos):
    half = ROT_DIM // 2
    inv = ROPE_THETA ** (-jnp.arange(0, ROT_DIM, 2, dtype=f32) / ROT_DIM)
    ang = pos.astype(f32)[:, None] * inv[None, :]
    cos, sin = jnp.cos(ang), jnp.sin(ang)
    t = pos.shape[0]
    rest = DA_DK - ROT_DIM
    c64 = jnp.concatenate([cos, cos, jnp.ones((t, rest), f32)], axis=1)
    sa64 = jnp.concatenate([-sin, jnp.zeros((t, half + rest), f32)], axis=1)
    sb64 = jnp.concatenate([jnp.zeros((t, half), f32), sin, jnp.zeros((t, rest), f32)], axis=1)
    return tuple(jnp.tile(a, (1, HEAD_W // DA_DK)) for a in (c64, sa64, sb64))


def _hgrn_chunk(q, k, v, lf, st, chunk, consts):
    sel, pair_masks, halves = consts
    heads = [slice(hh * HEAD_W, (hh + 1) * HEAD_W) for hh in range(N_HEADS)]
    terms = jnp.concatenate(_split3(lf), axis=1)
    gs3 = _dot(sel, terms)
    gs = gs3[:, :SEG_W] + gs3[:, SEG_W:2 * SEG_W] + gs3[:, 2 * SEG_W:]
    g = gs[:chunk]
    g_last = g[chunk - 1:chunk, :]
    qx = (q * jnp.exp2(g)).astype(bf16)
    q_bf, k_bf, v_bf = q.astype(bf16), k.astype(bf16), v.astype(bf16)
    st_bf = [s.astype(bf16) for s in st]
    o = [_dot_nt(qx[:, sl], st_bf[hh]) for hh, sl in enumerate(heads)]
    a = [_dot_nt(q_bf[:, sl], k_bf[:, sl]) * pair_masks[0] for sl in heads]
    n_mm = 1
    for lvl, h in enumerate(halves, start=1):
        if h % 8 == 0:
            ref = jnp.concatenate([jnp.broadcast_to(g[r0 + h - 1:r0 + h, :], (2 * h, SEG_W))
                                   for r0 in range(0, chunk, 2 * h)], axis=0)
        else:
            ref = gs[n_mm * chunk:(n_mm + 1) * chunk]
            n_mm += 1
        e = jnp.exp2(-jnp.abs(g - ref))
        qe, ke = (q * e).astype(bf16), (k * e).astype(bf16)
        a = [a[hh] + _dot_nt(qe[:, sl], ke[:, sl]) * pair_masks[lvl] for hh, sl in enumerate(heads)]
    kdec = (k * jnp.exp2(g_last - g)).astype(bf16)
    decay = jnp.exp2(g_last)
    o = jnp.concatenate([o[hh] + _dot(a[hh].astype(bf16), v_bf[:, sl]) for hh, sl in enumerate(heads)], axis=1)
    st_new = [st[hh] * decay[:, sl] + _dot_tn(v_bf[:, sl], kdec[:, sl]) for hh, sl in enumerate(heads)]
    return o, st_new


def _hgrn_consts(chunk):
    r = lax.broadcasted_iota(jnp.int32, (chunk, chunk), 0)
    c = lax.broadcasted_iota(jnp.int32, (chunk, chunk), 1)
    sels = [jnp.where(c <= r, 1.0, 0.0)]
    masks = [jnp.where(c == r, 1.0, 0.0)]
    halves = []
    h = chunk // 2
    while h >= 1:
        blk = -(2 * h)
        if h % 8:
            sels.append(jnp.where(c <= (r & blk) + (h - 1), 1.0, 0.0))
        same_block = (r & blk) == (c & blk)
        masks.append(jnp.where(same_block & ((r & h) != 0) & ((c & h) == 0), 1.0, 0.0))
        halves.append(h)
        h //= 2
    return jnp.concatenate(sels, axis=0).astype(bf16), masks, halves


def _hgrn_kernel(q_ref, k_ref, v_ref, lf_ref, s0_ref, o_ref, sfin_ref, st_ref, *, chunk, n_chunks):
    j = pl.program_id(1)
    consts = _hgrn_consts(chunk)

    @pl.when(j == 0)
    def _():
        for hh in range(N_HEADS):
            st_ref[hh] = s0_ref[hh].T

    def body(c, carry):
        rows = pl.ds(pl.multiple_of(c * chunk, chunk), chunk)
        o, st_new = _hgrn_chunk(q_ref[rows, :], k_ref[rows, :], v_ref[rows, :], lf_ref[rows, :],
                                [st_ref[hh] for hh in range(N_HEADS)], chunk, consts)
        o_ref[rows, :] = o
        for hh in range(N_HEADS):
            st_ref[hh] = st_new[hh]
        return carry

    lax.fori_loop(0, n_chunks, body, 0, unroll=2 if n_chunks % 2 == 0 else 1)

    @pl.when(j == pl.num_programs(1) - 1)
    def _():
        for hh in range(N_HEADS):
            sfin_ref[hh] = st_ref[hh].T


def _hgrn(q, k, v, lf, s0, t, tb, chunk):
    n = q.shape[0]
    b = n // t
    nj = t // tb
    tok = pl.BlockSpec((tb, SEG_W), lambda bi, j: (bi * nj + j, 0))
    st = pl.BlockSpec((None, N_HEADS, HEAD_W, HEAD_W), lambda bi, j: (bi, 0, 0, 0))
    return pl.pallas_call(
        functools.partial(_hgrn_kernel, chunk=chunk,n_chunks=tb // chunk),
        grid=(b, nj),
        in_specs=[tok, tok, tok, tok, st],
        out_specs=[tok, st],
        out_shape=[jax.ShapeDtypeStruct((n, SEG_W), f32), jax.ShapeDtypeStruct(s0.shape, f32)],
        scratch_shapes=[pltpu.VMEM((N_HEADS, HEAD_W, HEAD_W), f32)],
        compiler_params=pltpu.CompilerParams(dimension_semantics=("arbitrary", "arbitrary"),
                                             vmem_limit_bytes=VMEM_LIMIT),
        name="hgrn",
    )(q, k, v, lf, s0)


def _stack_maps(q):
    lane = lax.broadcasted_iota(jnp.int32, q.shape, 1)
    zero = jnp.zeros_like(q)
    return jnp.concatenate([jnp.where(lane < DA_DK, q, zero), jnp.where(lane >= DA_DK, q, zero)], axis=0)


def _softmax_step(s, v_bf, m_ref, l_ref, acc_ref, rows=None):
    sl = slice(None) if rows is None else rows
    m_prev = m_ref[sl, :]
    m_new = jnp.maximum(m_prev, jnp.max(s, axis=-1, keepdims=True))
    alpha = jnp.exp2(m_prev - m_new)
    p = jnp.exp2(s - m_new)
    l_ref[sl, :] = alpha * l_ref[sl, :] + jnp.sum(p, axis=-1, keepdims=True)
    acc_ref[sl, :] = alpha * acc_ref[sl, :] + _dot(p.astype(bf16), v_bf)
    m_ref[sl, :] = m_new


def _attn_kernel(lam_ref, qt_ref, k_ref, vt_ref, o_ref, m_ref, l_ref, acc_ref, s_ref, *, tq, n_q):
    i = pl.program_id(2)
    qt = qt_ref[...]
    sub = lax.broadcasted_iota(jnp.int32, qt.shape, 0)
    zero = jnp.zeros_like(qt)
    qs = jnp.concatenate([jnp.where(sub < DA_DK, qt, zero), jnp.where(sub >= DA_DK, qt, zero)], axis=1)
    m_ref[...] = jnp.full(m_ref.shape, NEG, f32)
    l_ref[...] = jnp.zeros(l_ref.shape, f32)
    acc_ref[...] = jnp.zeros(acc_ref.shape, f32)

    def scores(j):
        return _dot(k_ref[j * tq:(j + 1) * tq, :], qs)

    def update(s, vt):
        m_prev = m_ref[...]
        m_new = jnp.maximum(m_prev, jnp.max(s, axis=0, keepdims=True))
        alpha = jnp.exp2(m_prev - m_new)
        p = jnp.exp2(s - m_new)
        l_ref[...] = alpha * l_ref[...] + jnp.sum(p, axis=0, keepdims=True)
        acc_ref[...] = alpha * acc_ref[...] + _dot(vt, p.astype(bf16))
        m_ref[...] = m_new

    s_ref[0] = scores(0)
    for j in range(n_q - 1):
        @pl.when(j < i)
        def _():
            s_ref[(j + 1) % 2] = scores(j + 1)
            update(s_ref[j % 2], vt_ref[:, j * tq:(j + 1) * tq])

    s = s_ref[i % 2]
    key = lax.broadcasted_iota(jnp.int32, s.shape, 0)
    qry = lax.broadcasted_iota(jnp.int32, s.shape, 1) & (tq - 1)
    update(jnp.where(key <= qry, s, NEG), vt_ref[:, pl.ds(pl.multiple_of(i * tq, tq), tq)])
    o = acc_ref[...] / l_ref[...]
    o_ref[...] = (o[:, :tq] - lam_ref[0] * o[:, tq:]).T


def _attn_prompt(lam, qt_bf, k_bf, vt_bf, b, s, tq):
    k3 = k_bf.reshape(b, s, SEG_W)
    qspec = pl.BlockSpec((None, HEAD_W, tq), lambda bi, h, i: (bi, h, i))
    kspec = pl.BlockSpec((None, s, HEAD_W), lambda bi, h, i: (bi, 0, h))
    vspec = pl.BlockSpec((None, HEAD_W, s), lambda bi, h, i: (bi, h, 0))
    out = pl.pallas_call(
        functools.partial(_attn_kernel, tq=tq, n_q=s // tq),
        grid=(b, N_HEADS, s // tq),
        in_specs=[pl.BlockSpec(memory_space=pltpu.SMEM), qspec, kspec, vspec],
        out_specs=pl.BlockSpec((None, tq, HEAD_W), lambda bi, h, i: (bi, i, h)),
        out_shape=jax.ShapeDtypeStruct((b, s, SEG_W), f32),
        scratch_shapes=[pltpu.VMEM((1, 2 * tq), f32), pltpu.VMEM((1, 2 * tq), f32),
                        pltpu.VMEM((HEAD_W, 2 * tq), f32), pltpu.VMEM((2, tq, 2 * tq), f32)],
        compiler_params=pltpu.CompilerParams(dimension_semantics=("arbitrary",) * 3, vmem_limit_bytes=VMEM_LIMIT),
        name="attn_prompt",
    )(lam, qt_bf, k3, vt_bf)
    return out.reshape(b * s, SEG_W)


def _attn_sample_kernel(pt_ref, lam_ref, qs_ref, kn_ref, vn_ref, *rest, n_pg, layer):
    del pt_ref, layer
    k_refs = rest[:n_pg]
    v_refs = rest[n_pg:2 * n_pg]
    o_ref, m_ref, l_ref, acc_ref = rest[2 * n_pg:]
    j = pl.program_id(1)
    t = SAMPLE_PAD

    @pl.when(j == 0)
    def _():
        m_ref[...] = jnp.full(m_ref.shape, NEG, f32)
        l_ref[...] = jnp.zeros(l_ref.shape, f32)
        acc_ref[...] = jnp.zeros(acc_ref.shape, f32)

    head_rows = [slice(hh * 2 * t, (hh + 1) * 2 * t) for hh in range(N_HEADS)]
    head_toks = [pl.ds(hh, PAGE_SIZE, stride=N_HEADS) for hh in range(N_HEADS)]
    qs = qs_ref[...]
    s = [_dot_nt(qs[rows, :], jnp.concatenate([kr[toks, :].astype(bf16) for kr in k_refs], axis=0))
         for rows, toks in zip(head_rows, head_toks)]
    m_prev = m_ref[...]
    m_new = jnp.maximum(m_prev, jnp.concatenate([jnp.max(sh, axis=-1, keepdims=True) for sh in s], axis=0))
    alpha = jnp.exp2(m_prev - m_new)
    p = [jnp.exp2(sh - m_new[rows, :]) for sh, rows in zip(s, head_rows)]
    l_ref[...] = alpha * l_ref[...] + jnp.concatenate([jnp.sum(ph, axis=-1, keepdims=True) for ph in p], axis=0)
    pv = [_dot(ph.astype(bf16), jnp.concatenate([vr[toks, :].astype(bf16) for vr in v_refs], axis=0))
          for ph, toks in zip(p, head_toks)]
    acc_ref[...] = alpha * acc_ref[...] + jnp.concatenate(pv, axis=0)
    m_ref[...] = m_new

    @pl.when(j == pl.num_programs(1) - 1)
    def _():
        for hh in range(N_HEADS):
            rows = slice(hh * 2 * t, (hh + 1) * 2 * t)
            sl = slice(hh * HEAD_W, (hh + 1) * HEAD_W)
            s = _dot_nt(qs_ref[rows, :], kn_ref[:, sl].astype(bf16))
            row = lax.broadcasted_iota(jnp.int32, s.shape, 0) & (t - 1)
            col = lax.broadcasted_iota(jnp.int32, s.shape, 1)
            s = jnp.where(col <= row, s, NEG)
            _softmax_step(s, vn_ref[:, sl].astype(bf16), m_ref, l_ref, acc_ref, rows)
            o = acc_ref[rows, :] / l_ref[rows, :]
            o_ref[:, sl] = o[:t] - lam_ref[0] * o[t:]


def _attn_sample(page_table, lam, qs, kn, vn, cache_k, cache_v, layer, n_pg):
    bd, n_pages = page_table.shape
    t = SAMPLE_PAD

    cache_k = cache_k.reshape(cache_k.shape[:2] + (PAGE_SIZE * N_HEADS, HEAD_W))
    cache_v = cache_v.reshape(cache_v.shape[:2] + (PAGE_SIZE * N_HEADS, HEAD_W))

    def page_spec(p):
        return pl.BlockSpec((None, None, PAGE_SIZE * N_HEADS, HEAD_W),
                            lambda bi, j, pt: (layer, pt[bi * n_pages + j * n_pg + p], 0, 0))

    per_b = lambda shape: pl.BlockSpec((None,) + shape, lambda bi, j, pt: (bi, 0, 0))
    pages = [page_spec(p) for p in range(n_pg)]
    return pl.pallas_call(
        functools.partial(_attn_sample_kernel, n_pg=n_pg, layer=layer),
        grid_spec=pltpu.PrefetchScalarGridSpec(
            num_scalar_prefetch=1,
            grid=(bd, n_pages // n_pg),
            in_specs=[pl.BlockSpec(memory_space=pltpu.SMEM), per_b((2 * t * N_HEADS, HEAD_W)),
                      per_b((t, SEG_W)), per_b((t, SEG_W))] + pages + pages,
            out_specs=per_b((t, SEG_W)),
            scratch_shapes=[pltpu.VMEM((2 * t * N_HEADS, 1), f32), pltpu.VMEM((2 * t * N_HEADS, 1), f32),
                            pltpu.VMEM((2 * t * N_HEADS, HEAD_W), f32)]),
        out_shape=jax.ShapeDtypeStruct((bd, t, SEG_W), f32),
        compiler_params=pltpu.CompilerParams(dimension_semantics=("arbitrary", "arbitrary"),
                                             vmem_limit_bytes=VMEM_LIMIT),
        name="attn_sample",
    )(page_table.reshape(-1), lam, qs, kn, vn, *([cache_k] * n_pg), *([cache_v] * n_pg))


def _route(logits):
    lane = lax.broadcasted_iota(jnp.int32, logits.shape, 1).astype(f32)
    big = float(ROUTER_W)
    is_g = lane < N_GROUPS
    m1 = jnp.max(jnp.where(is_g, logits, -jnp.inf), axis=-1, keepdims=True)
    grp = jnp.min(jnp.where(is_g & (logits == m1), lane, big), axis=-1, keepdims=True)
    p_grp = 1.0 / jnp.sum(jnp.where(is_g, jnp.exp(logits - m1), 0.0), axis=-1, keepdims=True)
    lo = GATE_LANE0 + EXPERTS_PER_GROUP * grp
    in_g = (lane >= lo) & (lane < lo + EXPERTS_PER_GROUP)
    v1 = jnp.max(jnp.where(in_g, logits, -jnp.inf), axis=-1, keepdims=True)
    i1 = jnp.min(jnp.where(in_g & (logits == v1), lane, big), axis=-1, keepdims=True)
    rest = in_g & (lane != i1)
    v2 = jnp.max(jnp.where(rest, logits, -jnp.inf), axis=-1, keepdims=True)
    i2 = jnp.min(jnp.where(rest & (logits == v2), lane, big), axis=-1, keepdims=True)
    e = jnp.exp(v2 - v1)
    w1 = 1.0 / (1.0 + e)
    w2 = e / (1.0 + e)
    gates = jnp.where(lane == i1, p_grp * w1, 0.0) + jnp.where(lane == i2, p_grp * w2, 0.0)
    a = jnp.minimum(i1, i2) - lo
    b = jnp.maximum(i1, i2) - lo
    pair = jnp.where(a == 0.0, b - 1.0, jnp.where(a == 1.0, b + 1.0, 5.0))
    return jnp.where(lane == 0.0, grp, jnp.where(lane == 1.0, grp * float(len(PAIR_LO)) + pair, gates))


def _mix_kernel(oh_ref, od_ref, gh_ref, h_ref, hnw_ref, dnw_ref, wo_ref, fnw_ref, wrh_ref, wrl_ref, br_ref,
                hx_ref, rt_ref, *, od_scale):
    parts = []
    for hh in range(N_HEADS):
        sl = slice(hh * HEAD_W, (hh + 1) * HEAD_W)
        gate = 1.0 / (1.0 + jnp.exp(-gh_ref[:, sl]))
        parts.append((_rms(oh_ref[:, sl], hnw_ref[...]) * gate).astype(bf16))
    for hh in range(N_HEADS):
        sl = slice(hh * HEAD_W, (hh + 1) * HEAD_W)
        parts.append((_rms(od_ref[:, sl], dnw_ref[...]) * od_scale).astype(bf16))
    h2 = h_ref[...] + _dot(jnp.concatenate(parts, axis=1), wo_ref[...])
    xn = _rms(h2, fnw_ref[...])
    x_hi = xn.astype(bf16)
    x_lo = (xn - x_hi.astype(f32)).astype(bf16)
    logits = _dot(x_hi, wrh_ref[...]) + _dot(x_lo, wrh_ref[...]) + _dot(x_hi, wrl_ref[...]) + br_ref[...]
    tm = h2.shape[0]
    for c in range(D_MODEL // HEAD_W):
        hx_ref[pl.ds(c, tm, stride=ROW_CHUNKS), :] = h2[:, c * HEAD_W:(c + 1) * HEAD_W]
    router = _route(logits)
    hx_ref[pl.ds(D_MODEL // HEAD_W, tm, stride=ROW_CHUNKS), :] = router
    rt_ref[...] = router


def _mix(o_h, o_d, g_h, h, hnw, dnw, wo_bf, fnw, wr_hi, wr_lo, br, od_scale, tm):
    n = o_h.shape[0]
    row = lambda i: (i, 0)
    fixed = lambda i: (0, 0)
    seg = pl.BlockSpec((tm, SEG_W), row)
    return pl.pallas_call(
        functools.partial(_mix_kernel, od_scale=od_scale),
        grid=(n // tm,),
        in_specs=[seg, seg, seg, pl.BlockSpec((tm, D_MODEL), row),
                  pl.BlockSpec((1, HEAD_W), fixed), pl.BlockSpec((1, HEAD_W), fixed),
                  pl.BlockSpec((D_MODEL, D_MODEL), fixed), pl.BlockSpec((1, D_MODEL), fixed),
                  pl.BlockSpec((D_MODEL, ROUTER_W), fixed), pl.BlockSpec((D_MODEL, ROUTER_W), fixed),
                  pl.BlockSpec((1, ROUTER_W), fixed)],
        out_specs=[pl.BlockSpec((tm * ROW_CHUNKS, HEAD_W), row), pl.BlockSpec((tm, ROUTER_W), row)],
        out_shape=[jax.ShapeDtypeStruct((n * ROW_CHUNKS, HEAD_W), f32), jax.ShapeDtypeStruct((n, ROUTER_W), f32)],
        compiler_params=pltpu.CompilerParams(dimension_semantics=("arbitrary",), vmem_limit_bytes=VMEM_LIMIT),
        name="mix",
    )(o_h, o_d, g_h, h, hnw, dnw, wo_bf, fnw, wr_hi, wr_lo, br)


def _moe_kernel(ids_ref, exp_ref, cnt_ref, hx_hbm, fnw_ref, onw_ref, *rest, tm, n_e, final_norm):
    wg_refs, wu_refs, wd_refs = rest[:n_e], rest[n_e:2 * n_e], rest[2 * n_e:3 * n_e]
    out_hbm, xbuf, ybuf, gsem, ssem = rest[3 * n_e:]
    i = pl.program_id(0)
    n_tiles = pl.num_programs(0)
    slot = i % 2

    def gather_copy(tile, sl, r):
        tok = ids_ref[tile * tm + r]
        return pltpu.make_async_copy(hx_hbm.at[pl.ds(tok * ROW_CHUNKS, ROW_CHUNKS)],
                                     xbuf.at[sl, pl.ds(r * ROW_CHUNKS, ROW_CHUNKS)], gsem.at[sl])

    def scatter_copy(tile, sl, r):
        tok = ids_ref[tile * tm + r]
        return pltpu.make_async_copy(ybuf.at[sl, pl.ds(r, 1)], out_hbm.at[pl.ds(tok, 1)], ssem.at[sl])

    def for_rows(n, fn):
        def body(r, carry):
            fn(r)
            return carry
        if isinstance(n, int):
            lax.fori_loop(0, n, body, 0, unroll=8)
        else:
            lax.fori_loop(0, n, body, 0)

    def gather_wait(sl):
        pltpu.make_async_copy(hx_hbm.at[pl.ds(0, tm * ROW_CHUNKS)], xbuf.at[sl], gsem.at[sl]).wait()

    def scatter_start(tile, sl):
        cnt = cnt_ref[tile]

        @pl.when(cnt == tm)
        def _():
            for_rows(tm, lambda r: scatter_copy(tile, sl, r).start())

        @pl.when(cnt < tm)
        def _():
            for_rows(cnt, lambda r: scatter_copy(tile, sl, r).start())

    def scatter_wait(tile, sl):
        cnt = cnt_ref[tile]

        @pl.when(cnt == tm)
        def _():
            pltpu.make_async_copy(ybuf.at[sl], out_hbm.at[pl.ds(0, tm)], ssem.at[sl]).wait()

        @pl.when(cnt < tm)
        def _():
            for_rows(cnt, lambda r: scatter_copy(tile, sl, r).wait())

    @pl.when(i == 0)
    def _():
        for_rows(tm, lambda r: gather_copy(0, 0, r).start())

    @pl.when(i + 1 < n_tiles)
    def _():
        for_rows(tm, lambda r: gather_copy(i + 1, 1 - slot, r).start())

    gather_wait(slot)

    @pl.when(i >= 2)
    def _():
        scatter_wait(i - 2, slot)

    @pl.when(cnt_ref[i] > 0)
    def _():
        xs = xbuf.at[slot]
        chunk = lambda c: xs[pl.ds(c, tm, stride=ROW_CHUNKS), :]
        h2 = jnp.concatenate([chunk(c) for c in range(D_MODEL // HEAD_W)], axis=1)
        router = chunk(D_MODEL // HEAD_W)
        xn = _rms(h2, fnw_ref[...]).astype(bf16)
        lane = lax.broadcasted_iota(jnp.int32, router.shape, 1)
        acc = h2
        for e in range(n_e):
            gate_lane = GATE_LANE0 + exp_ref[i * n_e + e]
            gate = jnp.sum(jnp.where(lane == gate_lane, router, 0.0), axis=-1, keepdims=True)
            a = _dot(xn, wg_refs[e][...])
            hid = a / (1.0 + jnp.exp(-a)) * _dot(xn, wu_refs[e][...])
            acc = acc + _dot((hid * gate).astype(bf16), wd_refs[e][...])
        if final_norm:
            acc = _rms(acc, onw_ref[...])
        ybuf[slot] = acc

    scatter_start(i, slot)

    @pl.when(i == n_tiles - 1)
    def _():
        @pl.when(i >= 1)
        def _():
            scatter_wait(i - 1, 1 - slot)
        scatter_wait(i, slot)


def _dispatch(cls_f32, n_cls, tm):
    n = cls_f32.shape[0]
    n_tiles = n // tm + n_cls
    grp = cls_f32.astype(jnp.int32)
    onehot = (grp[:, None] == jnp.arange(n_cls, dtype=jnp.int32)[None, :]).astype(jnp.int32)
    csum = jnp.cumsum(onehot, axis=0)
    cnt = csum[-1]
    rank = jnp.sum(csum * onehot, axis=1) - 1
    tiles_g = (cnt + tm - 1) // tm
    tile_end = jnp.cumsum(tiles_g)
    tile_start = tile_end - tiles_g
    pos = jnp.sum(onehot * (tile_start * tm)[None, :], axis=1) + rank
    tile = jnp.arange(n_tiles, dtype=jnp.int32)
    tile_grp_raw = jnp.sum((tile[:, None] >= tile_end[None, :]).astype(jnp.int32), axis=1)
    last_grp = jnp.max(jnp.where(cnt > 0, jnp.arange(n_cls, dtype=jnp.int32), 0))
    tile_grp = jnp.minimum(tile_grp_raw, last_grp)
    in_range = tile_grp_raw < n_cls
    rows_left = cnt[tile_grp] - (tile - tile_start[tile_grp]) * tm
    tile_cnt = jnp.where(in_range, jnp.clip(rows_left, 0, tm), 0).astype(jnp.int32)
    ids = jnp.zeros((n_tiles * tm,), jnp.int32).at[pos].set(jnp.arange(n, dtype=jnp.int32))
    return ids, tile_grp, tile_cnt


def _moe(hx, router, fnw, onw, wg_bf, wu_bf, wd_bf, layer, tm, final_norm, by_pair):
    n = router.shape[0]
    if by_pair:
        n_e = 2
        ids, tile_cls, tile_cnt = _dispatch(router[:, 1], N_GROUPS * len(PAIR_LO), tm)
        grp, pair = tile_cls // len(PAIR_LO), tile_cls % len(PAIR_LO)
        members = [jnp.array(PAIR_LO, jnp.int32)[pair], jnp.array(PAIR_HI, jnp.int32)[pair]]
    else:
        n_e = EXPERTS_PER_GROUP
        ids, grp, tile_cnt = _dispatch(router[:, 0], N_GROUPS, tm)
        members = [jnp.full_like(grp, e) for e in range(n_e)]
    tile_exp = jnp.stack([grp * EXPERTS_PER_GROUP + m for m in members], axis=1).reshape(-1).astype(jnp.int32)
    n_tiles = tile_cnt.shape[0]
    fixed = lambda i, ids_r, exp_r, cnt_r: (0, 0)

    def wspecs(shape):
        return [pl.BlockSpec((None, None) + shape, lambda i, ids_r, exp_r, cnt_r, e=e: (layer, exp_r[i * n_e + e], 0, 0))
                for e in range(n_e)]

    return pl.pallas_call(
        functools.partial(_moe_kernel, tm=tm, n_e=n_e, final_norm=final_norm),
        grid_spec=pltpu.PrefetchScalarGridSpec(
            num_scalar_prefetch=3,
            grid=(n_tiles,),
            in_specs=[pl.BlockSpec(memory_space=pl.ANY),
                      pl.BlockSpec((1, D_MODEL), fixed), pl.BlockSpec((1, D_MODEL), fixed)]
                     + wspecs((D_MODEL, EXPERT_FF)) + wspecs((D_MODEL, EXPERT_FF)) + wspecs((EXPERT_FF, D_MODEL)),
            out_specs=pl.BlockSpec(memory_space=pl.ANY),
            scratch_shapes=[pltpu.VMEM((2, tm * ROW_CHUNKS, HEAD_W), f32), pltpu.VMEM((2, tm, D_MODEL), f32),
                            pltpu.SemaphoreType.DMA((2,)), pltpu.SemaphoreType.DMA((2,))]),
        out_shape=jax.ShapeDtypeStruct((n, D_MODEL), f32),
        compiler_params=pltpu.CompilerParams(dimension_semantics=("arbitrary",), vmem_limit_bytes=VMEM_LIMIT),
        name="moe",
    )(ids, tile_exp, tile_cnt, hx, fnw, onw, *([wg_bf] * n_e), *([wu_bf] * n_e), *([wd_bf] * n_e))


def _stacked_sample_queries(qd):
    bd, t = qd.shape[:2]
    q5 = qd.reshape(bd, t, N_HEADS, 2, DA_DK)
    eye = jnp.eye(2, dtype=qd.dtype)
    qz = q5[:, :, :, :, None, :] * eye[None, None, None, :, :, None]
    return qz.transpose(0, 2, 3, 1, 4, 5).reshape(bd, N_HEADS * 2 * t, HEAD_W)


def kernel(x_prompt, x_sample, cache_k, cache_v, state_hgrn, page_table, attn_norm_w, w_in, hgrn_lb, hgrn_norm_w,
           diff_lambda, diff_norm_w, w_o, ffn_norm_w, w_r1, b_r1, w_r2, b_r2, w_gate, w_up, w_down, final_norm_w):
    b, s = x_prompt.shape[:2]
    bd, t = x_sample.shape[:2]
    depth = w_in.shape[0]
    n_pages = page_table.shape[1]
    past_len = n_pages * PAGE_SIZE
    assert x_prompt.shape[2] == D_MODEL and w_in.shape[2] == N_SEG * SEG_W and t <= SAMPLE_PAD
    tp = SAMPLE_PAD

    tabs_p = _rope_tables(jnp.arange(s, dtype=jnp.int32))
    tabs_s = _rope_tables(jnp.tile(past_len + jnp.arange(t, dtype=jnp.int32), bd))
    p_lb = jax.nn.softmax(hgrn_lb.astype(f32), axis=0)
    lb_all = jnp.cumsum(p_lb, axis=0) - p_lb[0:1]

    w_in_bf = w_in.astype(bf16)
    w_o_bf = w_o.astype(bf16)
    wg_bf, wu_bf, wd_bf = w_gate.astype(bf16), w_up.astype(bf16), w_down.astype(bf16)
    pad_r = ROUTER_W - N_GROUPS - N_EXPERTS
    w_r = jnp.concatenate([w_r1, w_r2, jnp.zeros((depth, D_MODEL, pad_r), f32)], axis=2)
    w_r_hi = w_r.astype(bf16)
    w_r_lo = (w_r - w_r_hi.astype(f32)).astype(bf16)
    b_r = jnp.concatenate([b_r1, b_r2, jnp.zeros((depth, pad_r), f32)], axis=1)

    tm_p = 256 if (b * s) % 256 == 0 else b * s
    tq = 512 if s % 512 == 0 else s
    tb_p = 512 if s % 512 == 0 else s
    tm_moe_p = 256 if (b * s) % 256 == 0 else b * s
    n_s = bd * t
    n_pg = next(p for p in (16, 8, 4, 2, 1) if n_pages % p == 0)
    tm_proj = 512 if s % 512 == 0 else tm_p
    q_scale = DA_DK ** -0.5 * LOG2E

    hp = x_prompt.reshape(b * s, D_MODEL)
    hs = x_sample.reshape(n_s, D_MODEL)
    zeros_state = jnp.zeros((b, N_HEADS, HEAD_W, HEAD_W), f32)
    kp_l, vp_l, sp_l, ks_l, vs_l, ss_l = [], [], [], [], [], []
    for l in range(depth):
        lam_init = 0.8 - 0.6 * math.exp(-0.3 * l)
        dl = diff_lambda[l].astype(f32)
        lam = (jnp.exp(jnp.sum(dl[0] * dl[1])) - jnp.exp(jnp.sum(dl[2] * dl[3])) + lam_init).reshape(1)
        lb = lb_all[l].reshape(1, SEG_W)
        nw = attn_norm_w[l].reshape(1, D_MODEL)
        fnw = ffn_norm_w[l].reshape(1, D_MODEL)
        onw = final_norm_w.reshape(1, D_MODEL)
        hnw = hgrn_norm_w[l].reshape(1, HEAD_W)
        dnw = diff_norm_w[l].reshape(1, HEAD_W)
        last = l == depth - 1

        def tail(o_h, o_d, g_h, h, tm_mix, tm_moe, by_pair):
            hx, router = _mix(o_h, o_d, g_h, h, hnw, dnw, w_o_bf[l], fnw, w_r_hi[l], w_r_lo[l],
                              b_r[l].reshape(1, ROUTER_W), 1.0 - lam_init, tm_mix)
            return _moe(hx, router, fnw, onw, wg_bf, wu_bf, wd_bf, l, tm_moe, last, by_pair)

        qh, kh, ih, lf, gh, qd, kd, vd, kb, vb = _proj(hp, b * s, nw, w_in_bf[l], lb, tabs_p, tm_proj, q_scale, seq=s)
        o_h, s_fin = _hgrn(qh, kh, ih, lf, zeros_state, s, tb_p, HG_CHUNK)
        o_d = _attn_prompt(lam, qd, kb, vb, b, s, tq)
        hp = tail(o_h, o_d, gh, hp, tm_p, tm_moe_p, True)
        kp_l.append(kd.reshape(b, s, N_HEADS, HEAD_W))
        vp_l.append(vd.reshape(b, s, N_HEADS, HEAD_W))
        sp_l.append(s_fin)

        qh, kh, ih, lf, gh, qd, kd, vd, kb, vb = _proj(hs, n_s, nw, w_in_bf[l], lb, tabs_s, n_s, q_scale)
        pad = lambda a: jnp.pad(a.reshape(bd, t, SEG_W), ((0, 0), (0, tp - t), (0, 0)))
        flat = lambda a: pad(a).reshape(bd * tp, SEG_W)
        o_h, s_fin = _hgrn(flat(qh), flat(kh), flat(ih), flat(lf), state_hgrn[l].astype(f32), tp, tp, tp)
        o_h = o_h.reshape(bd, tp, SEG_W)[:, :t].reshape(n_s, SEG_W)
        o_d = _attn_sample(page_table, lam, _stacked_sample_queries(pad(qd)), pad(kb), pad(vb),
                           cache_k, cache_v, l, n_pg)
        o_d = o_d[:, :t].reshape(n_s, SEG_W)
        hs = tail(o_h, o_d, gh, hs, n_s, n_s, False)
        ks_l.append(kd.reshape(bd, t, N_HEADS, HEAD_W))
        vs_l.append(vd.reshape(bd, t, N_HEADS, HEAD_W))
        ss_l.append(s_fin)

    return (hp.reshape(b, s, D_MODEL), hs.reshape(bd, t, D_MODEL), jnp.stack(kp_l), jnp.stack(vp_l),
            jnp.stack(sp_l), jnp.stack(ks_l), jnp.stack(vs_l), jnp.stack(ss_l))
```

```python
import functools
import math

import jax
import jax.numpy as jnp
from jax import lax
from jax.experimental import pallas as pl
from jax.experimental.pallas import tpu as pltpu

f32 = jnp.float32
bf16 = jnp.bfloat16

D_MODEL = 1024
N_HEADS = 4
HEAD_W = 128
SEG_W = N_HEADS * HEAD_W
N_SEG = 7
DA_DK = 64
ROT_DIM = DA_DK // 4
ROPE_THETA = 500000.0
PAGE_SIZE = 128
N_GROUPS = 4
EXPERTS_PER_GROUP = 4
N_EXPERTS = N_GROUPS * EXPERTS_PER_GROUP
PAIR_LO = (0, 0, 0, 1, 1, 2)
PAIR_HI = (1, 2, 3, 2, 3, 3)
EXPERT_FF = D_MODEL // 2
NORM_EPS = 1e-6
NEG = -1e30
F_MIN = 1e-30

HG_CHUNK = 64
LOG2E = math.log2(math.e)
SAMPLE_PAD = 8
GATE_LANE0 = N_GROUPS
ROUTER_W = 128
ROW_CHUNKS = (D_MODEL + ROUTER_W) // HEAD_W

VMEM_LIMIT = 48 * 1024 * 1024


def _dot(a, b):
    return jnp.dot(a, b, preferred_element_type=f32)


def _dot_nt(a, b):
    return lax.dot_general(a, b, (((1,), (1,)), ((), ())), preferred_element_type=f32)


def _dot_tn(a, b):
    return lax.dot_general(a, b, (((0,), (0,)), ((), ())), preferred_element_type=f32)


def _rms(x, w):
    return x * lax.rsqrt(jnp.mean(x * x, axis=-1, keepdims=True) + NORM_EPS) * w


def _split3(x):
    hi = x.astype(bf16)
    r1 = x - hi.astype(f32)
    mid = r1.astype(bf16)
    lo = (r1 - mid.astype(f32)).astype(bf16)
    return hi, mid, lo


def _proj_kernel(h_ref, nw_ref, w_ref, lb_ref, c_ref, sa_ref, sb_ref,
                 qh_ref, kh_ref, ih_ref, lf_ref, gh_ref, qd_ref, kd_ref, vd_ref, kb_ref, vb_ref, *, q_scale, transposed):
    tm = h_ref.shape[0]
    xn = _rms(h_ref[...], nw_ref[...]).astype(bf16)

    def seg(i):
        return _dot(xn, w_ref[:, i * SEG_W:(i + 1) * SEG_W])

    qh_ref[...] = seg(0) * (HEAD_W ** -0.5)
    hf = seg(1)
    lb = lb_ref[...]
    e = jnp.exp(-jnp.abs(hf))
    r = 1.0 / (1.0 + e)
    pos = hf >= 0.0
    sig = jnp.where(pos, r, e * r)
    nsig = jnp.where(pos, e * r, r)
    f = lb + (1.0 - lb) * sig
    lf_ref[...] = jnp.log(jnp.maximum(f, F_MIN)) * LOG2E
    kh_ref[...] = (1.0 - lb) * nsig
    ih_ref[...] = seg(2)
    gh_ref[...] = seg(3)

    c = c_ref[...]
    sa = sa_ref[...]
    sb = sb_ref[...]

    def rope(z, hh):
        zz = z[:, hh * HEAD_W:(hh + 1) * HEAD_W]
        return zz * c + pltpu.roll(zz, HEAD_W - ROT_DIM // 2, 1) * sa + pltpu.roll(zz, ROT_DIM // 2, 1) * sb

    zq = seg(4)
    zk = seg(5)
    for hh in range(N_HEADS):
        sl = slice(hh * HEAD_W, (hh + 1) * HEAD_W)
        qr = rope(zq, hh) * q_scale
        if transposed:
            qd_ref[sl, :] = qr.T.astype(bf16)
        else:
            qd_ref[:, sl] = qr.astype(bf16)
        kr = rope(zk, hh)
        kd_ref[pl.ds(hh, tm, stride=N_HEADS), :] = kr
        kb_ref[:, sl] = kr.astype(bf16)
    vd = seg(6)
    for hh in range(N_HEADS):
        vd_ref[pl.ds(hh, tm, stride=N_HEADS), :] = vd[:, hh * HEAD_W:(hh + 1) * HEAD_W]
    vb_ref[...] = vd.T.astype(bf16) if transposed else vd.astype(bf16)


def _proj(h, n, nw, w_bf, lb, tabs, tm, q_scale, seq=None):
    npos = tabs[0].shape[0] // tm
    row = lambda i: (i, 0)
    fixed = lambda i: (0, 0)
    tab = lambda i: (i % npos, 0)
    seg_f32 = jax.ShapeDtypeStruct((n, SEG_W), f32)
    seg_bf = jax.ShapeDtypeStruct((n, SEG_W), bf16)
    seg_spec = pl.BlockSpec((tm, SEG_W), row)
    rows_shape = jax.ShapeDtypeStruct((n * N_HEADS, HEAD_W), f32)
    rows_spec = pl.BlockSpec((tm * N_HEADS, HEAD_W), row)
    if seq is None:
        t_shape, t_spec = seg_bf, seg_spec
    else:
        nb = seq // tm
        t_shape = jax.ShapeDtypeStruct((n // seq, SEG_W, seq), bf16)
        t_spec = pl.BlockSpec((None, SEG_W, tm), lambda i: (i // nb, 0, i % nb))
    return pl.pallas_call(
        functools.partial(_proj_kernel, q_scale=q_scale, transposed=seq is not None),
        grid=(n // tm,),
        in_specs=[pl.BlockSpec((tm, D_MODEL), row),
                  pl.BlockSpec((1, D_MODEL), fixed),
                  pl.BlockSpec((D_MODEL, N_SEG * SEG_W), fixed),
                  pl.BlockSpec((1, SEG_W), fixed),
                  pl.BlockSpec((tm, HEAD_W), tab),
                  pl.BlockSpec((tm, HEAD_W), tab),
                  pl.BlockSpec((tm, HEAD_W), tab)],
        out_specs=[seg_spec] * 5 + [t_spec, rows_spec, rows_spec, seg_spec, t_spec],
        out_shape=[seg_f32] * 5 + [t_shape, rows_shape, rows_shape, seg_bf, t_shape],
        compiler_params=pltpu.CompilerParams(dimension_semantics=("arbitrary",), vmem_limit_bytes=VMEM_LIMIT),
        name="proj",
    )(h, nw, w_bf, lb, *tabs)


def _rope_tables(pos):
    half = ROT_DIM // 2
    inv = ROPE_THETA ** (-jnp.arange(0, ROT_DIM, 2, dtype=f32) / ROT_DIM)
    ang = pos.astype(f32)[:, None] * inv[None, :]
    cos, sin = jnp.cos(ang), jnp.sin(ang)
    t = pos.shape[0]
    rest = DA_DK - ROT_DIM
    c64 = jnp.concatenate([cos, cos, jnp.ones((t, rest), f32)], axis=1)
    sa64 = jnp.concatenate([-sin, jnp.zeros((t, half + rest), f32)], axis=1)
    sb64 = jnp.concatenate([jnp.zeros((t, half), f32), sin, jnp.zeros((t, rest), f32)], axis=1)
    return tuple(jnp.tile(a, (1, HEAD_W // DA_DK)) for a in (c64, sa64, sb64))


def _hgrn_chunk(q, k, v, lf, st, chunk, consts):
    sel, pair_masks, halves = consts
    heads = [slice(hh * HEAD_W, (hh + 1) * HEAD_W) for hh in range(N_HEADS)]
    terms = jnp.concatenate(_split3(lf), axis=1)
    gs3 = _dot(sel, terms)
    gs = gs3[:, :SEG_W] + gs3[:, SEG_W:2 * SEG_W] + gs3[:, 2 * SEG_W:]
    g = gs[:chunk]
    g_last = g[chunk - 1:chunk, :]
    qx = (q * jnp.exp2(g)).astype(bf16)
    q_bf, k_bf, v_bf = q.astype(bf16), k.astype(bf16), v.astype(bf16)
    st_bf = [s.astype(bf16) for s in st]
    o = [_dot_nt(qx[:, sl], st_bf[hh]) for hh, sl in enumerate(heads)]
    a = [_dot_nt(q_bf[:, sl], k_bf[:, sl]) * pair_masks[0] for sl in heads]
    n_mm = 1
    for lvl, h in enumerate(halves, start=1):
        if h % 8 == 0:
            ref = jnp.concatenate([jnp.broadcast_to(g[r0 + h - 1:r0 + h, :], (2 * h, SEG_W))
                                   for r0 in range(0, chunk, 2 * h)], axis=0)
        else:
            ref = gs[n_mm * chunk:(n_mm + 1) * chunk]
            n_mm += 1
        e = jnp.exp2(-jnp.abs(g - ref))
        qe, ke = (q * e).astype(bf16), (k * e).astype(bf16)
        a = [a[hh] + _dot_nt(qe[:, sl], ke[:, sl]) * pair_masks[lvl] for hh, sl in enumerate(heads)]
    kdec = (k * jnp.exp2(g_last - g)).astype(bf16)
    decay = jnp.exp2(g_last)
    o = jnp.concatenate([o[hh] + _dot(a[hh].astype(bf16), v_bf[:, sl]) for hh, sl in enumerate(heads)], axis=1)
    st_new = [st[hh] * decay[:, sl] + _dot_tn(v_bf[:, sl], kdec[:, sl]) for hh, sl in enumerate(heads)]
    return o, st_new


def _hgrn_consts(chunk):
    r = lax.broadcasted_iota(jnp.int32, (chunk, chunk), 0)
    c = lax.broadcasted_iota(jnp.int32, (chunk, chunk), 1)
    sels = [jnp.where(c <= r, 1.0, 0.0)]
    masks = [jnp.where(c == r, 1.0, 0.0)]
    halves = []
    h = chunk // 2
    while h >= 1:
        blk = -(2 * h)
        if h % 8:
            sels.append(jnp.where(c <= (r & blk) + (h - 1), 1.0, 0.0))
        same_block = (r & blk) == (c & blk)
        masks.append(jnp.where(same_block & ((r & h) != 0) & ((c & h) == 0), 1.0, 0.0))
        halves.append(h)
        h //= 2
    return jnp.concatenate(sels, axis=0).astype(bf16), masks, halves


def _hgrn_kernel(q_ref, k_ref, v_ref, lf_ref, s0_ref, o_ref, sfin_ref, st_ref, *, chunk, n_chunks):
    j = pl.program_id(1)
    consts = _hgrn_consts(chunk)

    @pl.when(j == 0)
    def _():
        for hh in range(N_HEADS):
            st_ref[hh] = s0_ref[hh].T

    def body(c, carry):
        rows = pl.ds(pl.multiple_of(c * chunk, chunk), chunk)
        o, st_new = _hgrn_chunk(q_ref[rows, :], k_ref[rows, :], v_ref[rows, :], lf_ref[rows, :],
                                [st_ref[hh] for hh in range(N_HEADS)], chunk, consts)
        o_ref[rows, :] = o
        for hh in range(N_HEADS):
            st_ref[hh] = st_new[hh]
        return carry

    lax.fori_loop(0, n_chunks, body, 0, unroll=2 if n_chunks % 2 == 0 else 1)

    @pl.when(j == pl.num_programs(1) - 1)
    def _():
        for hh in range(N_HEADS):
            sfin_ref[hh] = st_ref[hh].T


def _hgrn(q, k, v, lf, s0, t, tb, chunk):
    n = q.shape[0]
    b = n // t
    nj = t // tb
    tok = pl.BlockSpec((tb, SEG_W), lambda bi, j: (bi * nj + j, 0))
    st = pl.BlockSpec((None, N_HEADS, HEAD_W, HEAD_W), lambda bi, j: (bi, 0, 0, 0))
    return pl.pallas_call(
        functools.partial(_hgrn_kernel, chunk=chunk,n_chunks=tb // chunk),
        grid=(b, nj),
        in_specs=[tok, tok, tok, tok, st],
        out_specs=[tok, st],
        out_shape=[jax.ShapeDtypeStruct((n, SEG_W), f32), jax.ShapeDtypeStruct(s0.shape, f32)],
        scratch_shapes=[pltpu.VMEM((N_HEADS, HEAD_W, HEAD_W), f32)],
        compiler_params=pltpu.CompilerParams(dimension_semantics=("arbitrary", "arbitrary"),
                                             vmem_limit_bytes=VMEM_LIMIT),
        name="hgrn",
    )(q, k, v, lf, s0)


def _stack_maps(q):
    lane = lax.broadcasted_iota(jnp.int32, q.shape, 1)
    zero = jnp.zeros_like(q)
    return jnp.concatenate([jnp.where(lane < DA_DK, q, zero), jnp.where(lane >= DA_DK, q, zero)], axis=0)


def _softmax_step(s, v_bf, m_ref, l_ref, acc_ref, rows=None):
    sl = slice(None) if rows is None else rows
    m_prev = m_ref[sl, :]
    m_new = jnp.maximum(m_prev, jnp.max(s, axis=-1, keepdims=True))
    alpha = jnp.exp2(m_prev - m_new)
    p = jnp.exp2(s - m_new)
    l_ref[sl, :] = alpha * l_ref[sl, :] + jnp.sum(p, axis=-1, keepdims=True)
    acc_ref[sl, :] = alpha * acc_ref[sl, :] + _dot(p.astype(bf16), v_bf)
    m_ref[sl, :] = m_new


def _attn_kernel(lam_ref, qt_ref, k_ref, vt_ref, o_ref, m_ref, l_ref, acc_ref, s_ref, *, tq, n_q):
    i = pl.program_id(2)
    qt = qt_ref[...]
    sub = lax.broadcasted_iota(jnp.int32, qt.shape, 0)
    zero = jnp.zeros_like(qt)
    qs = jnp.concatenate([jnp.where(sub < DA_DK, qt, zero), jnp.where(sub >= DA_DK, qt, zero)], axis=1)
    m_ref[...] = jnp.full(m_ref.shape, NEG, f32)
    l_ref[...] = jnp.zeros(l_ref.shape, f32)
    acc_ref[...] = jnp.zeros(acc_ref.shape, f32)

    def scores(j):
        return _dot(k_ref[j * tq:(j + 1) * tq, :], qs)

    def update(s, vt):
        m_prev = m_ref[...]
        m_new = jnp.maximum(m_prev, jnp.max(s, axis=0, keepdims=True))
        alpha = jnp.exp2(m_prev - m_new)
        p = jnp.exp2(s - m_new)
        l_ref[...] = alpha * l_ref[...] + jnp.sum(p, axis=0, keepdims=True)
        acc_ref[...] = alpha * acc_ref[...] + _dot(vt, p.astype(bf16))
        m_ref[...] = m_new

    s_ref[0] = scores(0)
    for j in range(n_q - 1):
        @pl.when(j < i)
        def _():
            s_ref[(j + 1) % 2] = scores(j + 1)
            update(s_ref[j % 2], vt_ref[:, j * tq:(j + 1) * tq])

    s = s_ref[i % 2]
    key = lax.broadcasted_iota(jnp.int32, s.shape, 0)
    qry = lax.broadcasted_iota(jnp.int32, s.shape, 1) & (tq - 1)
    update(jnp.where(key <= qry, s, NEG), vt_ref[:, pl.ds(pl.multiple_of(i * tq, tq), tq)])
    o = acc_ref[...] / l_ref[...]
    o_ref[...] = (o[:, :tq] - lam_ref[0] * o[:, tq:]).T


def _attn_prompt(lam, qt_bf, k_bf, vt_bf, b, s, tq):
    k3 = k_bf.reshape(b, s, SEG_W)
    qspec = pl.BlockSpec((None, HEAD_W, tq), lambda bi, h, i: (bi, h, i))
    kspec = pl.BlockSpec((None, s, HEAD_W), lambda bi, h, i: (bi, 0, h))
    vspec = pl.BlockSpec((None, HEAD_W, s), lambda bi, h, i: (bi, h, 0))
    out = pl.pallas_call(
        functools.partial(_attn_kernel, tq=tq, n_q=s // tq),
        grid=(b, N_HEADS, s // tq),
        in_specs=[pl.BlockSpec(memory_space=pltpu.SMEM), qspec, kspec, vspec],
        out_specs=pl.BlockSpec((None, tq, HEAD_W), lambda bi, h, i: (bi, i, h)),
        out_shape=jax.ShapeDtypeStruct((b, s, SEG_W), f32),
        scratch_shapes=[pltpu.VMEM((1, 2 * tq), f32), pltpu.VMEM((1, 2 * tq), f32),
                        pltpu.VMEM((HEAD_W, 2 * tq), f32), pltpu.VMEM((2, tq, 2 * tq), f32)],
        compiler_params=pltpu.CompilerParams(dimension_semantics=("arbitrary",) * 3, vmem_limit_bytes=VMEM_LIMIT),
        name="attn_prompt",
    )(lam, qt_bf, k3, vt_bf)
    return out.reshape(b * s, SEG_W)


def _attn_sample_kernel(pt_ref, lam_ref, qs_ref, kn_ref, vn_ref, *rest, n_pg, layer):
    del pt_ref, layer
    k_refs = rest[:n_pg]
    v_refs = rest[n_pg:2 * n_pg]
    o_ref, m_ref, l_ref, acc_ref = rest[2 * n_pg:]
    j = pl.program_id(1)
    t = SAMPLE_PAD

    @pl.when(j == 0)
    def _():
        m_ref[...] = jnp.full(m_ref.shape, NEG, f32)
        l_ref[...] = jnp.zeros(l_ref.shape, f32)
        acc_ref[...] = jnp.zeros(acc_ref.shape, f32)

    head_rows = [slice(hh * 2 * t, (hh + 1) * 2 * t) for hh in range(N_HEADS)]
    head_toks = [pl.ds(hh, PAGE_SIZE, stride=N_HEADS) for hh in range(N_HEADS)]
    qs = qs_ref[...]
    s = [_dot_nt(qs[rows, :], jnp.concatenate([kr[toks, :].astype(bf16) for kr in k_refs], axis=0))
         for rows, toks in zip(head_rows, head_toks)]
    m_prev = m_ref[...]
    m_new = jnp.maximum(m_prev, jnp.concatenate([jnp.max(sh, axis=-1, keepdims=True) for sh in s], axis=0))
    alpha = jnp.exp2(m_prev - m_new)
    p = [jnp.exp2(sh - m_new[rows, :]) for sh, rows in zip(s, head_rows)]
    l_ref[...] = alpha * l_ref[...] + jnp.concatenate([jnp.sum(ph, axis=-1, keepdims=True) for ph in p], axis=0)
    pv = [_dot(ph.astype(bf16), jnp.concatenate([vr[toks, :].astype(bf16) for vr in v_refs], axis=0))
          for ph, toks in zip(p, head_toks)]
    acc_ref[...] = alpha * acc_ref[...] + jnp.concatenate(pv, axis=0)
    m_ref[...] = m_new

    @pl.when(j == pl.num_programs(1) - 1)
    def _():
        for hh in range(N_HEADS):
            rows = slice(hh * 2 * t, (hh + 1) * 2 * t)
            sl = slice(hh * HEAD_W, (hh + 1) * HEAD_W)
            s = _dot_nt(qs_ref[rows, :], kn_ref[:, sl].astype(bf16))
            row = lax.broadcasted_iota(jnp.int32, s.shape, 0) & (t - 1)
            col = lax.broadcasted_iota(jnp.int32, s.shape, 1)
            s = jnp.where(col <= row, s, NEG)
            _softmax_step(s, vn_ref[:, sl].astype(bf16), m_ref, l_ref, acc_ref, rows)
            o = acc_ref[rows, :] / l_ref[rows, :]
            o_ref[:, sl] = o[:t] - lam_ref[0] * o[t:]


def _attn_sample(page_table, lam, qs, kn, vn, cache_k, cache_v, layer, n_pg):
    bd, n_pages = page_table.shape
    t = SAMPLE_PAD

    cache_k = cache_k.reshape(cache_k.shape[:2] + (PAGE_SIZE * N_HEADS, HEAD_W))
    cache_v = cache_v.reshape(cache_v.shape[:2] + (PAGE_SIZE * N_HEADS, HEAD_W))

    def page_spec(p):
        return pl.BlockSpec((None, None, PAGE_SIZE * N_HEADS, HEAD_W),
                            lambda bi, j, pt: (layer, pt[bi * n_pages + j * n_pg + p], 0, 0))

    per_b = lambda shape: pl.BlockSpec((None,) + shape, lambda bi, j, pt: (bi, 0, 0))
    pages = [page_spec(p) for p in range(n_pg)]
    return pl.pallas_call(
        functools.partial(_attn_sample_kernel, n_pg=n_pg, layer=layer),
        grid_spec=pltpu.PrefetchScalarGridSpec(
            num_scalar_prefetch=1,
            grid=(bd, n_pages // n_pg),
            in_specs=[pl.BlockSpec(memory_space=pltpu.SMEM), per_b((2 * t * N_HEADS, HEAD_W)),
                      per_b((t, SEG_W)), per_b((t, SEG_W))] + pages + pages,
            out_specs=per_b((t, SEG_W)),
            scratch_shapes=[pltpu.VMEM((2 * t * N_HEADS, 1), f32), pltpu.VMEM((2 * t * N_HEADS, 1), f32),
                            pltpu.VMEM((2 * t * N_HEADS, HEAD_W), f32)]),
        out_shape=jax.ShapeDtypeStruct((bd, t, SEG_W), f32),
        compiler_params=pltpu.CompilerParams(dimension_semantics=("arbitrary", "arbitrary"),
                                             vmem_limit_bytes=VMEM_LIMIT),
        name="attn_sample",
    )(page_table.reshape(-1), lam, qs, kn, vn, *([cache_k] * n_pg), *([cache_v] * n_pg))


def _route(logits):
    lane = lax.broadcasted_iota(jnp.int32, logits.shape, 1).astype(f32)
    big = float(ROUTER_W)
    is_g = lane < N_GROUPS
    m1 = jnp.max(jnp.where(is_g, logits, -jnp.inf), axis=-1, keepdims=True)
    grp = jnp.min(jnp.where(is_g & (logits == m1), lane, big), axis=-1, keepdims=True)
    p_grp = 1.0 / jnp.sum(jnp.where(is_g, jnp.exp(logits - m1), 0.0), axis=-1, keepdims=True)
    lo = GATE_LANE0 + EXPERTS_PER_GROUP * grp
    in_g = (lane >= lo) & (lane < lo + EXPERTS_PER_GROUP)
    v1 = jnp.max(jnp.where(in_g, logits, -jnp.inf), axis=-1, keepdims=True)
    i1 = jnp.min(jnp.where(in_g & (logits == v1), lane, big), axis=-1, keepdims=True)
    rest = in_g & (lane != i1)
    v2 = jnp.max(jnp.where(rest, logits, -jnp.inf), axis=-1, keepdims=True)
    i2 = jnp.min(jnp.where(rest & (logits == v2), lane, big), axis=-1, keepdims=True)
    e = jnp.exp(v2 - v1)
    w1 = 1.0 / (1.0 + e)
    w2 = e / (1.0 + e)
    gates = jnp.where(lane == i1, p_grp * w1, 0.0) + jnp.where(lane == i2, p_grp * w2, 0.0)
    a = jnp.minimum(i1, i2) - lo
    b = jnp.maximum(i1, i2) - lo
    pair = jnp.where(a == 0.0, b - 1.0, jnp.where(a == 1.0, b + 1.0, 5.0))
    return jnp.where(lane == 0.0, grp, jnp.where(lane == 1.0, grp * float(len(PAIR_LO)) + pair, gates))


def _mix_kernel(oh_ref, od_ref, gh_ref, h_ref, hnw_ref, dnw_ref, wo_ref, fnw_ref, wrh_ref, wrl_ref, br_ref,
                hx_ref, rt_ref, *, od_scale):
    parts = []
    for hh in range(N_HEADS):
        sl = slice(hh * HEAD_W, (hh + 1) * HEAD_W)
        gate = 1.0 / (1.0 + jnp.exp(-gh_ref[:, sl]))
        parts.append((_rms(oh_ref[:, sl], hnw_ref[...]) * gate).astype(bf16))
    for hh in range(N_HEADS):
        sl = slice(hh * HEAD_W, (hh + 1) * HEAD_W)
        parts.append((_rms(od_ref[:, sl], dnw_ref[...]) * od_scale).astype(bf16))
    h2 = h_ref[...] + _dot(jnp.concatenate(parts, axis=1), wo_ref[...])
    xn = _rms(h2, fnw_ref[...])
    x_hi = xn.astype(bf16)
    x_lo = (xn - x_hi.astype(f32)).astype(bf16)
    logits = _dot(x_hi, wrh_ref[...]) + _dot(x_lo, wrh_ref[...]) + _dot(x_hi, wrl_ref[...]) + br_ref[...]
    tm = h2.shape[0]
    for c in range(D_MODEL // HEAD_W):
        hx_ref[pl.ds(c, tm, stride=ROW_CHUNKS), :] = h2[:, c * HEAD_W:(c + 1) * HEAD_W]
    router = _route(logits)
    hx_ref[pl.ds(D_MODEL // HEAD_W, tm, stride=ROW_CHUNKS), :] = router
    rt_ref[...] = router


def _mix(o_h, o_d, g_h, h, hnw, dnw, wo_bf, fnw, wr_hi, wr_lo, br, od_scale, tm):
    n = o_h.shape[0]
    row = lambda i: (i, 0)
    fixed = lambda i: (0, 0)
    seg = pl.BlockSpec((tm, SEG_W), row)
    return pl.pallas_call(
        functools.partial(_mix_kernel, od_scale=od_scale),
        grid=(n // tm,),
        in_specs=[seg, seg, seg, pl.BlockSpec((tm, D_MODEL), row),
                  pl.BlockSpec((1, HEAD_W), fixed), pl.BlockSpec((1, HEAD_W), fixed),
                  pl.BlockSpec((D_MODEL, D_MODEL), fixed), pl.BlockSpec((1, D_MODEL), fixed),
                  pl.BlockSpec((D_MODEL, ROUTER_W), fixed), pl.BlockSpec((D_MODEL, ROUTER_W), fixed),
                  pl.BlockSpec((1, ROUTER_W), fixed)],
        out_specs=[pl.BlockSpec((tm * ROW_CHUNKS, HEAD_W), row), pl.BlockSpec((tm, ROUTER_W), row)],
        out_shape=[jax.ShapeDtypeStruct((n * ROW_CHUNKS, HEAD_W), f32), jax.ShapeDtypeStruct((n, ROUTER_W), f32)],
        compiler_params=pltpu.CompilerParams(dimension_semantics=("arbitrary",), vmem_limit_bytes=VMEM_LIMIT),
        name="mix",
    )(o_h, o_d, g_h, h, hnw, dnw, wo_bf, fnw, wr_hi, wr_lo, br)


def _moe_kernel(ids_ref, exp_ref, cnt_ref, hx_hbm, fnw_ref, onw_ref, *rest, tm, n_e, final_norm):
    wg_refs, wu_refs, wd_refs = rest[:n_e], rest[n_e:2 * n_e], rest[2 * n_e:3 * n_e]
    out_hbm, xbuf, ybuf, gsem, ssem = rest[3 * n_e:]
    i = pl.program_id(0)
    n_tiles = pl.num_programs(0)
    slot = i % 2

    def gather_copy(tile, sl, r):
        tok = ids_ref[tile * tm + r]
        return pltpu.make_async_copy(hx_hbm.at[pl.ds(tok * ROW_CHUNKS, ROW_CHUNKS)],
                                     xbuf.at[sl, pl.ds(r * ROW_CHUNKS, ROW_CHUNKS)], gsem.at[sl])

    def scatter_copy(tile, sl, r):
        tok = ids_ref[tile * tm + r]
        return pltpu.make_async_copy(ybuf.at[sl, pl.ds(r, 1)], out_hbm.at[pl.ds(tok, 1)], ssem.at[sl])

    def start_rows(n, copy):
        def pair(k, carry):
            copy(2 * k).start(priority=0)
            copy(2 * k + 1).start(priority=1)
            return carry
        if isinstance(n, int):
            assert n % 2 == 0
            lax.fori_loop(0, n // 2, pair, 0, unroll=4)
        else:
            lax.fori_loop(0, n // 2, pair, 0)

            @pl.when(n % 2 == 1)
            def _():
                copy(n - 1).start(priority=0)

    def wait_rows(n, copy):
        def body(r, carry):
            copy(r).wait()
            return carry
        lax.fori_loop(0, n, body, 0)

    def start_tile(tile, copy):
        cnt = cnt_ref[tile]

        @pl.when(cnt == tm)
        def _():
            start_rows(tm, copy)

        @pl.when(cnt < tm)
        def _():
            start_rows(cnt, copy)

    def wait_tile(tile, copy, whole):
        cnt = cnt_ref[tile]

        @pl.when(cnt == tm)
        def _():
            whole.wait()

        @pl.when(cnt < tm)
        def _():
            wait_rows(cnt, copy)

    def gather_start(tile, sl):
        start_tile(tile, lambda r: gather_copy(tile, sl, r))

    def gather_wait(tile, sl):
        wait_tile(tile, lambda r: gather_copy(tile, sl, r),
                  pltpu.make_async_copy(hx_hbm.at[pl.ds(0, tm * ROW_CHUNKS)], xbuf.at[sl], gsem.at[sl]))

    def scatter_start(tile, sl):
        start_tile(tile, lambda r: scatter_copy(tile, sl, r))

    def scatter_wait(tile, sl):
        wait_tile(tile, lambda r: scatter_copy(tile, sl, r),
                  pltpu.make_async_copy(ybuf.at[sl], out_hbm.at[pl.ds(0, tm)], ssem.at[sl]))

    @pl.when(i == 0)
    def _():
        xbuf[...] = jnp.zeros(xbuf.shape, f32)
        gather_start(0, 0)

    @pl.when(i + 1 < n_tiles)
    def _():
        gather_start(i + 1, 1 - slot)

    gather_wait(i, slot)

    @pl.when(i >= 2)
    def _():
        scatter_wait(i - 2, slot)

    @pl.when(cnt_ref[i] > 0)
    def _():
        xs = xbuf.at[slot]
        chunk = lambda c: xs[pl.ds(c, tm, stride=ROW_CHUNKS), :]
        h2 = jnp.concatenate([chunk(c) for c in range(D_MODEL // HEAD_W)], axis=1)
        router = chunk(D_MODEL // HEAD_W)
        xn = _rms(h2, fnw_ref[...]).astype(bf16)
        lane = lax.broadcasted_iota(jnp.int32, router.shape, 1)
        acc = h2
        for e in range(n_e):
            gate_lane = GATE_LANE0 + exp_ref[i * n_e + e]
            gate = jnp.sum(jnp.where(lane == gate_lane, router, 0.0), axis=-1, keepdims=True)
            a = _dot(xn, wg_refs[e][...])
            hid = a / (1.0 + jnp.exp(-a)) * _dot(xn, wu_refs[e][...])
            acc = acc + _dot((hid * gate).astype(bf16), wd_refs[e][...])
        if final_norm:
            acc = _rms(acc, onw_ref[...])
        ybuf[slot] = acc

    scatter_start(i, slot)

    @pl.when(i == n_tiles - 1)
    def _():
        @pl.when(i >= 1)
        def _():
            scatter_wait(i - 1, 1 - slot)
        scatter_wait(i, slot)


def _dispatch(cls_f32, n_cls, tm):
    n = cls_f32.shape[0]
    n_tiles = n // tm + n_cls
    grp = cls_f32.astype(jnp.int32)
    onehot = (grp[:, None] == jnp.arange(n_cls, dtype=jnp.int32)[None, :]).astype(jnp.int32)
    csum = jnp.cumsum(onehot, axis=0)
    cnt = csum[-1]
    rank = jnp.sum(csum * onehot, axis=1) - 1
    tiles_g = (cnt + tm - 1) // tm
    tile_end = jnp.cumsum(tiles_g)
    tile_start = tile_end - tiles_g
    pos = jnp.sum(onehot * (tile_start * tm)[None, :], axis=1) + rank
    tile = jnp.arange(n_tiles, dtype=jnp.int32)
    tile_grp_raw = jnp.sum((tile[:, None] >= tile_end[None, :]).astype(jnp.int32), axis=1)
    last_grp = jnp.max(jnp.where(cnt > 0, jnp.arange(n_cls, dtype=jnp.int32), 0))
    tile_grp = jnp.minimum(tile_grp_raw, last_grp)
    in_range = tile_grp_raw < n_cls
    rows_left = cnt[tile_grp] - (tile - tile_start[tile_grp]) * tm
    tile_cnt = jnp.where(in_range, jnp.clip(rows_left, 0, tm), 0).astype(jnp.int32)
    ids = jnp.zeros((n_tiles * tm,), jnp.int32).at[pos].set(jnp.arange(n, dtype=jnp.int32))
    return ids, tile_grp, tile_cnt


def _moe(hx, router, fnw, onw, wg_bf, wu_bf, wd_bf, layer, tm, final_norm, by_pair):
    n = router.shape[0]
    if by_pair:
        n_e = 2
        ids, tile_cls, tile_cnt = _dispatch(router[:, 1], N_GROUPS * len(PAIR_LO), tm)
        grp, pair = tile_cls // len(PAIR_LO), tile_cls % len(PAIR_LO)
        members = [jnp.array(PAIR_LO, jnp.int32)[pair], jnp.array(PAIR_HI, jnp.int32)[pair]]
    else:
        n_e = EXPERTS_PER_GROUP
        ids, grp, tile_cnt = _dispatch(router[:, 0], N_GROUPS, tm)
        members = [jnp.full_like(grp, e) for e in range(n_e)]
    tile_exp = jnp.stack([grp * EXPERTS_PER_GROUP + m for m in members], axis=1).reshape(-1).astype(jnp.int32)
    n_tiles = tile_cnt.shape[0]
    fixed = lambda i, ids_r, exp_r, cnt_r: (0, 0)

    def wspecs(shape):
        return [pl.BlockSpec((None, None) + shape, lambda i, ids_r, exp_r, cnt_r, e=e: (layer, exp_r[i * n_e + e], 0, 0))
                for e in range(n_e)]

    return pl.pallas_call(
        functools.partial(_moe_kernel, tm=tm, n_e=n_e, final_norm=final_norm),
        grid_spec=pltpu.PrefetchScalarGridSpec(
            num_scalar_prefetch=3,
            grid=(n_tiles,),
            in_specs=[pl.BlockSpec(memory_space=pl.ANY),
                      pl.BlockSpec((1, D_MODEL), fixed), pl.BlockSpec((1, D_MODEL), fixed)]
                     + wspecs((D_MODEL, EXPERT_FF)) + wspecs((D_MODEL, EXPERT_FF)) + wspecs((EXPERT_FF, D_MODEL)),
            out_specs=pl.BlockSpec(memory_space=pl.ANY),
            scratch_shapes=[pltpu.VMEM((2, tm * ROW_CHUNKS, HEAD_W), f32), pltpu.VMEM((2, tm, D_MODEL), f32),
                            pltpu.SemaphoreType.DMA((2,)), pltpu.SemaphoreType.DMA((2,))]),
        out_shape=jax.ShapeDtypeStruct((n, D_MODEL), f32),
        compiler_params=pltpu.CompilerParams(dimension_semantics=("arbitrary",), vmem_limit_bytes=VMEM_LIMIT),
        name="moe",
    )(ids, tile_exp, tile_cnt, hx, fnw, onw, *([wg_bf] * n_e), *([wu_bf] * n_e), *([wd_bf] * n_e))


def _stacked_sample_queries(qd):
    bd, t = qd.shape[:2]
    q5 = qd.reshape(bd, t, N_HEADS, 2, DA_DK)
    eye = jnp.eye(2, dtype=qd.dtype)
    qz = q5[:, :, :, :, None, :] * eye[None, None, None, :, :, None]
    return qz.transpose(0, 2, 3, 1, 4, 5).reshape(bd, N_HEADS * 2 * t, HEAD_W)


def kernel(x_prompt, x_sample, cache_k, cache_v, state_hgrn, page_table, attn_norm_w, w_in, hgrn_lb, hgrn_norm_w,
           diff_lambda, diff_norm_w, w_o, ffn_norm_w, w_r1, b_r1, w_r2, b_r2, w_gate, w_up, w_down, final_norm_w):
    b, s = x_prompt.shape[:2]
    bd, t = x_sample.shape[:2]
    depth = w_in.shape[0]
    n_pages = page_table.shape[1]
    past_len = n_pages * PAGE_SIZE
    assert x_prompt.shape[2] == D_MODEL and w_in.shape[2] == N_SEG * SEG_W and t <= SAMPLE_PAD
    tp = SAMPLE_PAD

    tabs_p = _rope_tables(jnp.arange(s, dtype=jnp.int32))
    tabs_s = _rope_tables(jnp.tile(past_len + jnp.arange(t, dtype=jnp.int32), bd))
    p_lb = jax.nn.softmax(hgrn_lb.astype(f32), axis=0)
    lb_all = jnp.cumsum(p_lb, axis=0) - p_lb[0:1]

    w_in_bf = w_in.astype(bf16)
    w_o_bf = w_o.astype(bf16)
    wg_bf, wu_bf, wd_bf = w_gate.astype(bf16), w_up.astype(bf16), w_down.astype(bf16)
    pad_r = ROUTER_W - N_GROUPS - N_EXPERTS
    w_r = jnp.concatenate([w_r1, w_r2, jnp.zeros((depth, D_MODEL, pad_r), f32)], axis=2)
    w_r_hi = w_r.astype(bf16)
    w_r_lo = (w_r - w_r_hi.astype(f32)).astype(bf16)
    b_r = jnp.concatenate([b_r1, b_r2, jnp.zeros((depth, pad_r), f32)], axis=1)

    tm_p = 256 if (b * s) % 256 == 0 else b * s
    tq = 512 if s % 512 == 0 else s
    tb_p = 512 if s % 512 == 0 else s
    tm_moe_p = 256 if (b * s) % 256 == 0 else b * s
    n_s = bd * t
    n_pg = next(p for p in (16, 8, 4, 2, 1) if n_pages % p == 0)
    tm_proj = 512 if s % 512 == 0 else tm_p
    q_scale = DA_DK ** -0.5 * LOG2E

    hp = x_prompt.reshape(b * s, D_MODEL)
    hs = x_sample.reshape(n_s, D_MODEL)
    zeros_state = jnp.zeros((b, N_HEADS, HEAD_W, HEAD_W), f32)
    kp_l, vp_l, sp_l, ks_l, vs_l, ss_l = [], [], [], [], [], []
    for l in range(depth):
        lam_init = 0.8 - 0.6 * math.exp(-0.3 * l)
        dl = diff_lambda[l].astype(f32)
        lam = (jnp.exp(jnp.sum(dl[0] * dl[1])) - jnp.exp(jnp.sum(dl[2] * dl[3])) + lam_init).reshape(1)
        lb = lb_all[l].reshape(1, SEG_W)
        nw = attn_norm_w[l].reshape(1, D_MODEL)
        fnw = ffn_norm_w[l].reshape(1, D_MODEL)
        onw = final_norm_w.reshape(1, D_MODEL)
        hnw = hgrn_norm_w[l].reshape(1, HEAD_W)
        dnw = diff_norm_w[l].reshape(1, HEAD_W)
        last = l == depth - 1

        def tail(o_h, o_d, g_h, h, tm_mix, tm_moe, by_pair):
            hx, router = _mix(o_h, o_d, g_h, h, hnw, dnw, w_o_bf[l], fnw, w_r_hi[l], w_r_lo[l],
                              b_r[l].reshape(1, ROUTER_W), 1.0 - lam_init, tm_mix)
            return _moe(hx, router, fnw, onw, wg_bf, wu_bf, wd_bf, l, tm_moe, last, by_pair)

        qh, kh, ih, lf, gh, qd, kd, vd, kb, vb = _proj(hp, b * s, nw, w_in_bf[l], lb, tabs_p, tm_proj, q_scale, seq=s)
        o_h, s_fin = _hgrn(qh, kh, ih, lf, zeros_state, s, tb_p, HG_CHUNK)
        o_d = _attn_prompt(lam, qd, kb, vb, b, s, tq)
        hp = tail(o_h, o_d, gh, hp, tm_p, tm_moe_p, True)
        kp_l.append(kd.reshape(b, s, N_HEADS, HEAD_W))
        vp_l.append(vd.reshape(b, s, N_HEADS, HEAD_W))
        sp_l.append(s_fin)

        qh, kh, ih, lf, gh, qd, kd, vd, kb, vb = _proj(hs, n_s, nw, w_in_bf[l], lb, tabs_s, n_s, q_scale)
        pad = lambda a: jnp.pad(a.reshape(bd, t, SEG_W), ((0, 0), (0, tp - t), (0, 0)))
        flat = lambda a: pad(a).reshape(bd * tp, SEG_W)
        o_h, s_fin = _hgrn(flat(qh), flat(kh), flat(ih), flat(lf), state_hgrn[l].astype(f32), tp, tp, tp)
        o_h = o_h.reshape(bd, tp, SEG_W)[:, :t].reshape(n_s, SEG_W)
        o_d = _attn_sample(page_table, lam, _stacked_sample_queries(pad(qd)), pad(kb), pad(vb),
                           cache_k, cache_v, l, n_pg)
        o_d = o_d[:, :t].reshape(n_s, SEG_W)
        hs = tail(o_h, o_d, gh, hs, n_s, n_s, False)
        ks_l.append(kd.reshape(bd, t, N_HEADS, HEAD_W))
        vs_l.append(vd.reshape(bd, t, N_HEADS, HEAD_W))
        ss_l.append(s_fin)

    return (hp.reshape(b, s, D_MODEL), hs.reshape(bd, t, D_MODEL), jnp.stack(kp_l), jnp.stack(vp_l),
            jnp.stack(sp_l), jnp.stack(ks_l), jnp.stack(vs_l), jnp.stack(ss_l))
```

```python
import functools
import math

import jax
import jax.numpy as jnp
from jax import lax
from jax.experimental import pallas as pl
from jax.experimental.pallas import tpu as pltpu

f32 = jnp.float32
bf16 = jnp.bfloat16

D_MODEL = 1024
N_HEADS = 4
HEAD_W = 128
SEG_W = N_HEADS * HEAD_W
N_SEG = 7
DA_DK = 64
ROT_DIM = DA_DK // 4
ROPE_THETA = 500000.0
PAGE_SIZE = 128
N_GROUPS = 4
EXPERTS_PER_GROUP = 4
N_EXPERTS = N_GROUPS * EXPERTS_PER_GROUP
PAIR_LO = (0, 0, 0, 1, 1, 2)
PAIR_HI = (1, 2, 3, 2, 3, 3)
EXPERT_FF = D_MODEL // 2
NORM_EPS = 1e-6
NEG = -1e30
F_MIN = 1e-30

HG_CHUNK = 64
LOG2E = math.log2(math.e)
SAMPLE_PAD = 8
GATE_LANE0 = N_GROUPS
ROUTER_W = 128
ROW_CHUNKS = (D_MODEL + ROUTER_W) // HEAD_W

VMEM_LIMIT = 48 * 1024 * 1024


def _dot(a, b):
    return jnp.dot(a, b, preferred_element_type=f32)


def _dot_nt(a, b):
    return lax.dot_general(a, b, (((1,), (1,)), ((), ())), preferred_element_type=f32)


def _dot_tn(a, b):
    return lax.dot_general(a, b, (((0,), (0,)), ((), ())), preferred_element_type=f32)


def _rms(x, w):
    return x * lax.rsqrt(jnp.mean(x * x, axis=-1, keepdims=True) + NORM_EPS) * w


def _split3(x):
    hi = x.astype(bf16)
    r1 = x - hi.astype(f32)
    mid = r1.astype(bf16)
    lo = (r1 - mid.astype(f32)).astype(bf16)
    return hi, mid, lo


def _proj_kernel(h_ref, nw_ref, w_ref, lb_ref, c_ref, sa_ref, sb_ref,
                 qh_ref, kh_ref, ih_ref, lf_ref, gh_ref, qd_ref, kd_ref, vd_ref, kb_ref, vb_ref, *, q_scale, transposed):
    tm = h_ref.shape[0]
    xn = _rms(h_ref[...], nw_ref[...]).astype(bf16)

    def seg(i):
        return _dot(xn, w_ref[:, i * SEG_W:(i + 1) * SEG_W])

    qh_ref[...] = seg(0) * (HEAD_W ** -0.5)
    hf = seg(1)
    lb = lb_ref[...]
    e = jnp.exp(-jnp.abs(hf))
    r = 1.0 / (1.0 + e)
    pos = hf >= 0.0
    sig = jnp.where(pos, r, e * r)
    nsig = jnp.where(pos, e * r, r)
    f = lb + (1.0 - lb) * sig
    lf_ref[...] = jnp.log(jnp.maximum(f, F_MIN)) * LOG2E
    kh_ref[...] = (1.0 - lb) * nsig
    ih_ref[...] = seg(2)
    gh_ref[...] = seg(3)

    c = c_ref[...]
    sa = sa_ref[...]
    sb = sb_ref[...]

    def rope(z, hh):
        zz = z[:, hh * HEAD_W:(hh + 1) * HEAD_W]
        return zz * c + pltpu.roll(zz, HEAD_W - ROT_DIM // 2, 1) * sa + pltpu.roll(zz, ROT_DIM // 2, 1) * sb

    zq = seg(4)
    zk = seg(5)
    for hh in range(N_HEADS):
        sl = slice(hh * HEAD_W, (hh + 1) * HEAD_W)
        qr = rope(zq, hh) * q_scale
        if transposed:
            qd_ref[sl, :] = qr.T.astype(bf16)
        else:
            qd_ref[:, sl] = qr.astype(bf16)
        kr = rope(zk, hh)
        kd_ref[pl.ds(hh, tm, stride=N_HEADS), :] = kr
        kb_ref[:, sl] = kr.astype(bf16)
    vd = seg(6)
    for hh in range(N_HEADS):
        vd_ref[pl.ds(hh, tm, stride=N_HEADS), :] = vd[:, hh * HEAD_W:(hh + 1) * HEAD_W]
    vb_ref[...] = vd.T.astype(bf16) if transposed else vd.astype(bf16)


def _proj(h, n, nw, w_bf, lb, tabs, tm, q_scale, seq=None):
    npos = tabs[0].shape[0] // tm
    row = lambda i: (i, 0)
    fixed = lambda i: (0, 0)
    tab = lambda i: (i % npos, 0)
    seg_f32 = jax.ShapeDtypeStruct((n, SEG_W), f32)
    seg_bf = jax.ShapeDtypeStruct((n, SEG_W), bf16)
    seg_spec = pl.BlockSpec((tm, SEG_W), row)
    rows_shape = jax.ShapeDtypeStruct((n * N_HEADS, HEAD_W), f32)
    rows_spec = pl.BlockSpec((tm * N_HEADS, HEAD_W), row)
    if seq is None:
        t_shape, t_spec = seg_bf, seg_spec
    else:
        nb = seq // tm
        t_shape = jax.ShapeDtypeStruct((n // seq, SEG_W, seq), bf16)
        t_spec = pl.BlockSpec((None, SEG_W, tm), lambda i: (i // nb, 0, i % nb))
    return pl.pallas_call(
        functools.partial(_proj_kernel, q_scale=q_scale, transposed=seq is not None),
        grid=(n // tm,),
        in_specs=[pl.BlockSpec((tm, D_MODEL), row),
                  pl.BlockSpec((1, D_MODEL), fixed),
                  pl.BlockSpec((D_MODEL, N_SEG * SEG_W), fixed),
                  pl.BlockSpec((1, SEG_W), fixed),
                  pl.BlockSpec((tm, HEAD_W), tab),
                  pl.BlockSpec((tm, HEAD_W), tab),
                  pl.BlockSpec((tm, HEAD_W), tab)],
        out_specs=[seg_spec] * 5 + [t_spec, rows_spec, rows_spec, seg_spec, t_spec],
        out_shape=[seg_f32] * 5 + [t_shape, rows_shape, rows_shape, seg_bf, t_shape],
        compiler_params=pltpu.CompilerParams(dimension_semantics=("arbitrary",), vmem_limit_bytes=VMEM_LIMIT),
        name="proj",
    )(h, nw, w_bf, lb, *tabs)


def _rope_tables(pos):
    half = ROT_DIM // 2
    inv = ROPE_THETA ** (-jnp.arange(0, ROT_DIM, 2, dtype=f32) / ROT_DIM)
    ang = pos.astype(f32)[:, None] * inv[None, :]
    cos, sin = jnp.cos(ang), jnp.sin(ang)
    t = pos.shape[0]
    rest = DA_DK - ROT_DIM
    c64 = jnp.concatenate([cos, cos, jnp.ones((t, rest), f32)], axis=1)
    sa64 = jnp.concatenate([-sin, jnp.zeros((t, half + rest), f32)], axis=1)
    sb64 = jnp.concatenate([jnp.zeros((t, half), f32), sin, jnp.zeros((t, rest), f32)], axis=1)
    return tuple(jnp.tile(a, (1, HEAD_W // DA_DK)) for a in (c64, sa64, sb64))


def _hgrn_chunk(q, k, v, lf, st, chunk, consts):
    sel, pair_masks, halves = consts
    heads = [slice(hh * HEAD_W, (hh + 1) * HEAD_W) for hh in range(N_HEADS)]
    terms = jnp.concatenate(_split3(lf), axis=1)
    gs3 = _dot(sel, terms)
    gs = gs3[:, :SEG_W] + gs3[:, SEG_W:2 * SEG_W] + gs3[:, 2 * SEG_W:]
    g = gs[:chunk]
    g_last = g[chunk - 1:chunk, :]
    qx = (q * jnp.exp2(g)).astype(bf16)
    q_bf, k_bf, v_bf = q.astype(bf16), k.astype(bf16), v.astype(bf16)
    st_bf = [s.astype(bf16) for s in st]
    o = [_dot_nt(qx[:, sl], st_bf[hh]) for hh, sl in enumerate(heads)]
    a = [_dot_nt(q_bf[:, sl], k_bf[:, sl]) * pair_masks[0] for sl in heads]
    n_mm = 1
    for lvl, h in enumerate(halves, start=1):
        if h % 8 == 0:
            ref = jnp.concatenate([jnp.broadcast_to(g[r0 + h - 1:r0 + h, :], (2 * h, SEG_W))
                                   for r0 in range(0, chunk, 2 * h)], axis=0)
        else:
            ref = gs[n_mm * chunk:(n_mm + 1) * chunk]
            n_mm += 1
        e = jnp.exp2(-jnp.abs(g - ref))
        qe, ke = (q * e).astype(bf16), (k * e).astype(bf16)
        a = [a[hh] + _dot_nt(qe[:, sl], ke[:, sl]) * pair_masks[lvl] for hh, sl in enumerate(heads)]
    kdec = (k * jnp.exp2(g_last - g)).astype(bf16)
    decay = jnp.exp2(g_last)
    o = jnp.concatenate([o[hh] + _dot(a[hh].astype(bf16), v_bf[:, sl]) for hh, sl in enumerate(heads)], axis=1)
    st_new = [st[hh] * decay[:, sl] + _dot_tn(v_bf[:, sl], kdec[:, sl]) for hh, sl in enumerate(heads)]
    return o, st_new


def _hgrn_consts(chunk):
    r = lax.broadcasted_iota(jnp.int32, (chunk, chunk), 0)
    c = lax.broadcasted_iota(jnp.int32, (chunk, chunk), 1)
    sels = [jnp.where(c <= r, 1.0, 0.0)]
    masks = [jnp.where(c == r, 1.0, 0.0)]
    halves = []
    h = chunk // 2
    while h >= 1:
        blk = -(2 * h)
        if h % 8:
            sels.append(jnp.where(c <= (r & blk) + (h - 1), 1.0, 0.0))
        same_block = (r & blk) == (c & blk)
        masks.append(jnp.where(same_block & ((r & h) != 0) & ((c & h) == 0), 1.0, 0.0))
        halves.append(h)
        h //= 2
    return jnp.concatenate(sels, axis=0).astype(bf16), masks, halves


def _hgrn_kernel(q_ref, k_ref, v_ref, lf_ref, s0_ref, o_ref, sfin_ref, st_ref, *, chunk, n_chunks):
    j = pl.program_id(1)
    consts = _hgrn_consts(chunk)

    @pl.when(j == 0)
    def _():
        for hh in range(N_HEADS):
            st_ref[hh] = s0_ref[hh].T

    def body(c, carry):
        rows = pl.ds(pl.multiple_of(c * chunk, chunk), chunk)
        o, st_new = _hgrn_chunk(q_ref[rows, :], k_ref[rows, :], v_ref[rows, :], lf_ref[rows, :],
                                [st_ref[hh] for hh in range(N_HEADS)], chunk, consts)
        o_ref[rows, :] = o
        for hh in range(N_HEADS):
            st_ref[hh] = st_new[hh]
        return carry

    lax.fori_loop(0, n_chunks, body, 0, unroll=2 if n_chunks % 2 == 0 else 1)

    @pl.when(j == pl.num_programs(1) - 1)
    def _():
        for hh in range(N_HEADS):
            sfin_ref[hh] = st_ref[hh].T


def _hgrn(q, k, v, lf, s0, t, tb, chunk):
    n = q.shape[0]
    b = n // t
    nj = t // tb
    tok = pl.BlockSpec((tb, SEG_W), lambda bi, j: (bi * nj + j, 0))
    st = pl.BlockSpec((None, N_HEADS, HEAD_W, HEAD_W), lambda bi, j: (bi, 0, 0, 0))
    return pl.pallas_call(
        functools.partial(_hgrn_kernel, chunk=chunk,n_chunks=tb // chunk),
        grid=(b, nj),
        in_specs=[tok, tok, tok, tok, st],
        out_specs=[tok, st],
        out_shape=[jax.ShapeDtypeStruct((n, SEG_W), f32), jax.ShapeDtypeStruct(s0.shape, f32)],
        scratch_shapes=[pltpu.VMEM((N_HEADS, HEAD_W, HEAD_W), f32)],
        compiler_params=pltpu.CompilerParams(dimension_semantics=("arbitrary", "arbitrary"),
                                             vmem_limit_bytes=VMEM_LIMIT),
        name="hgrn",
    )(q, k, v, lf, s0)


def _stack_maps(q):
    lane = lax.broadcasted_iota(jnp.int32, q.shape, 1)
    zero = jnp.zeros_like(q)
    return jnp.concatenate([jnp.where(lane < DA_DK, q, zero), jnp.where(lane >= DA_DK, q, zero)], axis=0)


def _softmax_step(s, v_bf, m_ref, l_ref, acc_ref, rows=None):
    sl = slice(None) if rows is None else rows
    m_prev = m_ref[sl, :]
    m_new = jnp.maximum(m_prev, jnp.max(s, axis=-1, keepdims=True))
    alpha = jnp.exp2(m_prev - m_new)
    p = jnp.exp2(s - m_new)
    l_ref[sl, :] = alpha * l_ref[sl, :] + jnp.sum(p, axis=-1, keepdims=True)
    acc_ref[sl, :] = alpha * acc_ref[sl, :] + _dot(p.astype(bf16), v_bf)
    m_ref[sl, :] = m_new


def _attn_prompt_step(i, lam_ref, qt_ref, k_ref, vt_ref, o_ref, m_ref, l_ref, acc_ref, s_ref, tq, n_q):
    qt = qt_ref[...]
    sub = lax.broadcasted_iota(jnp.int32, qt.shape, 0)
    zero = jnp.zeros_like(qt)
    qs = jnp.concatenate([jnp.where(sub < DA_DK, qt, zero), jnp.where(sub >= DA_DK, qt, zero)], axis=1)
    m_ref[...] = jnp.full(m_ref.shape, NEG, f32)
    l_ref[...] = jnp.zeros(l_ref.shape, f32)
    acc_ref[...] = jnp.zeros(acc_ref.shape, f32)

    def scores(j):
        return _dot(k_ref[j * tq:(j + 1) * tq, :], qs)

    def update(s, vt):
        m_prev = m_ref[...]
        m_new = jnp.maximum(m_prev, jnp.max(s, axis=0, keepdims=True))
        alpha = jnp.exp2(m_prev - m_new)
        p = jnp.exp2(s - m_new)
        l_ref[...] = alpha * l_ref[...] + jnp.sum(p, axis=0, keepdims=True)
        acc_ref[...] = alpha * acc_ref[...] + _dot(vt, p.astype(bf16))
        m_ref[...] = m_new

    s_ref[0] = scores(0)
    for j in range(n_q - 1):
        @pl.when(j < i)
        def _():
            s_ref[(j + 1) % 2] = scores(j + 1)
            update(s_ref[j % 2], vt_ref[:, j * tq:(j + 1) * tq])

    s = s_ref[i % 2]
    key = lax.broadcasted_iota(jnp.int32, s.shape, 0)
    qry = lax.broadcasted_iota(jnp.int32, s.shape, 1) & (tq - 1)
    update(jnp.where(key <= qry, s, NEG), vt_ref[:, pl.ds(pl.multiple_of(i * tq, tq), tq)])
    o = acc_ref[...] / l_ref[...]
    o_ref[...] = (o[:, :tq] - lam_ref[0] * o[:, tq:]).T


def _attn_sample_step(j, n_j, lam_ref, qs_ref, kn_ref, vn_ref, k_refs, v_refs, o_ref, m_ref, l_ref, acc_ref):
    t = SAMPLE_PAD

    @pl.when(j == 0)
    def _():
        m_ref[...] = jnp.full(m_ref.shape, NEG, f32)
        l_ref[...] = jnp.zeros(l_ref.shape, f32)
        acc_ref[...] = jnp.zeros(acc_ref.shape, f32)

    head_rows = [slice(hh * 2 * t, (hh + 1) * 2 * t) for hh in range(N_HEADS)]
    head_toks = [pl.ds(hh, PAGE_SIZE, stride=N_HEADS) for hh in range(N_HEADS)]
    qs = qs_ref[...]
    s = [_dot_nt(qs[rows, :], jnp.concatenate([kr[toks, :].astype(bf16) for kr in k_refs], axis=0))
         for rows, toks in zip(head_rows, head_toks)]
    m_prev = m_ref[...]
    m_new = jnp.maximum(m_prev, jnp.concatenate([jnp.max(sh, axis=-1, keepdims=True) for sh in s], axis=0))
    alpha = jnp.exp2(m_prev - m_new)
    p = [jnp.exp2(sh - m_new[rows, :]) for sh, rows in zip(s, head_rows)]
    l_ref[...] = alpha * l_ref[...] + jnp.concatenate([jnp.sum(ph, axis=-1, keepdims=True) for ph in p], axis=0)
    pv = [_dot(ph.astype(bf16), jnp.concatenate([vr[toks, :].astype(bf16) for vr in v_refs], axis=0))
          for ph, toks in zip(p, head_toks)]
    acc_ref[...] = alpha * acc_ref[...] + jnp.concatenate(pv, axis=0)
    m_ref[...] = m_new

    @pl.when(j == n_j - 1)
    def _():
        for hh in range(N_HEADS):
            rows = slice(hh * 2 * t, (hh + 1) * 2 * t)
            sl = slice(hh * HEAD_W, (hh + 1) * HEAD_W)
            s = _dot_nt(qs_ref[rows, :], kn_ref[:, sl].astype(bf16))
            row = lax.broadcasted_iota(jnp.int32, s.shape, 0) & (t - 1)
            col = lax.broadcasted_iota(jnp.int32, s.shape, 1)
            s = jnp.where(col <= row, s, NEG)
            _softmax_step(s, vn_ref[:, sl].astype(bf16), m_ref, l_ref, acc_ref, rows)
            o = acc_ref[rows, :] / l_ref[rows, :]
            o_ref[:, sl] = o[:t] - lam_ref[0] * o[t:]


def _attn_kernel(pt_ref, lam_ref, qt_ref, k_ref, vt_ref, qs_ref, kn_ref, vn_ref, *rest,
                 tq, n_q, n_pg, n_j, steps_p, steps_s):
    del pt_ref
    k_pages, v_pages = rest[:n_pg], rest[n_pg:2 * n_pg]
    op_ref, os_ref, pm_ref, pl_ref, pacc_ref, ps_ref, sm_ref, sl_ref, sacc_ref = rest[2 * n_pg:]
    step = pl.program_id(0)

    def prompt():
        _attn_prompt_step(step % n_q, lam_ref, qt_ref, k_ref, vt_ref, op_ref, pm_ref, pl_ref, pacc_ref, ps_ref,
                          tq, n_q)

    def sample():
        _attn_sample_step(step % n_j, n_j, lam_ref, qs_ref, kn_ref, vn_ref, k_pages, v_pages, os_ref,
                          sm_ref, sl_ref, sacc_ref)

    if steps_p == steps_s:
        prompt()
        sample()
    else:
        pl.when(step < steps_p)(prompt)
        pl.when(step < steps_s)(sample)


def _attn(page_table, lam, qt_bf, k_bf, vt_bf, qs, kn, vn, cache_k, cache_v, layer, b, s, tq, n_pg):
    bd, n_pages = page_table.shape
    t = SAMPLE_PAD
    n_q, n_j = s // tq, n_pages // n_pg
    steps_p, steps_s = b * N_HEADS * n_q, bd * n_j
    k3 = k_bf.reshape(b, s, SEG_W)
    cache_k = cache_k.reshape(cache_k.shape[:2] + (PAGE_SIZE * N_HEADS, HEAD_W))
    cache_v = cache_v.reshape(cache_v.shape[:2] + (PAGE_SIZE * N_HEADS, HEAD_W))

    def p_idx(step):
        lin = jnp.minimum(step, steps_p - 1)
        return lin // (N_HEADS * n_q), (lin // n_q) % N_HEADS, lin % n_q

    def s_idx(step):
        lin = jnp.minimum(step, steps_s - 1)
        return lin // n_j, lin % n_j

    def on_p(f):
        return lambda step, pt: f(*p_idx(step))

    def per_seq(shape):
        return pl.BlockSpec((None,) + shape, lambda step, pt: (s_idx(step)[0], 0, 0))

    def page_spec(p):
        def index(step, pt):
            sb, j = s_idx(step)
            return layer, pt[sb * n_pages + j * n_pg + p], 0, 0
        return pl.BlockSpec((None, None, PAGE_SIZE * N_HEADS, HEAD_W), index)

    pages = [page_spec(p) for p in range(n_pg)]
    out_p, out_s = pl.pallas_call(
        functools.partial(_attn_kernel, tq=tq, n_q=n_q, n_pg=n_pg, n_j=n_j, steps_p=steps_p, steps_s=steps_s),
        grid_spec=pltpu.PrefetchScalarGridSpec(
            num_scalar_prefetch=1,
            grid=(max(steps_p, steps_s),),
            in_specs=[pl.BlockSpec(memory_space=pltpu.SMEM),
                      pl.BlockSpec((None, HEAD_W, tq), on_p(lambda bi, h, i: (bi, h, i))),
                      pl.BlockSpec((None, s, HEAD_W), on_p(lambda bi, h, i: (bi, 0, h))),
                      pl.BlockSpec((None, HEAD_W, s), on_p(lambda bi, h, i: (bi, h, 0))),
                      per_seq((2 * t * N_HEADS, HEAD_W)), per_seq((t, SEG_W)), per_seq((t, SEG_W))] + pages + pages,
            out_specs=[pl.BlockSpec((None, tq, HEAD_W), on_p(lambda bi, h, i: (bi, i, h))), per_seq((t, SEG_W))],
            scratch_shapes=[pltpu.VMEM((1, 2 * tq), f32), pltpu.VMEM((1, 2 * tq), f32),
                            pltpu.VMEM((HEAD_W, 2 * tq), f32), pltpu.VMEM((2, tq, 2 * tq), f32),
                            pltpu.VMEM((2 * t * N_HEADS, 1), f32), pltpu.VMEM((2 * t * N_HEADS, 1), f32),
                            pltpu.VMEM((2 * t * N_HEADS, HEAD_W), f32)]),
        out_shape=[jax.ShapeDtypeStruct((b, s, SEG_W), f32), jax.ShapeDtypeStruct((bd, t, SEG_W), f32)],
        compiler_params=pltpu.CompilerParams(dimension_semantics=("arbitrary",), vmem_limit_bytes=VMEM_LIMIT),
        name="attn",
    )(page_table.reshape(-1), lam, qt_bf, k3, vt_bf, qs, kn, vn, *([cache_k] * n_pg), *([cache_v] * n_pg))
    return out_p.reshape(b * s, SEG_W), out_s


def _route(logits):
    lane = lax.broadcasted_iota(jnp.int32, logits.shape, 1).astype(f32)
    big = float(ROUTER_W)
    is_g = lane < N_GROUPS
    m1 = jnp.max(jnp.where(is_g, logits, -jnp.inf), axis=-1, keepdims=True)
    grp = jnp.min(jnp.where(is_g & (logits == m1), lane, big), axis=-1, keepdims=True)
    p_grp = 1.0 / jnp.sum(jnp.where(is_g, jnp.exp(logits - m1), 0.0), axis=-1, keepdims=True)
    lo = GATE_LANE0 + EXPERTS_PER_GROUP * grp
    in_g = (lane >= lo) & (lane < lo + EXPERTS_PER_GROUP)
    v1 = jnp.max(jnp.where(in_g, logits, -jnp.inf), axis=-1, keepdims=True)
    i1 = jnp.min(jnp.where(in_g & (logits == v1), lane, big), axis=-1, keepdims=True)
    rest = in_g & (lane != i1)
    v2 = jnp.max(jnp.where(rest, logits, -jnp.inf), axis=-1, keepdims=True)
    i2 = jnp.min(jnp.where(rest & (logits == v2), lane, big), axis=-1, keepdims=True)
    e = jnp.exp(v2 - v1)
    w1 = 1.0 / (1.0 + e)
    w2 = e / (1.0 + e)
    gates = jnp.where(lane == i1, p_grp * w1, 0.0) + jnp.where(lane == i2, p_grp * w2, 0.0)
    a = jnp.minimum(i1, i2) - lo
    b = jnp.maximum(i1, i2) - lo
    pair = jnp.where(a == 0.0, b - 1.0, jnp.where(a == 1.0, b + 1.0, 5.0))
    return jnp.where(lane == 0.0, grp, jnp.where(lane == 1.0, grp * float(len(PAIR_LO)) + pair, gates))


def _mix_kernel(oh_ref, od_ref, gh_ref, h_ref, hnw_ref, dnw_ref, wo_ref, fnw_ref, wrh_ref, wrl_ref, br_ref,
                hx_ref, rt_ref, *, od_scale):
    parts = []
    for hh in range(N_HEADS):
        sl = slice(hh * HEAD_W, (hh + 1) * HEAD_W)
        gate = 1.0 / (1.0 + jnp.exp(-gh_ref[:, sl]))
        parts.append((_rms(oh_ref[:, sl], hnw_ref[...]) * gate).astype(bf16))
    for hh in range(N_HEADS):
        sl = slice(hh * HEAD_W, (hh + 1) * HEAD_W)
        parts.append((_rms(od_ref[:, sl], dnw_ref[...]) * od_scale).astype(bf16))
    h2 = h_ref[...] + _dot(jnp.concatenate(parts, axis=1), wo_ref[...])
    xn = _rms(h2, fnw_ref[...])
    x_hi = xn.astype(bf16)
    x_lo = (xn - x_hi.astype(f32)).astype(bf16)
    logits = _dot(x_hi, wrh_ref[...]) + _dot(x_lo, wrh_ref[...]) + _dot(x_hi, wrl_ref[...]) + br_ref[...]
    tm = h2.shape[0]
    for c in range(D_MODEL // HEAD_W):
        hx_ref[pl.ds(c, tm, stride=ROW_CHUNKS), :] = h2[:, c * HEAD_W:(c + 1) * HEAD_W]
    router = _route(logits)
    hx_ref[pl.ds(D_MODEL // HEAD_W, tm, stride=ROW_CHUNKS), :] = router
    rt_ref[...] = router


def _mix(o_h, o_d, g_h, h, hnw, dnw, wo_bf, fnw, wr_hi, wr_lo, br, od_scale, tm):
    n = o_h.shape[0]
    row = lambda i: (i, 0)
    fixed = lambda i: (0, 0)
    seg = pl.BlockSpec((tm, SEG_W), row)
    return pl.pallas_call(
        functools.partial(_mix_kernel, od_scale=od_scale),
        grid=(n // tm,),
        in_specs=[seg, seg, seg, pl.BlockSpec((tm, D_MODEL), row),
                  pl.BlockSpec((1, HEAD_W), fixed), pl.BlockSpec((1, HEAD_W), fixed),
                  pl.BlockSpec((D_MODEL, D_MODEL), fixed), pl.BlockSpec((1, D_MODEL), fixed),
                  pl.BlockSpec((D_MODEL, ROUTER_W), fixed), pl.BlockSpec((D_MODEL, ROUTER_W), fixed),
                  pl.BlockSpec((1, ROUTER_W), fixed)],
        out_specs=[pl.BlockSpec((tm * ROW_CHUNKS, HEAD_W), row), pl.BlockSpec((tm, ROUTER_W), row)],
        out_shape=[jax.ShapeDtypeStruct((n * ROW_CHUNKS, HEAD_W), f32), jax.ShapeDtypeStruct((n, ROUTER_W), f32)],
        compiler_params=pltpu.CompilerParams(dimension_semantics=("arbitrary",), vmem_limit_bytes=VMEM_LIMIT),
        name="mix",
    )(o_h, o_d, g_h, h, hnw, dnw, wo_bf, fnw, wr_hi, wr_lo, br)


def _moe_kernel(ids_ref, exp_ref, cnt_ref, hx_hbm, fnw_ref, onw_ref, *rest, tm, n_e, final_norm):
    wg_refs, wu_refs, wd_refs = rest[:n_e], rest[n_e:2 * n_e], rest[2 * n_e:3 * n_e]
    out_hbm, xbuf, ybuf, gsem, ssem = rest[3 * n_e:]
    i = pl.program_id(0)
    n_tiles = pl.num_programs(0)
    slot = i % 2

    def gather_copy(tile, sl, r):
        tok = ids_ref[tile * tm + r]
        return pltpu.make_async_copy(hx_hbm.at[pl.ds(tok * ROW_CHUNKS, ROW_CHUNKS)],
                                     xbuf.at[sl, pl.ds(r * ROW_CHUNKS, ROW_CHUNKS)], gsem.at[sl])

    def scatter_copy(tile, sl, r):
        tok = ids_ref[tile * tm + r]
        return pltpu.make_async_copy(ybuf.at[sl, pl.ds(r, 1)], out_hbm.at[pl.ds(tok, 1)], ssem.at[sl])

    def start_rows(n, copy):
        def pair(k, carry):
            copy(2 * k).start(priority=0)
            copy(2 * k + 1).start(priority=1)
            return carry
        if isinstance(n, int):
            assert n % 2 == 0
            lax.fori_loop(0, n // 2, pair, 0, unroll=4)
        else:
            lax.fori_loop(0, n // 2, pair, 0)

            @pl.when(n % 2 == 1)
            def _():
                copy(n - 1).start(priority=0)

    def wait_rows(n, copy):
        def body(r, carry):
            copy(r).wait()
            return carry
        lax.fori_loop(0, n, body, 0)

    def start_tile(tile, copy):
        cnt = cnt_ref[tile]

        @pl.when(cnt == tm)
        def _():
            start_rows(tm, copy)

        @pl.when(cnt < tm)
        def _():
            start_rows(cnt, copy)

    def wait_tile(tile, copy, whole):
        cnt = cnt_ref[tile]

        @pl.when(cnt == tm)
        def _():
            whole.wait()

        @pl.when(cnt < tm)
        def _():
            wait_rows(cnt, copy)

    def gather_start(tile, sl):
        start_tile(tile, lambda r: gather_copy(tile, sl, r))

    def gather_wait(tile, sl):
        wait_tile(tile, lambda r: gather_copy(tile, sl, r),
                  pltpu.make_async_copy(hx_hbm.at[pl.ds(0, tm * ROW_CHUNKS)], xbuf.at[sl], gsem.at[sl]))

    def scatter_start(tile, sl):
        start_tile(tile, lambda r: scatter_copy(tile, sl, r))

    def scatter_wait(tile, sl):
        wait_tile(tile, lambda r: scatter_copy(tile, sl, r),
                  pltpu.make_async_copy(ybuf.at[sl], out_hbm.at[pl.ds(0, tm)], ssem.at[sl]))

    @pl.when(i == 0)
    def _():
        xbuf[...] = jnp.zeros(xbuf.shape, f32)
        gather_start(0, 0)

    @pl.when(i + 1 < n_tiles)
    def _():
        gather_start(i + 1, 1 - slot)

    gather_wait(i, slot)

    @pl.when(i >= 2)
    def _():
        scatter_wait(i - 2, slot)

    @pl.when(cnt_ref[i] > 0)
    def _():
        xs = xbuf.at[slot]
        chunk = lambda c: xs[pl.ds(c, tm, stride=ROW_CHUNKS), :]
        h2 = jnp.concatenate([chunk(c) for c in range(D_MODEL // HEAD_W)], axis=1)
        router = chunk(D_MODEL // HEAD_W)
        xn = _rms(h2, fnw_ref[...]).astype(bf16)
        lane = lax.broadcasted_iota(jnp.int32, router.shape, 1)
        acc = h2
        for e in range(n_e):
            gate_lane = GATE_LANE0 + exp_ref[i * n_e + e]
            gate = jnp.sum(jnp.where(lane == gate_lane, router, 0.0), axis=-1, keepdims=True)
            a = _dot(xn, wg_refs[e][...])
            hid = a / (1.0 + jnp.exp(-a)) * _dot(xn, wu_refs[e][...])
            acc = acc + _dot((hid * gate).astype(bf16), wd_refs[e][...])
        if final_norm:
            acc = _rms(acc, onw_ref[...])
        ybuf[slot] = acc

    scatter_start(i, slot)

    @pl.when(i == n_tiles - 1)
    def _():
        @pl.when(i >= 1)
        def _():
            scatter_wait(i - 1, 1 - slot)
        scatter_wait(i, slot)


def _dispatch(cls_f32, n_cls, tm):
    n = cls_f32.shape[0]
    n_tiles = n // tm + n_cls
    grp = cls_f32.astype(jnp.int32)
    onehot = (grp[:, None] == jnp.arange(n_cls, dtype=jnp.int32)[None, :]).astype(jnp.int32)
    csum = jnp.cumsum(onehot, axis=0)
    cnt = csum[-1]
    rank = jnp.sum(csum * onehot, axis=1) - 1
    tiles_g = (cnt + tm - 1) // tm
    tile_end = jnp.cumsum(tiles_g)
    tile_start = tile_end - tiles_g
    pos = jnp.sum(onehot * (tile_start * tm)[None, :], axis=1) + rank
    tile = jnp.arange(n_tiles, dtype=jnp.int32)
    tile_grp_raw = jnp.sum((tile[:, None] >= tile_end[None, :]).astype(jnp.int32), axis=1)
    last_grp = jnp.max(jnp.where(cnt > 0, jnp.arange(n_cls, dtype=jnp.int32), 0))
    tile_grp = jnp.minimum(tile_grp_raw, last_grp)
    in_range = tile_grp_raw < n_cls
    rows_left = cnt[tile_grp] - (tile - tile_start[tile_grp]) * tm
    tile_cnt = jnp.where(in_range, jnp.clip(rows_left, 0, tm), 0).astype(jnp.int32)
    ids = jnp.zeros((n_tiles * tm,), jnp.int32).at[pos].set(jnp.arange(n, dtype=jnp.int32))
    return ids, tile_grp, tile_cnt


def _moe(hx, router, fnw, onw, wg_bf, wu_bf, wd_bf, layer, tm, final_norm, by_pair):
    n = router.shape[0]
    if by_pair:
        n_e = 2
        ids, tile_cls, tile_cnt = _dispatch(router[:, 1], N_GROUPS * len(PAIR_LO), tm)
        grp, pair = tile_cls // len(PAIR_LO), tile_cls % len(PAIR_LO)
        members = [jnp.array(PAIR_LO, jnp.int32)[pair], jnp.array(PAIR_HI, jnp.int32)[pair]]
    else:
        n_e = EXPERTS_PER_GROUP
        ids, grp, tile_cnt = _dispatch(router[:, 0], N_GROUPS, tm)
        members = [jnp.full_like(grp, e) for e in range(n_e)]
    tile_exp = jnp.stack([grp * EXPERTS_PER_GROUP + m for m in members], axis=1).reshape(-1).astype(jnp.int32)
    n_tiles = tile_cnt.shape[0]
    fixed = lambda i, ids_r, exp_r, cnt_r: (0, 0)

    def wspecs(shape):
        return [pl.BlockSpec((None, None) + shape, lambda i, ids_r, exp_r, cnt_r, e=e: (layer, exp_r[i * n_e + e], 0, 0))
                for e in range(n_e)]

    return pl.pallas_call(
        functools.partial(_moe_kernel, tm=tm, n_e=n_e, final_norm=final_norm),
        grid_spec=pltpu.PrefetchScalarGridSpec(
            num_scalar_prefetch=3,
            grid=(n_tiles,),
            in_specs=[pl.BlockSpec(memory_space=pl.ANY),
                      pl.BlockSpec((1, D_MODEL), fixed), pl.BlockSpec((1, D_MODEL), fixed)]
                     + wspecs((D_MODEL, EXPERT_FF)) + wspecs((D_MODEL, EXPERT_FF)) + wspecs((EXPERT_FF, D_MODEL)),
            out_specs=pl.BlockSpec(memory_space=pl.ANY),
            scratch_shapes=[pltpu.VMEM((2, tm * ROW_CHUNKS, HEAD_W), f32), pltpu.VMEM((2, tm, D_MODEL), f32),
                            pltpu.SemaphoreType.DMA((2,)), pltpu.SemaphoreType.DMA((2,))]),
        out_shape=jax.ShapeDtypeStruct((n, D_MODEL), f32),
        compiler_params=pltpu.CompilerParams(dimension_semantics=("arbitrary",), vmem_limit_bytes=VMEM_LIMIT),
        name="moe",
    )(ids, tile_exp, tile_cnt, hx, fnw, onw, *([wg_bf] * n_e), *([wu_bf] * n_e), *([wd_bf] * n_e))


def _stacked_sample_queries(qd):
    bd, t = qd.shape[:2]
    q5 = qd.reshape(bd, t, N_HEADS, 2, DA_DK)
    eye = jnp.eye(2, dtype=qd.dtype)
    qz = q5[:, :, :, :, None, :] * eye[None, None, None, :, :, None]
    return qz.transpose(0, 2, 3, 1, 4, 5).reshape(bd, N_HEADS * 2 * t, HEAD_W)


def kernel(x_prompt, x_sample, cache_k, cache_v, state_hgrn, page_table, attn_norm_w, w_in, hgrn_lb, hgrn_norm_w,
           diff_lambda, diff_norm_w, w_o, ffn_norm_w, w_r1, b_r1, w_r2, b_r2, w_gate, w_up, w_down, final_norm_w):
    b, s = x_prompt.shape[:2]
    bd, t = x_sample.shape[:2]
    depth = w_in.shape[0]
    n_pages = page_table.shape[1]
    past_len = n_pages * PAGE_SIZE
    assert x_prompt.shape[2] == D_MODEL and w_in.shape[2] == N_SEG * SEG_W and t <= SAMPLE_PAD
    tp = SAMPLE_PAD

    tabs_p = _rope_tables(jnp.arange(s, dtype=jnp.int32))
    tabs_s = _rope_tables(jnp.tile(past_len + jnp.arange(t, dtype=jnp.int32), bd))
    p_lb = jax.nn.softmax(hgrn_lb.astype(f32), axis=0)
    lb_all = jnp.cumsum(p_lb, axis=0) - p_lb[0:1]

    w_in_bf = w_in.astype(bf16)
    w_o_bf = w_o.astype(bf16)
    wg_bf, wu_bf, wd_bf = w_gate.astype(bf16), w_up.astype(bf16), w_down.astype(bf16)
    pad_r = ROUTER_W - N_GROUPS - N_EXPERTS
    w_r = jnp.concatenate([w_r1, w_r2, jnp.zeros((depth, D_MODEL, pad_r), f32)], axis=2)
    w_r_hi = w_r.astype(bf16)
    w_r_lo = (w_r - w_r_hi.astype(f32)).astype(bf16)
    b_r = jnp.concatenate([b_r1, b_r2, jnp.zeros((depth, pad_r), f32)], axis=1)

    tm_p = 256 if (b * s) % 256 == 0 else b * s
    tq = 512 if s % 512 == 0 else s
    tb_p = 512 if s % 512 == 0 else s
    tm_moe_p = 256 if (b * s) % 256 == 0 else b * s
    n_s = bd * t
    n_pg = next(p for p in (16, 8, 4, 2, 1) if n_pages % p == 0)
    tm_proj = 512 if s % 512 == 0 else tm_p
    q_scale = DA_DK ** -0.5 * LOG2E

    hp = x_prompt.reshape(b * s, D_MODEL)
    hs = x_sample.reshape(n_s, D_MODEL)
    zeros_state = jnp.zeros((b, N_HEADS, HEAD_W, HEAD_W), f32)
    kp_l, vp_l, sp_l, ks_l, vs_l, ss_l = [], [], [], [], [], []
    for l in range(depth):
        lam_init = 0.8 - 0.6 * math.exp(-0.3 * l)
        dl = diff_lambda[l].astype(f32)
        lam = (jnp.exp(jnp.sum(dl[0] * dl[1])) - jnp.exp(jnp.sum(dl[2] * dl[3])) + lam_init).reshape(1)
        lb = lb_all[l].reshape(1, SEG_W)
        nw = attn_norm_w[l].reshape(1, D_MODEL)
        fnw = ffn_norm_w[l].reshape(1, D_MODEL)
        onw = final_norm_w.reshape(1, D_MODEL)
        hnw = hgrn_norm_w[l].reshape(1, HEAD_W)
        dnw = diff_norm_w[l].reshape(1, HEAD_W)
        last = l == depth - 1

        def tail(o_h, o_d, g_h, h, tm_mix, tm_moe, by_pair):
            hx, router = _mix(o_h, o_d, g_h, h, hnw, dnw, w_o_bf[l], fnw, w_r_hi[l], w_r_lo[l],
                              b_r[l].reshape(1, ROUTER_W), 1.0 - lam_init, tm_mix)
            return _moe(hx, router, fnw, onw, wg_bf, wu_bf, wd_bf, l, tm_moe, last, by_pair)

        qh, kh, ih, lf, gh_p, qd_p, kd, vd, kb_p, vb_p = _proj(hp, b * s, nw, w_in_bf[l], lb, tabs_p, tm_proj,
                                                               q_scale, seq=s)
        oh_p, s_fin = _hgrn(qh, kh, ih, lf, zeros_state, s, tb_p, HG_CHUNK)
        kp_l.append(kd.reshape(b, s, N_HEADS, HEAD_W))
        vp_l.append(vd.reshape(b, s, N_HEADS, HEAD_W))
        sp_l.append(s_fin)

        qh, kh, ih, lf, gh_s, qd, kd, vd, kb, vb = _proj(hs, n_s, nw, w_in_bf[l], lb, tabs_s, n_s, q_scale)
        pad = lambda a: jnp.pad(a.reshape(bd, t, SEG_W), ((0, 0), (0, tp - t), (0, 0)))
        flat = lambda a: pad(a).reshape(bd * tp, SEG_W)
        oh_s, s_fin = _hgrn(flat(qh), flat(kh), flat(ih), flat(lf), state_hgrn[l].astype(f32), tp, tp, tp)
        oh_s = oh_s.reshape(bd, tp, SEG_W)[:, :t].reshape(n_s, SEG_W)
        ks_l.append(kd.reshape(bd, t, N_HEADS, HEAD_W))
        vs_l.append(vd.reshape(bd, t, N_HEADS, HEAD_W))
        ss_l.append(s_fin)

        od_p, od_s = _attn(page_table, lam, qd_p, kb_p, vb_p, _stacked_sample_queries(pad(qd)), pad(kb), pad(vb),
                           cache_k, cache_v, l, b, s, tq, n_pg)
        hp = tail(oh_p, od_p, gh_p, hp, tm_p, tm_moe_p, True)
        hs = tail(oh_s, od_s[:, :t].reshape(n_s, SEG_W), gh_s, hs, n_s, n_s, False)

    return (hp.reshape(b, s, D_MODEL), hs.reshape(bd, t, D_MODEL), jnp.stack(kp_l), jnp.stack(vp_l),
            jnp.stack(sp_l), jnp.stack(ks_l), jnp.stack(vs_l), jnp.stack(ss_l))
```

```python
import functools
import math

import jax
import jax.numpy as jnp
from jax import lax
from jax.experimental import pallas as pl
from jax.experimental.pallas import tpu as pltpu

f32 = jnp.float32
bf16 = jnp.bfloat16

D_MODEL = 1024
N_HEADS = 4
HEAD_W = 128
SEG_W = N_HEADS * HEAD_W
N_SEG = 7
DA_DK = 64
ROT_DIM = DA_DK // 4
ROPE_THETA = 500000.0
PAGE_SIZE = 128
N_GROUPS = 4
EXPERTS_PER_GROUP = 4
N_EXPERTS = N_GROUPS * EXPERTS_PER_GROUP
PAIR_LO = (0, 0, 0, 1, 1, 2)
PAIR_HI = (1, 2, 3, 2, 3, 3)
EXPERT_FF = D_MODEL // 2
NORM_EPS = 1e-6
NEG = -1e30
F_MIN = 1e-30

HG_CHUNK = 64
LOG2E = math.log2(math.e)
SAMPLE_PAD = 8
GATE_LANE0 = N_GROUPS
ROUTER_W = 128
ROW_CHUNKS = (D_MODEL + ROUTER_W) // HEAD_W

VMEM_LIMIT = 48 * 1024 * 1024


def _dot(a, b):
    return jnp.dot(a, b, preferred_element_type=f32)


def _dot_nt(a, b):
    return lax.dot_general(a, b, (((1,), (1,)), ((), ())), preferred_element_type=f32)


def _dot_tn(a, b):
    return lax.dot_general(a, b, (((0,), (0,)), ((), ())), preferred_element_type=f32)


def _rms(x, w):
    return x * lax.rsqrt(jnp.mean(x * x, axis=-1, keepdims=True) + NORM_EPS) * w


def _split3(x):
    hi = x.astype(bf16)
    r1 = x - hi.astype(f32)
    mid = r1.astype(bf16)
    lo = (r1 - mid.astype(f32)).astype(bf16)
    return hi, mid, lo


def _proj_kernel(h_ref, nw_ref, w_ref, lb_ref, c_ref, sa_ref, sb_ref,
                 qh_ref, kh_ref, ih_ref, lf_ref, gh_ref, qd_ref, kd_ref, vd_ref, kb_ref, vb_ref, *, q_scale, transposed):
    tm = h_ref.shape[0]
    xn = _rms(h_ref[...], nw_ref[...]).astype(bf16)

    def seg(i):
        return _dot(xn, w_ref[:, i * SEG_W:(i + 1) * SEG_W])

    qh_ref[...] = seg(0) * (HEAD_W ** -0.5)
    hf = seg(1)
    lb = lb_ref[...]
    e = jnp.exp(-jnp.abs(hf))
    r = 1.0 / (1.0 + e)
    pos = hf >= 0.0
    sig = jnp.where(pos, r, e * r)
    nsig = jnp.where(pos, e * r, r)
    f = lb + (1.0 - lb) * sig
    lf_ref[...] = jnp.log(jnp.maximum(f, F_MIN)) * LOG2E
    kh_ref[...] = (1.0 - lb) * nsig
    ih_ref[...] = seg(2)
    gh_ref[...] = seg(3)

    c = c_ref[...]
    sa = sa_ref[...]
    sb = sb_ref[...]

    def rope(z, hh):
        zz = z[:, hh * HEAD_W:(hh + 1) * HEAD_W]
        return zz * c + pltpu.roll(zz, HEAD_W - ROT_DIM // 2, 1) * sa + pltpu.roll(zz, ROT_DIM // 2, 1) * sb

    zq = seg(4)
    zk = seg(5)
    for hh in range(N_HEADS):
        sl = slice(hh * HEAD_W, (hh + 1) * HEAD_W)
        qr = rope(zq, hh) * q_scale
        if transposed:
            qd_ref[sl, :] = qr.T.astype(bf16)
        else:
            qd_ref[:, sl] = qr.astype(bf16)
        kr = rope(zk, hh)
        kd_ref[pl.ds(hh, tm, stride=N_HEADS), :] = kr
        kb_ref[:, sl] = kr.astype(bf16)
    vd = seg(6)
    for hh in range(N_HEADS):
        vd_ref[pl.ds(hh, tm, stride=N_HEADS), :] = vd[:, hh * HEAD_W:(hh + 1) * HEAD_W]
    vb_ref[...] = vd.T.astype(bf16) if transposed else vd.astype(bf16)


def _proj(h, n, nw, w_bf, lb, tabs, tm, q_scale, seq=None):
    npos = tabs[0].shape[0] // tm
    row = lambda i: (i, 0)
    fixed = lambda i: (0, 0)
    tab = lambda i: (i % npos, 0)
    seg_f32 = jax.ShapeDtypeStruct((n, SEG_W), f32)
    seg_bf = jax.ShapeDtypeStruct((n, SEG_W), bf16)
    seg_spec = pl.BlockSpec((tm, SEG_W), row)
    rows_shape = jax.ShapeDtypeStruct((n * N_HEADS, HEAD_W), f32)
    rows_spec = pl.BlockSpec((tm * N_HEADS, HEAD_W), row)
    if seq is None:
        t_shape, t_spec = seg_bf, seg_spec
    else:
        nb = seq // tm
        t_shape = jax.ShapeDtypeStruct((n // seq, SEG_W, seq), bf16)
        t_spec = pl.BlockSpec((None, SEG_W, tm), lambda i: (i // nb, 0, i % nb))
    return pl.pallas_call(
        functools.partial(_proj_kernel, q_scale=q_scale, transposed=seq is not None),
        grid=(n // tm,),
        in_specs=[pl.BlockSpec((tm, D_MODEL), row),
                  pl.BlockSpec((1, D_MODEL), fixed),
                  pl.BlockSpec((D_MODEL, N_SEG * SEG_W), fixed),
                  pl.BlockSpec((1, SEG_W), fixed),
                  pl.BlockSpec((tm, HEAD_W), tab),
                  pl.BlockSpec((tm, HEAD_W), tab),
                  pl.BlockSpec((tm, HEAD_W), tab)],
        out_specs=[seg_spec] * 5 + [t_spec, rows_spec, rows_spec, seg_spec, t_spec],
        out_shape=[seg_f32] * 5 + [t_shape, rows_shape, rows_shape, seg_bf, t_shape],
        compiler_params=pltpu.CompilerParams(dimension_semantics=("arbitrary",), vmem_limit_bytes=VMEM_LIMIT),
        name="proj",
    )(h, nw, w_bf, lb, *tabs)


def _rope_tables(pos):
    half = ROT_DIM // 2
    inv = ROPE_THETA ** (-jnp.arange(0, ROT_DIM, 2, dtype=f32) / ROT_DIM)
    ang = pos.astype(f32)[:, None] * inv[None, :]
    cos, sin = jnp.cos(ang), jnp.sin(ang)
    t = pos.shape[0]
    rest = DA_DK - ROT_DIM
    c64 = jnp.concatenate([cos, cos, jnp.ones((t, rest), f32)], axis=1)
    sa64 = jnp.concatenate([-sin, jnp.zeros((t, half + rest), f32)], axis=1)
    sb64 = jnp.concatenate([jnp.zeros((t, half), f32), sin, jnp.zeros((t, rest), f32)], axis=1)
    return tuple(jnp.tile(a, (1, HEAD_W // DA_DK)) for a in (c64, sa64, sb64))


def _hgrn_chunk(q, k, v, lf, st, chunk, consts):
    sel, pair_masks, halves = consts
    heads = [slice(hh * HEAD_W, (hh + 1) * HEAD_W) for hh in range(N_HEADS)]
    terms = jnp.concatenate(_split3(lf), axis=1)
    gs3 = _dot(sel, terms)
    gs = gs3[:, :SEG_W] + gs3[:, SEG_W:2 * SEG_W] + gs3[:, 2 * SEG_W:]
    g = gs[:chunk]
    g_last = g[chunk - 1:chunk, :]
    qx = (q * jnp.exp2(g)).astype(bf16)
    q_bf, k_bf, v_bf = q.astype(bf16), k.astype(bf16), v.astype(bf16)
    st_bf = [s.astype(bf16) for s in st]
    o = [_dot_nt(qx[:, sl], st_bf[hh]) for hh, sl in enumerate(heads)]
    a = [_dot_nt(q_bf[:, sl], k_bf[:, sl]) * pair_masks[0] for sl in heads]
    n_mm = 1
    for lvl, h in enumerate(halves, start=1):
        if h % 8 == 0:
            ref = jnp.concatenate([jnp.broadcast_to(g[r0 + h - 1:r0 + h, :], (2 * h, SEG_W))
                                   for r0 in range(0, chunk, 2 * h)], axis=0)
        else:
            ref = gs[n_mm * chunk:(n_mm + 1) * chunk]
            n_mm += 1
        e = jnp.exp2(-jnp.abs(g - ref))
        qe, ke = (q * e).astype(bf16), (k * e).astype(bf16)
        a = [a[hh] + _dot_nt(qe[:, sl], ke[:, sl]) * pair_masks[lvl] for hh, sl in enumerate(heads)]
    kdec = (k * jnp.exp2(g_last - g)).astype(bf16)
    decay = jnp.exp2(g_last)
    o = jnp.concatenate([o[hh] + _dot(a[hh].astype(bf16), v_bf[:, sl]) for hh, sl in enumerate(heads)], axis=1)
    st_new = [st[hh] * decay[:, sl] + _dot_tn(v_bf[:, sl], kdec[:, sl]) for hh, sl in enumerate(heads)]
    return o, st_new


def _hgrn_consts(chunk):
    r = lax.broadcasted_iota(jnp.int32, (chunk, chunk), 0)
    c = lax.broadcasted_iota(jnp.int32, (chunk, chunk), 1)
    sels = [jnp.where(c <= r, 1.0, 0.0)]
    masks = [jnp.where(c == r, 1.0, 0.0)]
    halves = []
    h = chunk // 2
    while h >= 1:
        blk = -(2 * h)
        if h % 8:
            sels.append(jnp.where(c <= (r & blk) + (h - 1), 1.0, 0.0))
        same_block = (r & blk) == (c & blk)
        masks.append(jnp.where(same_block & ((r & h) != 0) & ((c & h) == 0), 1.0, 0.0))
        halves.append(h)
        h //= 2
    return jnp.concatenate(sels, axis=0).astype(bf16), masks, halves


def _hgrn_kernel(q_ref, k_ref, v_ref, lf_ref, s0_ref, o_ref, sfin_ref, st_ref, *, chunk, n_chunks):
    j = pl.program_id(1)
    consts = _hgrn_consts(chunk)

    @pl.when(j == 0)
    def _():
        for hh in range(N_HEADS):
            st_ref[hh] = s0_ref[hh].T

    def body(c, carry):
        rows = pl.ds(pl.multiple_of(c * chunk, chunk), chunk)
        o, st_new = _hgrn_chunk(q_ref[rows, :], k_ref[rows, :], v_ref[rows, :], lf_ref[rows, :],
                                [st_ref[hh] for hh in range(N_HEADS)], chunk, consts)
        o_ref[rows, :] = o
        for hh in range(N_HEADS):
            st_ref[hh] = st_new[hh]
        return carry

    lax.fori_loop(0, n_chunks, body, 0, unroll=4 if n_chunks % 4 == 0 else 1)

    @pl.when(j == pl.num_programs(1) - 1)
    def _():
        for hh in range(N_HEADS):
            sfin_ref[hh] = st_ref[hh].T


def _hgrn(q, k, v, lf, s0, t, tb, chunk):
    n = q.shape[0]
    b = n // t
    nj = t // tb
    tok = pl.BlockSpec((tb, SEG_W), lambda bi, j: (bi * nj + j, 0))
    st = pl.BlockSpec((None, N_HEADS, HEAD_W, HEAD_W), lambda bi, j: (bi, 0, 0, 0))
    return pl.pallas_call(
        functools.partial(_hgrn_kernel, chunk=chunk,n_chunks=tb // chunk),
        grid=(b, nj),
        in_specs=[tok, tok, tok, tok, st],
        out_specs=[tok, st],
        out_shape=[jax.ShapeDtypeStruct((n, SEG_W), f32), jax.ShapeDtypeStruct(s0.shape, f32)],
        scratch_shapes=[pltpu.VMEM((N_HEADS, HEAD_W, HEAD_W), f32)],
        compiler_params=pltpu.CompilerParams(dimension_semantics=("arbitrary", "arbitrary"),
                                             vmem_limit_bytes=VMEM_LIMIT),
        name="hgrn",
    )(q, k, v, lf, s0)


def _stack_maps(q):
    lane = lax.broadcasted_iota(jnp.int32, q.shape, 1)
    zero = jnp.zeros_like(q)
    return jnp.concatenate([jnp.where(lane < DA_DK, q, zero), jnp.where(lane >= DA_DK, q, zero)], axis=0)


def _softmax_step(s, v_bf, m_ref, l_ref, acc_ref, rows=None):
    sl = slice(None) if rows is None else rows
    m_prev = m_ref[sl, :]
    m_new = jnp.maximum(m_prev, jnp.max(s, axis=-1, keepdims=True))
    alpha = jnp.exp2(m_prev - m_new)
    p = jnp.exp2(s - m_new)
    l_ref[sl, :] = alpha * l_ref[sl, :] + jnp.sum(p, axis=-1, keepdims=True)
    acc_ref[sl, :] = alpha * acc_ref[sl, :] + _dot(p.astype(bf16), v_bf)
    m_ref[sl, :] = m_new


def _attn_prompt_step(i, lam_ref, qt_ref, k_ref, vt_ref, o_ref, m_ref, l_ref, acc_ref, s_ref, tq, n_q):
    qt = qt_ref[...]
    sub = lax.broadcasted_iota(jnp.int32, qt.shape, 0)
    zero = jnp.zeros_like(qt)
    qs = jnp.concatenate([jnp.where(sub < DA_DK, qt, zero), jnp.where(sub >= DA_DK, qt, zero)], axis=1)
    m_ref[...] = jnp.full(m_ref.shape, NEG, f32)
    l_ref[...] = jnp.zeros(l_ref.shape, f32)
    acc_ref[...] = jnp.zeros(acc_ref.shape, f32)

    def scores(j):
        return _dot(k_ref[j * tq:(j + 1) * tq, :], qs)

    def update(s, vt):
        m_prev = m_ref[...]
        m_new = jnp.maximum(m_prev, jnp.max(s, axis=0, keepdims=True))
        alpha = jnp.exp2(m_prev - m_new)
        p = jnp.exp2(s - m_new)
        l_ref[...] = alpha * l_ref[...] + jnp.sum(p, axis=0, keepdims=True)
        acc_ref[...] = alpha * acc_ref[...] + _dot(vt, p.astype(bf16))
        m_ref[...] = m_new

    s_ref[0] = scores(0)
    for j in range(n_q - 1):
        @pl.when(j < i)
        def _():
            s_ref[(j + 1) % 2] = scores(j + 1)
            update(s_ref[j % 2], vt_ref[:, j * tq:(j + 1) * tq])

    s = s_ref[i % 2]
    key = lax.broadcasted_iota(jnp.int32, s.shape, 0)
    qry = lax.broadcasted_iota(jnp.int32, s.shape, 1) & (tq - 1)
    update(jnp.where(key <= qry, s, NEG), vt_ref[:, pl.ds(pl.multiple_of(i * tq, tq), tq)])
    o = acc_ref[...] / l_ref[...]
    o_ref[...] = (o[:, :tq] - lam_ref[0] * o[:, tq:]).T


def _attn_sample_step(j, n_j, lam_ref, qs_ref, kn_ref, vn_ref, k_refs, v_refs, o_ref, m_ref, l_ref, acc_ref):
    t = SAMPLE_PAD

    @pl.when(j == 0)
    def _():
        m_ref[...] = jnp.full(m_ref.shape, NEG, f32)
        l_ref[...] = jnp.zeros(l_ref.shape, f32)
        acc_ref[...] = jnp.zeros(acc_ref.shape, f32)

    head_rows = [slice(hh * 2 * t, (hh + 1) * 2 * t) for hh in range(N_HEADS)]
    head_toks = [pl.ds(hh, PAGE_SIZE, stride=N_HEADS) for hh in range(N_HEADS)]
    qs = qs_ref[...]
    s = [_dot_nt(qs[rows, :], jnp.concatenate([kr[toks, :].astype(bf16) for kr in k_refs], axis=0))
         for rows, toks in zip(head_rows, head_toks)]
    m_prev = m_ref[...]
    m_new = jnp.maximum(m_prev, jnp.concatenate([jnp.max(sh, axis=-1, keepdims=True) for sh in s], axis=0))
    alpha = jnp.exp2(m_prev - m_new)
    p = [jnp.exp2(sh - m_new[rows, :]) for sh, rows in zip(s, head_rows)]
    l_ref[...] = alpha * l_ref[...] + jnp.concatenate([jnp.sum(ph, axis=-1, keepdims=True) for ph in p], axis=0)
    pv = [_dot(ph.astype(bf16), jnp.concatenate([vr[toks, :].astype(bf16) for vr in v_refs], axis=0))
          for ph, toks in zip(p, head_toks)]
    acc_ref[...] = alpha * acc_ref[...] + jnp.concatenate(pv, axis=0)
    m_ref[...] = m_new

    @pl.when(j == n_j - 1)
    def _():
        for hh in range(N_HEADS):
            rows = slice(hh * 2 * t, (hh + 1) * 2 * t)
            sl = slice(hh * HEAD_W, (hh + 1) * HEAD_W)
            s = _dot_nt(qs_ref[rows, :], kn_ref[:, sl].astype(bf16))
            row = lax.broadcasted_iota(jnp.int32, s.shape, 0) & (t - 1)
            col = lax.broadcasted_iota(jnp.int32, s.shape, 1)
            s = jnp.where(col <= row, s, NEG)
            _softmax_step(s, vn_ref[:, sl].astype(bf16), m_ref, l_ref, acc_ref, rows)
            o = acc_ref[rows, :] / l_ref[rows, :]
            o_ref[:, sl] = o[:t] - lam_ref[0] * o[t:]


def _attn_kernel(pt_ref, lam_ref, qt_ref, k_ref, vt_ref, qs_ref, kn_ref, vn_ref, *rest,
                 tq, n_q, n_pg, n_j, steps_p, steps_s):
    del pt_ref
    k_pages, v_pages = rest[:n_pg], rest[n_pg:2 * n_pg]
    op_ref, os_ref, pm_ref, pl_ref, pacc_ref, ps_ref, sm_ref, sl_ref, sacc_ref = rest[2 * n_pg:]
    step = pl.program_id(0)

    def prompt():
        _attn_prompt_step(step % n_q, lam_ref, qt_ref, k_ref, vt_ref, op_ref, pm_ref, pl_ref, pacc_ref, ps_ref,
                          tq, n_q)

    def sample():
        _attn_sample_step(step % n_j, n_j, lam_ref, qs_ref, kn_ref, vn_ref, k_pages, v_pages, os_ref,
                          sm_ref, sl_ref, sacc_ref)

    if steps_p == steps_s:
        prompt()
        sample()
    else:
        pl.when(step < steps_p)(prompt)
        pl.when(step < steps_s)(sample)


def _attn(page_table, lam, qt_bf, k_bf, vt_bf, qs, kn, vn, cache_k, cache_v, layer, b, s, tq, n_pg):
    bd, n_pages = page_table.shape
    t = SAMPLE_PAD
    n_q, n_j = s // tq, n_pages // n_pg
    steps_p, steps_s = b * N_HEADS * n_q, bd * n_j
    k3 = k_bf.reshape(b, s, SEG_W)
    cache_k = cache_k.reshape(cache_k.shape[:2] + (PAGE_SIZE * N_HEADS, HEAD_W))
    cache_v = cache_v.reshape(cache_v.shape[:2] + (PAGE_SIZE * N_HEADS, HEAD_W))

    def p_idx(step):
        lin = jnp.minimum(step, steps_p - 1)
        return lin // (N_HEADS * n_q), (lin // n_q) % N_HEADS, lin % n_q

    def s_idx(step):
        lin = jnp.minimum(step, steps_s - 1)
        return lin // n_j, lin % n_j

    def on_p(f):
        return lambda step, pt: f(*p_idx(step))

    def per_seq(shape):
        return pl.BlockSpec((None,) + shape, lambda step, pt: (s_idx(step)[0], 0, 0))

    def page_spec(p):
        def index(step, pt):
            sb, j = s_idx(step)
            return layer, pt[sb * n_pages + j * n_pg + p], 0, 0
        return pl.BlockSpec((None, None, PAGE_SIZE * N_HEADS, HEAD_W), index)

    pages = [page_spec(p) for p in range(n_pg)]
    out_p, out_s = pl.pallas_call(
        functools.partial(_attn_kernel, tq=tq, n_q=n_q, n_pg=n_pg, n_j=n_j, steps_p=steps_p, steps_s=steps_s),
        grid_spec=pltpu.PrefetchScalarGridSpec(
            num_scalar_prefetch=1,
            grid=(max(steps_p, steps_s),),
            in_specs=[pl.BlockSpec(memory_space=pltpu.SMEM),
                      pl.BlockSpec((None, HEAD_W, tq), on_p(lambda bi, h, i: (bi, h, i))),
                      pl.BlockSpec((None, s, HEAD_W), on_p(lambda bi, h, i: (bi, 0, h))),
                      pl.BlockSpec((None, HEAD_W, s), on_p(lambda bi, h, i: (bi, h, 0))),
                      per_seq((2 * t * N_HEADS, HEAD_W)), per_seq((t, SEG_W)), per_seq((t, SEG_W))] + pages + pages,
            out_specs=[pl.BlockSpec((None, tq, HEAD_W), on_p(lambda bi, h, i: (bi, i, h))), per_seq((t, SEG_W))],
            scratch_shapes=[pltpu.VMEM((1, 2 * tq), f32), pltpu.VMEM((1, 2 * tq), f32),
                            pltpu.VMEM((HEAD_W, 2 * tq), f32), pltpu.VMEM((2, tq, 2 * tq), f32),
                            pltpu.VMEM((2 * t * N_HEADS, 1), f32), pltpu.VMEM((2 * t * N_HEADS, 1), f32),
                            pltpu.VMEM((2 * t * N_HEADS, HEAD_W), f32)]),
        out_shape=[jax.ShapeDtypeStruct((b, s, SEG_W), f32), jax.ShapeDtypeStruct((bd, t, SEG_W), f32)],
        compiler_params=pltpu.CompilerParams(dimension_semantics=("arbitrary",), vmem_limit_bytes=VMEM_LIMIT),
        name="attn",
    )(page_table.reshape(-1), lam, qt_bf, k3, vt_bf, qs, kn, vn, *([cache_k] * n_pg), *([cache_v] * n_pg))
    return out_p.reshape(b * s, SEG_W), out_s


def _route(logits):
    lane = lax.broadcasted_iota(jnp.int32, logits.shape, 1).astype(f32)
    big = float(ROUTER_W)
    is_g = lane < N_GROUPS
    m1 = jnp.max(jnp.where(is_g, logits, -jnp.inf), axis=-1, keepdims=True)
    grp = jnp.min(jnp.where(is_g & (logits == m1), lane, big), axis=-1, keepdims=True)
    p_grp = 1.0 / jnp.sum(jnp.where(is_g, jnp.exp(logits - m1), 0.0), axis=-1, keepdims=True)
    lo = GATE_LANE0 + EXPERTS_PER_GROUP * grp
    in_g = (lane >= lo) & (lane < lo + EXPERTS_PER_GROUP)
    v1 = jnp.max(jnp.where(in_g, logits, -jnp.inf), axis=-1, keepdims=True)
    i1 = jnp.min(jnp.where(in_g & (logits == v1), lane, big), axis=-1, keepdims=True)
    rest = in_g & (lane != i1)
    v2 = jnp.max(jnp.where(rest, logits, -jnp.inf), axis=-1, keepdims=True)
    i2 = jnp.min(jnp.where(rest & (logits == v2), lane, big), axis=-1, keepdims=True)
    e = jnp.exp(v2 - v1)
    w1 = 1.0 / (1.0 + e)
    w2 = e / (1.0 + e)
    gates = jnp.where(lane == i1, p_grp * w1, 0.0) + jnp.where(lane == i2, p_grp * w2, 0.0)
    a = jnp.minimum(i1, i2) - lo
    b = jnp.maximum(i1, i2) - lo
    pair = jnp.where(a == 0.0, b - 1.0, jnp.where(a == 1.0, b + 1.0, 5.0))
    return jnp.where(lane == 0.0, grp, jnp.where(lane == 1.0, grp * float(len(PAIR_LO)) + pair, gates))


def _mix_kernel(oh_ref, od_ref, gh_ref, h_ref, hnw_ref, dnw_ref, wo_ref, fnw_ref, wrh_ref, wrl_ref, br_ref,
                hx_ref, rt_ref, *, od_scale):
    parts = []
    for hh in range(N_HEADS):
        sl = slice(hh * HEAD_W, (hh + 1) * HEAD_W)
        gate = 1.0 / (1.0 + jnp.exp(-gh_ref[:, sl]))
        parts.append((_rms(oh_ref[:, sl], hnw_ref[...]) * gate).astype(bf16))
    for hh in range(N_HEADS):
        sl = slice(hh * HEAD_W, (hh + 1) * HEAD_W)
        parts.append((_rms(od_ref[:, sl], dnw_ref[...]) * od_scale).astype(bf16))
    h2 = h_ref[...] + _dot(jnp.concatenate(parts, axis=1), wo_ref[...])
    xn = _rms(h2, fnw_ref[...])
    x_hi = xn.astype(bf16)
    x_lo = (xn - x_hi.astype(f32)).astype(bf16)
    logits = _dot(x_hi, wrh_ref[...]) + _dot(x_lo, wrh_ref[...]) + _dot(x_hi, wrl_ref[...]) + br_ref[...]
    tm = h2.shape[0]
    for c in range(D_MODEL // HEAD_W):
        hx_ref[pl.ds(c, tm, stride=ROW_CHUNKS), :] = h2[:, c * HEAD_W:(c + 1) * HEAD_W]
    router = _route(logits)
    hx_ref[pl.ds(D_MODEL // HEAD_W, tm, stride=ROW_CHUNKS), :] = router
    rt_ref[...] = router


def _mix(o_h, o_d, g_h, h, hnw, dnw, wo_bf, fnw, wr_hi, wr_lo, br, od_scale, tm):
    n = o_h.shape[0]
    row = lambda i: (i, 0)
    fixed = lambda i: (0, 0)
    seg = pl.BlockSpec((tm, SEG_W), row)
    return pl.pallas_call(
        functools.partial(_mix_kernel, od_scale=od_scale),
        grid=(n // tm,),
        in_specs=[seg, seg, seg, pl.BlockSpec((tm, D_MODEL), row),
                  pl.BlockSpec((1, HEAD_W), fixed), pl.BlockSpec((1, HEAD_W), fixed),
                  pl.BlockSpec((D_MODEL, D_MODEL), fixed), pl.BlockSpec((1, D_MODEL), fixed),
                  pl.BlockSpec((D_MODEL, ROUTER_W), fixed), pl.BlockSpec((D_MODEL, ROUTER_W), fixed),
                  pl.BlockSpec((1, ROUTER_W), fixed)],
        out_specs=[pl.BlockSpec((tm * ROW_CHUNKS, HEAD_W), row), pl.BlockSpec((tm, ROUTER_W), row)],
        out_shape=[jax.ShapeDtypeStruct((n * ROW_CHUNKS, HEAD_W), f32), jax.ShapeDtypeStruct((n, ROUTER_W), f32)],
        compiler_params=pltpu.CompilerParams(dimension_semantics=("arbitrary",), vmem_limit_bytes=VMEM_LIMIT),
        name="mix",
    )(o_h, o_d, g_h, h, hnw, dnw, wo_bf, fnw, wr_hi, wr_lo, br)


def _moe_kernel(pos_ref, exp_ref, cnt_ref, hx_hbm, fnw_ref, onw_ref, *rest, tm, n_e, final_norm):
    wg_refs, wu_refs, wd_refs = rest[:n_e], rest[n_e:2 * n_e], rest[2 * n_e:3 * n_e]
    out_hbm, xbuf, ybuf, ids_ref, gsem, ssem = rest[3 * n_e:]
    i = pl.program_id(0)
    n_tiles = pl.num_programs(0)
    slot = i % 2

    def gather_copy(tile, sl, r):
        tok = ids_ref[tile * tm + r]
        return pltpu.make_async_copy(hx_hbm.at[pl.ds(tok * ROW_CHUNKS, ROW_CHUNKS)],
                                     xbuf.at[sl, pl.ds(r * ROW_CHUNKS, ROW_CHUNKS)], gsem.at[sl])

    def scatter_copy(tile, sl, r):
        tok = ids_ref[tile * tm + r]
        return pltpu.make_async_copy(ybuf.at[sl, pl.ds(r, 1)], out_hbm.at[pl.ds(tok, 1)], ssem.at[sl])

    @pl.when(i == 0)
    def _():
        def place(t, carry):
            ids_ref[pos_ref[t]] = t
            return carry
        lax.fori_loop(0, pos_ref.shape[0], place, 0, unroll=8)

    def start_rows(n, copy):
        def pair(k, carry):
            copy(2 * k).start(priority=0)
            copy(2 * k + 1).start(priority=1)
            return carry
        if isinstance(n, int):
            assert n % 2 == 0
            lax.fori_loop(0, n // 2, pair, 0, unroll=4)
        else:
            lax.fori_loop(0, n // 2, pair, 0)

            @pl.when(n % 2 == 1)
            def _():
                copy(n - 1).start(priority=0)

    def wait_rows(n, copy):
        def body(r, carry):
            copy(r).wait()
            return carry
        lax.fori_loop(0, n, body, 0)

    def start_tile(tile, copy):
        cnt = cnt_ref[tile]

        @pl.when(cnt == tm)
        def _():
            start_rows(tm, copy)

        @pl.when(cnt < tm)
        def _():
            start_rows(cnt, copy)

    def wait_tile(tile, copy, whole):
        cnt = cnt_ref[tile]

        @pl.when(cnt == tm)
        def _():
            whole.wait()

        @pl.when(cnt < tm)
        def _():
            wait_rows(cnt, copy)

    def gather_start(tile, sl):
        start_tile(tile, lambda r: gather_copy(tile, sl, r))

    def gather_wait(tile, sl):
        wait_tile(tile, lambda r: gather_copy(tile, sl, r),
                  pltpu.make_async_copy(hx_hbm.at[pl.ds(0, tm * ROW_CHUNKS)], xbuf.at[sl], gsem.at[sl]))

    def scatter_start(tile, sl):
        start_tile(tile, lambda r: scatter_copy(tile, sl, r))

    def scatter_wait(tile, sl):
        wait_tile(tile, lambda r: scatter_copy(tile, sl, r),
                  pltpu.make_async_copy(ybuf.at[sl], out_hbm.at[pl.ds(0, tm)], ssem.at[sl]))

    @pl.when(i == 0)
    def _():
        xbuf[...] = jnp.zeros(xbuf.shape, f32)
        gather_start(0, 0)

    @pl.when(i + 1 < n_tiles)
    def _():
        gather_start(i + 1, 1 - slot)

    gather_wait(i, slot)

    @pl.when(i >= 2)
    def _():
        scatter_wait(i - 2, slot)

    @pl.when(cnt_ref[i] > 0)
    def _():
        xs = xbuf.at[slot]
        chunk = lambda c: xs[pl.ds(c, tm, stride=ROW_CHUNKS), :]
        h2 = jnp.concatenate([chunk(c) for c in range(D_MODEL // HEAD_W)], axis=1)
        router = chunk(D_MODEL // HEAD_W)
        xn = _rms(h2, fnw_ref[...]).astype(bf16)
        lane = lax.broadcasted_iota(jnp.int32, router.shape, 1)
        acc = h2
        for e in range(n_e):
            gate_lane = GATE_LANE0 + exp_ref[i * n_e + e]
            gate = jnp.sum(jnp.where(lane == gate_lane, router, 0.0), axis=-1, keepdims=True)
            a = _dot(xn, wg_refs[e][...])
            hid = a / (1.0 + jnp.exp(-a)) * _dot(xn, wu_refs[e][...])
            acc = acc + _dot((hid * gate).astype(bf16), wd_refs[e][...])
        if final_norm:
            acc = _rms(acc, onw_ref[...])
        ybuf[slot] = acc

    scatter_start(i, slot)

    @pl.when(i == n_tiles - 1)
    def _():
        @pl.when(i >= 1)
        def _():
            scatter_wait(i - 1, 1 - slot)
        scatter_wait(i, slot)


def _dispatch(cls_f32, n_cls, tm):
    n = cls_f32.shape[0]
    n_tiles = n // tm + n_cls
    grp = cls_f32.astype(jnp.int32)
    onehot = (grp[:, None] == jnp.arange(n_cls, dtype=jnp.int32)[None, :]).astype(jnp.int32)
    csum = jnp.cumsum(onehot, axis=0)
    cnt = csum[-1]
    rank = jnp.sum(csum * onehot, axis=1) - 1
    tiles_g = (cnt + tm - 1) // tm
    tile_end = jnp.cumsum(tiles_g)
    tile_start = tile_end - tiles_g
    pos = jnp.sum(onehot * (tile_start * tm)[None, :], axis=1) + rank
    tile = jnp.arange(n_tiles, dtype=jnp.int32)
    tile_grp_raw = jnp.sum((tile[:, None] >= tile_end[None, :]).astype(jnp.int32), axis=1)
    last_grp = jnp.max(jnp.where(cnt > 0, jnp.arange(n_cls, dtype=jnp.int32), 0))
    tile_grp = jnp.minimum(tile_grp_raw, last_grp)
    in_range = tile_grp_raw < n_cls
    rows_left = cnt[tile_grp] - (tile - tile_start[tile_grp]) * tm
    tile_cnt = jnp.where(in_range, jnp.clip(rows_left, 0, tm), 0).astype(jnp.int32)
    return pos.astype(jnp.int32), tile_grp, tile_cnt


def _moe(hx, router, fnw, onw, wg_bf, wu_bf, wd_bf, layer, tm, final_norm, by_pair):
    n = router.shape[0]
    if by_pair:
        n_e = 2
        pos, tile_cls, tile_cnt = _dispatch(router[:, 1], N_GROUPS * len(PAIR_LO), tm)
        grp, pair = tile_cls // len(PAIR_LO), tile_cls % len(PAIR_LO)
        members = [jnp.array(PAIR_LO, jnp.int32)[pair], jnp.array(PAIR_HI, jnp.int32)[pair]]
    else:
        n_e = EXPERTS_PER_GROUP
        pos, grp, tile_cnt = _dispatch(router[:, 0], N_GROUPS, tm)
        members = [jnp.full_like(grp, e) for e in range(n_e)]
    tile_exp = jnp.stack([grp * EXPERTS_PER_GROUP + m for m in members], axis=1).reshape(-1).astype(jnp.int32)
    n_tiles = tile_cnt.shape[0]
    fixed = lambda i, ids_r, exp_r, cnt_r: (0, 0)

    def wspecs(shape):
        return [pl.BlockSpec((None, None) + shape, lambda i, ids_r, exp_r, cnt_r, e=e: (layer, exp_r[i * n_e + e], 0, 0))
                for e in range(n_e)]

    return pl.pallas_call(
        functools.partial(_moe_kernel, tm=tm, n_e=n_e, final_norm=final_norm),
        grid_spec=pltpu.PrefetchScalarGridSpec(
            num_scalar_prefetch=3,
            grid=(n_tiles,),
            in_specs=[pl.BlockSpec(memory_space=pl.ANY),
                      pl.BlockSpec((1, D_MODEL), fixed), pl.BlockSpec((1, D_MODEL), fixed)]
                     + wspecs((D_MODEL, EXPERT_FF)) + wspecs((D_MODEL, EXPERT_FF)) + wspecs((EXPERT_FF, D_MODEL)),
            out_specs=pl.BlockSpec(memory_space=pl.ANY),
            scratch_shapes=[pltpu.VMEM((2, tm * ROW_CHUNKS, HEAD_W), f32), pltpu.VMEM((2, tm, D_MODEL), f32),
                            pltpu.SMEM((n_tiles * tm,), jnp.int32),
                            pltpu.SemaphoreType.DMA((2,)), pltpu.SemaphoreType.DMA((2,))]),
        out_shape=jax.ShapeDtypeStruct((n, D_MODEL), f32),
        compiler_params=pltpu.CompilerParams(dimension_semantics=("arbitrary",), vmem_limit_bytes=VMEM_LIMIT),
        name="moe",
    )(pos, tile_exp, tile_cnt, hx, fnw, onw, *([wg_bf] * n_e), *([wu_bf] * n_e), *([wd_bf] * n_e))


def _stacked_sample_queries(qd):
    bd, t = qd.shape[:2]
    q5 = qd.reshape(bd, t, N_HEADS, 2, DA_DK)
    eye = jnp.eye(2, dtype=qd.dtype)
    qz = q5[:, :, :, :, None, :] * eye[None, None, None, :, :, None]
    return qz.transpose(0, 2, 3, 1, 4, 5).reshape(bd, N_HEADS * 2 * t, HEAD_W)


def kernel(x_prompt, x_sample, cache_k, cache_v, state_hgrn, page_table, attn_norm_w, w_in, hgrn_lb, hgrn_norm_w,
           diff_lambda, diff_norm_w, w_o, ffn_norm_w, w_r1, b_r1, w_r2, b_r2, w_gate, w_up, w_down, final_norm_w):
    b, s = x_prompt.shape[:2]
    bd, t = x_sample.shape[:2]
    depth = w_in.shape[0]
    n_pages = page_table.shape[1]
    past_len = n_pages * PAGE_SIZE
    assert x_prompt.shape[2] == D_MODEL and w_in.shape[2] == N_SEG * SEG_W and t <= SAMPLE_PAD
    tp = SAMPLE_PAD

    tabs_p = _rope_tables(jnp.arange(s, dtype=jnp.int32))
    tabs_s = _rope_tables(jnp.tile(past_len + jnp.arange(t, dtype=jnp.int32), bd))
    p_lb = jax.nn.softmax(hgrn_lb.astype(f32), axis=0)
    lb_all = jnp.cumsum(p_lb, axis=0) - p_lb[0:1]

    w_in_bf = w_in.astype(bf16)
    w_o_bf = w_o.astype(bf16)
    wg_bf, wu_bf, wd_bf = w_gate.astype(bf16), w_up.astype(bf16), w_down.astype(bf16)
    pad_r = ROUTER_W - N_GROUPS - N_EXPERTS
    w_r = jnp.concatenate([w_r1, w_r2, jnp.zeros((depth, D_MODEL, pad_r), f32)], axis=2)
    w_r_hi = w_r.astype(bf16)
    w_r_lo = (w_r - w_r_hi.astype(f32)).astype(bf16)
    b_r = jnp.concatenate([b_r1, b_r2, jnp.zeros((depth, pad_r), f32)], axis=1)

    tm_p = 256 if (b * s) % 256 == 0 else b * s
    tq = 512 if s % 512 == 0 else s
    tb_p = 512 if s % 512 == 0 else s
    tm_moe_p = 256 if (b * s) % 256 == 0 else b * s
    n_s = bd * t
    n_pg = next(p for p in (16, 8, 4, 2, 1) if n_pages % p == 0)
    tm_proj = 512 if s % 512 == 0 else tm_p
    q_scale = DA_DK ** -0.5 * LOG2E

    hp = x_prompt.reshape(b * s, D_MODEL)
    hs = x_sample.reshape(n_s, D_MODEL)
    zeros_state = jnp.zeros((b, N_HEADS, HEAD_W, HEAD_W), f32)
    kp_l, vp_l, sp_l, ks_l, vs_l, ss_l = [], [], [], [], [], []
    for l in range(depth):
        lam_init = 0.8 - 0.6 * math.exp(-0.3 * l)
        dl = diff_lambda[l].astype(f32)
        lam = (jnp.exp(jnp.sum(dl[0] * dl[1])) - jnp.exp(jnp.sum(dl[2] * dl[3])) + lam_init).reshape(1)
        lb = lb_all[l].reshape(1, SEG_W)
        nw = attn_norm_w[l].reshape(1, D_MODEL)
        fnw = ffn_norm_w[l].reshape(1, D_MODEL)
        onw = final_norm_w.reshape(1, D_MODEL)
        hnw = hgrn_norm_w[l].reshape(1, HEAD_W)
        dnw = diff_norm_w[l].reshape(1, HEAD_W)
        last = l == depth - 1

        def tail(o_h, o_d, g_h, h, tm_mix, tm_moe, by_pair):
            hx, router = _mix(o_h, o_d, g_h, h, hnw, dnw, w_o_bf[l], fnw, w_r_hi[l], w_r_lo[l],
                              b_r[l].reshape(1, ROUTER_W), 1.0 - lam_init, tm_mix)
            return _moe(hx, router, fnw, onw, wg_bf, wu_bf, wd_bf, l, tm_moe, last, by_pair)

        qh, kh, ih, lf, gh_p, qd_p, kd, vd, kb_p, vb_p = _proj(hp, b * s, nw, w_in_bf[l], lb, tabs_p, tm_proj,
                                                               q_scale, seq=s)
        oh_p, s_fin = _hgrn(qh, kh, ih, lf, zeros_state, s, tb_p, HG_CHUNK)
        kp_l.append(kd.reshape(b, s, N_HEADS, HEAD_W))
        vp_l.append(vd.reshape(b, s, N_HEADS, HEAD_W))
        sp_l.append(s_fin)

        qh, kh, ih, lf, gh_s, qd, kd, vd, kb, vb = _proj(hs, n_s, nw, w_in_bf[l], lb, tabs_s, n_s, q_scale)
        pad = lambda a: jnp.pad(a.reshape(bd, t, SEG_W), ((0, 0), (0, tp - t), (0, 0)))
        flat = lambda a: pad(a).reshape(bd * tp, SEG_W)
        oh_s, s_fin = _hgrn(flat(qh), flat(kh), flat(ih), flat(lf), state_hgrn[l].astype(f32), tp, tp, tp)
        oh_s = oh_s.reshape(bd, tp, SEG_W)[:, :t].reshape(n_s, SEG_W)
        ks_l.append(kd.reshape(bd, t, N_HEADS, HEAD_W))
        vs_l.append(vd.reshape(bd, t, N_HEADS, HEAD_W))
        ss_l.append(s_fin)

        od_p, od_s = _attn(page_table, lam, qd_p, kb_p, vb_p, _stacked_sample_queries(pad(qd)), pad(kb), pad(vb),
                           cache_k, cache_v, l, b, s, tq, n_pg)
        hp = tail(oh_p, od_p, gh_p, hp, tm_p, tm_moe_p, True)
        hs = tail(oh_s, od_s[:, :t].reshape(n_s, SEG_W), gh_s, hs, n_s, n_s, False)

    return (hp.reshape(b, s, D_MODEL), hs.reshape(bd, t, D_MODEL), jnp.stack(kp_l), jnp.stack(vp_l),
            jnp.stack(sp_l), jnp.stack(ks_l), jnp.stack(vs_l), jnp.stack(ss_l))
```

```python
import functools
import math

import jax
import jax.numpy as jnp
from jax import lax
from jax.experimental import pallas as pl
from jax.experimental.pallas import tpu as pltpu

f32 = jnp.float32
bf16 = jnp.bfloat16

D_MODEL = 1024
N_HEADS = 4
HEAD_W = 128
SEG_W = N_HEADS * HEAD_W
N_SEG = 7
DA_DK = 64
ROT_DIM = DA_DK // 4
ROPE_THETA = 500000.0
PAGE_SIZE = 128
N_GROUPS = 4
EXPERTS_PER_GROUP = 4
N_EXPERTS = N_GROUPS * EXPERTS_PER_GROUP
PAIR_LO = (0, 0, 0, 1, 1, 2)
PAIR_HI = (1, 2, 3, 2, 3, 3)
EXPERT_FF = D_MODEL // 2
NORM_EPS = 1e-6
NEG = -1e30
F_MIN = 1e-30

HG_CHUNK = 64
LOG2E = math.log2(math.e)
SAMPLE_PAD = 8
GATE_LANE0 = N_GROUPS
ROUTER_W = 128
ROW_CHUNKS = (D_MODEL + ROUTER_W) // HEAD_W

VMEM_LIMIT = 48 * 1024 * 1024


def _dot(a, b):
    return jnp.dot(a, b, preferred_element_type=f32)


def _dot_nt(a, b):
    return lax.dot_general(a, b, (((1,), (1,)), ((), ())), preferred_element_type=f32)


def _dot_tn(a, b):
    return lax.dot_general(a, b, (((0,), (0,)), ((), ())), preferred_element_type=f32)


def _rms(x, w):
    return x * lax.rsqrt(jnp.mean(x * x, axis=-1, keepdims=True) + NORM_EPS) * w


def _split3(x):
    hi = x.astype(bf16)
    r1 = x - hi.astype(f32)
    mid = r1.astype(bf16)
    lo = (r1 - mid.astype(f32)).astype(bf16)
    return hi, mid, lo


def _proj_kernel(h_ref, nw_ref, w_ref, lb_ref, c_ref, sa_ref, sb_ref, *rest, q_scale, transposed, layer, first):
    qh_ref, kh_ref, ih_ref, lf_ref, gh_ref, qd_ref, kd_ref, vd_ref, kb_ref, vb_ref = rest[-10:]
    if first:
        for d in range(kd_ref.shape[0]):
            if d != layer:
                kd_ref[d] = jnp.zeros(kd_ref.shape[1:], f32)
                vd_ref[d] = jnp.zeros(vd_ref.shape[1:], f32)
        kd_ref, vd_ref = kd_ref.at[layer], vd_ref.at[layer]
    tm = h_ref.shape[0]
    xn = _rms(h_ref[...], nw_ref[...]).astype(bf16)

    def seg(i):
        return _dot(xn, w_ref[:, i * SEG_W:(i + 1) * SEG_W])

    qh_ref[...] = seg(0) * (HEAD_W ** -0.5)
    hf = seg(1)
    lb = lb_ref[...]
    e = jnp.exp(-jnp.abs(hf))
    r = 1.0 / (1.0 + e)
    pos = hf >= 0.0
    sig = jnp.where(pos, r, e * r)
    nsig = jnp.where(pos, e * r, r)
    f = lb + (1.0 - lb) * sig
    lf_ref[...] = jnp.log(jnp.maximum(f, F_MIN)) * LOG2E
    kh_ref[...] = (1.0 - lb) * nsig
    ih_ref[...] = seg(2)
    gh_ref[...] = seg(3)

    c = c_ref[...]
    sa = sa_ref[...]
    sb = sb_ref[...]

    def rope(z, hh):
        zz = z[:, hh * HEAD_W:(hh + 1) * HEAD_W]
        return zz * c + pltpu.roll(zz, HEAD_W - ROT_DIM // 2, 1) * sa + pltpu.roll(zz, ROT_DIM // 2, 1) * sb

    zq = seg(4)
    zk = seg(5)
    for hh in range(N_HEADS):
        sl = slice(hh * HEAD_W, (hh + 1) * HEAD_W)
        qr = rope(zq, hh) * q_scale
        if transposed:
            qd_ref[sl, :] = qr.T.astype(bf16)
        else:
            qd_ref[:, sl] = qr.astype(bf16)
        kr = rope(zk, hh)
        kd_ref[pl.ds(hh, tm, stride=N_HEADS), :] = kr
        kb_ref[:, sl] = kr.astype(bf16)
    vd = seg(6)
    for hh in range(N_HEADS):
        vd_ref[pl.ds(hh, tm, stride=N_HEADS), :] = vd[:, hh * HEAD_W:(hh + 1) * HEAD_W]
    vb_ref[...] = vd.T.astype(bf16) if transposed else vd.astype(bf16)


def _proj(h, n, nw, w_bf, lb, tabs, tm, q_scale, depth, layer, kv_prev, seq=None):
    first = kv_prev is None
    npos = tabs[0].shape[0] // tm
    row = lambda i: (i, 0)
    fixed = lambda i: (0, 0)
    tab = lambda i: (i % npos, 0)
    seg_f32 = jax.ShapeDtypeStruct((n, SEG_W), f32)
    seg_bf = jax.ShapeDtypeStruct((n, SEG_W), bf16)
    seg_spec = pl.BlockSpec((tm, SEG_W), row)
    rows_shape = jax.ShapeDtypeStruct((depth, n * N_HEADS, HEAD_W), f32)
    if first:
        rows_spec = pl.BlockSpec((depth, tm * N_HEADS, HEAD_W), lambda i: (0, i, 0))
        extra_specs, extra_args, aliases = [], (), {}
    else:
        rows_spec = pl.BlockSpec((None, tm * N_HEADS, HEAD_W), lambda i: (layer, i, 0))
        extra_specs, extra_args = [pl.BlockSpec(memory_space=pl.ANY)] * 2, tuple(kv_prev)
        aliases = {7: 6, 8: 7}
    if seq is None:
        t_shape, t_spec = seg_bf, seg_spec
    else:
        nb = seq // tm
        t_shape = jax.ShapeDtypeStruct((n // seq, SEG_W, seq), bf16)
        t_spec = pl.BlockSpec((None, SEG_W, tm), lambda i: (i // nb, 0, i % nb))
    return pl.pallas_call(
        functools.partial(_proj_kernel, q_scale=q_scale, transposed=seq is not None, layer=layer, first=first),
        grid=(n // tm,),
        in_specs=[pl.BlockSpec((tm, D_MODEL), row),
                  pl.BlockSpec((1, D_MODEL), fixed),
                  pl.BlockSpec((D_MODEL, N_SEG * SEG_W), fixed),
                  pl.BlockSpec((1, SEG_W), fixed),
                  pl.BlockSpec((tm, HEAD_W), tab),
                  pl.BlockSpec((tm, HEAD_W), tab),
                  pl.BlockSpec((tm, HEAD_W), tab)] + extra_specs,
        out_specs=[seg_spec] * 5 + [t_spec, rows_spec, rows_spec, seg_spec, t_spec],
        out_shape=[seg_f32] * 5 + [t_shape, rows_shape, rows_shape, seg_bf, t_shape],
        input_output_aliases=aliases,
        compiler_params=pltpu.CompilerParams(dimension_semantics=("arbitrary",), vmem_limit_bytes=VMEM_LIMIT),
        name="proj",
    )(h, nw, w_bf, lb, *tabs, *extra_args)


def _rope_tables(pos):
    half = ROT_DIM // 2
    inv = ROPE_THETA ** (-jnp.arange(0, ROT_DIM, 2, dtype=f32) / ROT_DIM)
    ang = pos.astype(f32)[:, None] * inv[None, :]
    cos, sin = jnp.cos(ang), jnp.sin(ang)
    t = pos.shape[0]
    rest = DA_DK - ROT_DIM
    c64 = jnp.concatenate([cos, cos, jnp.ones((t, rest), f32)], axis=1)
    sa64 = jnp.concatenate([-sin, jnp.zeros((t, half + rest), f32)], axis=1)
    sb64 = jnp.concatenate([jnp.zeros((t, half), f32), sin, jnp.zeros((t, rest), f32)], axis=1)
    return tuple(jnp.tile(a, (1, HEAD_W // DA_DK)) for a in (c64, sa64, sb64))


def _hgrn_chunk(q, k, v, lf, st, chunk, consts):
    sel, pair_masks, halves = consts
    heads = [slice(hh * HEAD_W, (hh + 1) * HEAD_W) for hh in range(N_HEADS)]
    terms = jnp.concatenate(_split3(lf), axis=1)
    gs3 = _dot(sel, terms)
    gs = gs3[:, :SEG_W] + gs3[:, SEG_W:2 * SEG_W] + gs3[:, 2 * SEG_W:]
    g = gs[:chunk]
    g_last = g[chunk - 1:chunk, :]
    qx = (q * jnp.exp2(g)).astype(bf16)
    q_bf, k_bf, v_bf = q.astype(bf16), k.astype(bf16), v.astype(bf16)
    st_bf = [s.astype(bf16) for s in st]
    o = [_dot_nt(qx[:, sl], st_bf[hh]) for hh, sl in enumerate(heads)]
    a = [_dot_nt(q_bf[:, sl], k_bf[:, sl]) * pair_masks[0] for sl in heads]
    n_mm = 1
    for lvl, h in enumerate(halves, start=1):
        if h % 8 == 0:
            ref = jnp.concatenate([jnp.broadcast_to(g[r0 + h - 1:r0 + h, :], (2 * h, SEG_W))
                                   for r0 in range(0, chunk, 2 * h)], axis=0)
        else:
            ref = gs[n_mm * chunk:(n_mm + 1) * chunk]
            n_mm += 1
        e = jnp.exp2(-jnp.abs(g - ref))
        qe, ke = (q * e).astype(bf16), (k * e).astype(bf16)
        a = [a[hh] + _dot_nt(qe[:, sl], ke[:, sl]) * pair_masks[lvl] for hh, sl in enumerate(heads)]
    kdec = (k * jnp.exp2(g_last - g)).astype(bf16)
    decay = jnp.exp2(g_last)
    o = jnp.concatenate([o[hh] + _dot(a[hh].astype(bf16), v_bf[:, sl]) for hh, sl in enumerate(heads)], axis=1)
    st_new = [st[hh] * decay[:, sl] + _dot_tn(v_bf[:, sl], kdec[:, sl]) for hh, sl in enumerate(heads)]
    return o, st_new


def _hgrn_consts(chunk):
    r = lax.broadcasted_iota(jnp.int32, (chunk, chunk), 0)
    c = lax.broadcasted_iota(jnp.int32, (chunk, chunk), 1)
    sels = [jnp.where(c <= r, 1.0, 0.0)]
    masks = [jnp.where(c == r, 1.0, 0.0)]
    halves = []
    h = chunk // 2
    while h >= 1:
        blk = -(2 * h)
        if h % 8:
            sels.append(jnp.where(c <= (r & blk) + (h - 1), 1.0, 0.0))
        same_block = (r & blk) == (c & blk)
        masks.append(jnp.where(same_block & ((r & h) != 0) & ((c & h) == 0), 1.0, 0.0))
        halves.append(h)
        h //= 2
    return jnp.concatenate(sels, axis=0).astype(bf16), masks, halves


def _hgrn_kernel(q_ref, k_ref, v_ref, lf_ref, s0_ref, o_ref, sfin_ref, st_ref, *, chunk, n_chunks):
    j = pl.program_id(1)
    consts = _hgrn_consts(chunk)

    @pl.when(j == 0)
    def _():
        for hh in range(N_HEADS):
            st_ref[hh] = s0_ref[hh].T

    def body(c, carry):
        rows = pl.ds(pl.multiple_of(c * chunk, chunk), chunk)
        o, st_new = _hgrn_chunk(q_ref[rows, :], k_ref[rows, :], v_ref[rows, :], lf_ref[rows, :],
                                [st_ref[hh] for hh in range(N_HEADS)], chunk, consts)
        o_ref[rows, :] = o
        for hh in range(N_HEADS):
            st_ref[hh] = st_new[hh]
        return carry

    lax.fori_loop(0, n_chunks, body, 0, unroll=4 if n_chunks % 4 == 0 else 1)

    @pl.when(j == pl.num_programs(1) - 1)
    def _():
        for hh in range(N_HEADS):
            sfin_ref[hh] = st_ref[hh].T


def _hgrn(q, k, v, lf, s0, t, tb, chunk):
    n = q.shape[0]
    b = n // t
    nj = t // tb
    tok = pl.BlockSpec((tb, SEG_W), lambda bi, j: (bi * nj + j, 0))
    st = pl.BlockSpec((None, N_HEADS, HEAD_W, HEAD_W), lambda bi, j: (bi, 0, 0, 0))
    return pl.pallas_call(
        functools.partial(_hgrn_kernel, chunk=chunk,n_chunks=tb // chunk),
        grid=(b, nj),
        in_specs=[tok, tok, tok, tok, st],
        out_specs=[tok, st],
        out_shape=[jax.ShapeDtypeStruct((n, SEG_W), f32), jax.ShapeDtypeStruct(s0.shape, f32)],
        scratch_shapes=[pltpu.VMEM((N_HEADS, HEAD_W, HEAD_W), f32)],
        compiler_params=pltpu.CompilerParams(dimension_semantics=("arbitrary", "arbitrary"),
                                             vmem_limit_bytes=VMEM_LIMIT),
        name="hgrn",
    )(q, k, v, lf, s0)


def _stack_maps(q):
    lane = lax.broadcasted_iota(jnp.int32, q.shape, 1)
    zero = jnp.zeros_like(q)
    return jnp.concatenate([jnp.where(lane < DA_DK, q, zero), jnp.where(lane >= DA_DK, q, zero)], axis=0)


def _softmax_step(s, v_bf, m_ref, l_ref, acc_ref, rows=None):
    sl = slice(None) if rows is None else rows
    m_prev = m_ref[sl, :]
    m_new = jnp.maximum(m_prev, jnp.max(s, axis=-1, keepdims=True))
    alpha = jnp.exp2(m_prev - m_new)
    p = jnp.exp2(s - m_new)
    l_ref[sl, :] = alpha * l_ref[sl, :] + jnp.sum(p, axis=-1, keepdims=True)
    acc_ref[sl, :] = alpha * acc_ref[sl, :] + _dot(p.astype(bf16), v_bf)
    m_ref[sl, :] = m_new


def _attn_prompt_step(i, lam_ref, qt_ref, k_ref, vt_ref, o_ref, m_ref, l_ref, acc_ref, s_ref, tq, n_q):
    qt = qt_ref[...]
    sub = lax.broadcasted_iota(jnp.int32, qt.shape, 0)
    zero = jnp.zeros_like(qt)
    qs = jnp.concatenate([jnp.where(sub < DA_DK, qt, zero), jnp.where(sub >= DA_DK, qt, zero)], axis=1)
    m_ref[...] = jnp.full(m_ref.shape, NEG, f32)
    l_ref[...] = jnp.zeros(l_ref.shape, f32)
    acc_ref[...] = jnp.zeros(acc_ref.shape, f32)

    def scores(j):
        return _dot(k_ref[j * tq:(j + 1) * tq, :], qs)

    def update(s, vt):
        m_prev = m_ref[...]
        m_new = jnp.maximum(m_prev, jnp.max(s, axis=0, keepdims=True))
        alpha = jnp.exp2(m_prev - m_new)
        p = jnp.exp2(s - m_new)
        l_ref[...] = alpha * l_ref[...] + jnp.sum(p, axis=0, keepdims=True)
        acc_ref[...] = alpha * acc_ref[...] + _dot(vt, p.astype(bf16))
        m_ref[...] = m_new

    s_ref[0] = scores(0)
    for j in range(n_q - 1):
        @pl.when(j < i)
        def _():
            s_ref[(j + 1) % 2] = scores(j + 1)
            update(s_ref[j % 2], vt_ref[:, j * tq:(j + 1) * tq])

    s = s_ref[i % 2]
    key = lax.broadcasted_iota(jnp.int32, s.shape, 0)
    qry = lax.broadcasted_iota(jnp.int32, s.shape, 1) & (tq - 1)
    update(jnp.where(key <= qry, s, NEG), vt_ref[:, pl.ds(pl.multiple_of(i * tq, tq), tq)])
    o = acc_ref[...] / l_ref[...]
    o_ref[...] = (o[:, :tq] - lam_ref[0] * o[:, tq:]).T


def _attn_sample_step(j, n_j, lam_ref, qs_ref, kn_ref, vn_ref, k_refs, v_refs, o_ref, m_ref, l_ref, acc_ref):
    t = SAMPLE_PAD

    @pl.when(j == 0)
    def _():
        m_ref[...] = jnp.full(m_ref.shape, NEG, f32)
        l_ref[...] = jnp.zeros(l_ref.shape, f32)
        acc_ref[...] = jnp.zeros(acc_ref.shape, f32)

    head_rows = [slice(hh * 2 * t, (hh + 1) * 2 * t) for hh in range(N_HEADS)]
    head_toks = [pl.ds(hh, PAGE_SIZE, stride=N_HEADS) for hh in range(N_HEADS)]
    qs = qs_ref[...]
    s = [_dot_nt(qs[rows, :], jnp.concatenate([kr[toks, :].astype(bf16) for kr in k_refs], axis=0))
         for rows, toks in zip(head_rows, head_toks)]
    m_prev = m_ref[...]
    m_new = jnp.maximum(m_prev, jnp.concatenate([jnp.max(sh, axis=-1, keepdims=True) for sh in s], axis=0))
    alpha = jnp.exp2(m_prev - m_new)
    p = [jnp.exp2(sh - m_new[rows, :]) for sh, rows in zip(s, head_rows)]
    l_ref[...] = alpha * l_ref[...] + jnp.concatenate([jnp.sum(ph, axis=-1, keepdims=True) for ph in p], axis=0)
    pv = [_dot(ph.astype(bf16), jnp.concatenate([vr[toks, :].astype(bf16) for vr in v_refs], axis=0))
          for ph, toks in zip(p, head_toks)]
    acc_ref[...] = alpha * acc_ref[...] + jnp.concatenate(pv, axis=0)
    m_ref[...] = m_new

    @pl.when(j == n_j - 1)
    def _():
        for hh in range(N_HEADS):
            rows = slice(hh * 2 * t, (hh + 1) * 2 * t)
            sl = slice(hh * HEAD_W, (hh + 1) * HEAD_W)
            s = _dot_nt(qs_ref[rows, :], kn_ref[:, sl].astype(bf16))
            row = lax.broadcasted_iota(jnp.int32, s.shape, 0) & (t - 1)
            col = lax.broadcasted_iota(jnp.int32, s.shape, 1)
            s = jnp.where(col <= row, s, NEG)
            _softmax_step(s, vn_ref[:, sl].astype(bf16), m_ref, l_ref, acc_ref, rows)
            o = acc_ref[rows, :] / l_ref[rows, :]
            o_ref[:, sl] = o[:t] - lam_ref[0] * o[t:]


def _attn_kernel(pt_ref, lam_ref, qt_ref, k_ref, vt_ref, qs_ref, kn_ref, vn_ref, *rest,
                 tq, n_q, n_pg, n_j, steps_p, steps_s):
    del pt_ref
    k_pages, v_pages = rest[:n_pg], rest[n_pg:2 * n_pg]
    op_ref, os_ref, pm_ref, pl_ref, pacc_ref, ps_ref, sm_ref, sl_ref, sacc_ref = rest[2 * n_pg:]
    step = pl.program_id(0)

    def prompt():
        _attn_prompt_step(step % n_q, lam_ref, qt_ref, k_ref, vt_ref, op_ref, pm_ref, pl_ref, pacc_ref, ps_ref,
                          tq, n_q)

    def sample():
        _attn_sample_step(step % n_j, n_j, lam_ref, qs_ref, kn_ref, vn_ref, k_pages, v_pages, os_ref,
                          sm_ref, sl_ref, sacc_ref)

    if steps_p == steps_s:
        prompt()
        sample()
    else:
        pl.when(step < steps_p)(prompt)
        pl.when(step < steps_s)(sample)


def _attn(page_table, lam, qt_bf, k_bf, vt_bf, qs, kn, vn, cache_k, cache_v, layer, b, s, tq, n_pg):
    bd, n_pages = page_table.shape
    t = SAMPLE_PAD
    n_q, n_j = s // tq, n_pages // n_pg
    steps_p, steps_s = b * N_HEADS * n_q, bd * n_j
    k3 = k_bf.reshape(b, s, SEG_W)
    cache_k = cache_k.reshape(cache_k.shape[:2] + (PAGE_SIZE * N_HEADS, HEAD_W))
    cache_v = cache_v.reshape(cache_v.shape[:2] + (PAGE_SIZE * N_HEADS, HEAD_W))

    def p_idx(step):
        lin = jnp.minimum(step, steps_p - 1)
        return lin // (N_HEADS * n_q), (lin // n_q) % N_HEADS, lin % n_q

    def s_idx(step):
        lin = jnp.minimum(step, steps_s - 1)
        return lin // n_j, lin % n_j

    def on_p(f):
        return lambda step, pt: f(*p_idx(step))

    def per_seq(shape):
        return pl.BlockSpec((None,) + shape, lambda step, pt: (s_idx(step)[0], 0, 0))

    def page_spec(p):
        def index(step, pt):
            sb, j = s_idx(step)
            return layer, pt[sb * n_pages + j * n_pg + p], 0, 0
        return pl.BlockSpec((None, None, PAGE_SIZE * N_HEADS, HEAD_W), index)

    pages = [page_spec(p) for p in range(n_pg)]
    out_p, out_s = pl.pallas_call(
        functools.partial(_attn_kernel, tq=tq, n_q=n_q, n_pg=n_pg, n_j=n_j, steps_p=steps_p, steps_s=steps_s),
        grid_spec=pltpu.PrefetchScalarGridSpec(
            num_scalar_prefetch=1,
            grid=(max(steps_p, steps_s),),
            in_specs=[pl.BlockSpec(memory_space=pltpu.SMEM),
                      pl.BlockSpec((None, HEAD_W, tq), on_p(lambda bi, h, i: (bi, h, i))),
                      pl.BlockSpec((None, s, HEAD_W), on_p(lambda bi, h, i: (bi, 0, h))),
                      pl.BlockSpec((None, HEAD_W, s), on_p(lambda bi, h, i: (bi, h, 0))),
                      per_seq((2 * t * N_HEADS, HEAD_W)), per_seq((t, SEG_W)), per_seq((t, SEG_W))] + pages + pages,
            out_specs=[pl.BlockSpec((None, tq, HEAD_W), on_p(lambda bi, h, i: (bi, i, h))), per_seq((t, SEG_W))],
            scratch_shapes=[pltpu.VMEM((1, 2 * tq), f32), pltpu.VMEM((1, 2 * tq), f32),
                            pltpu.VMEM((HEAD_W, 2 * tq), f32), pltpu.VMEM((2, tq, 2 * tq), f32),
                            pltpu.VMEM((2 * t * N_HEADS, 1), f32), pltpu.VMEM((2 * t * N_HEADS, 1), f32),
                            pltpu.VMEM((2 * t * N_HEADS, HEAD_W), f32)]),
        out_shape=[jax.ShapeDtypeStruct((b, s, SEG_W), f32), jax.ShapeDtypeStruct((bd, t, SEG_W), f32)],
        compiler_params=pltpu.CompilerParams(dimension_semantics=("arbitrary",), vmem_limit_bytes=VMEM_LIMIT),
        name="attn",
    )(page_table.reshape(-1), lam, qt_bf, k3, vt_bf, qs, kn, vn, *([cache_k] * n_pg), *([cache_v] * n_pg))
    return out_p.reshape(b * s, SEG_W), out_s


def _route(logits):
    lane = lax.broadcasted_iota(jnp.int32, logits.shape, 1).astype(f32)
    big = float(ROUTER_W)
    is_g = lane < N_GROUPS
    m1 = jnp.max(jnp.where(is_g, logits, -jnp.inf), axis=-1, keepdims=True)
    grp = jnp.min(jnp.where(is_g & (logits == m1), lane, big), axis=-1, keepdims=True)
    p_grp = 1.0 / jnp.sum(jnp.where(is_g, jnp.exp(logits - m1), 0.0), axis=-1, keepdims=True)
    lo = GATE_LANE0 + EXPERTS_PER_GROUP * grp
    in_g = (lane >= lo) & (lane < lo + EXPERTS_PER_GROUP)
    v1 = jnp.max(jnp.where(in_g, logits, -jnp.inf), axis=-1, keepdims=True)
    i1 = jnp.min(jnp.where(in_g & (logits == v1), lane, big), axis=-1, keepdims=True)
    rest = in_g & (lane != i1)
    v2 = jnp.max(jnp.where(rest, logits, -jnp.inf), axis=-1, keepdims=True)
    i2 = jnp.min(jnp.where(rest & (logits == v2), lane, big), axis=-1, keepdims=True)
    e = jnp.exp(v2 - v1)
    w1 = 1.0 / (1.0 + e)
    w2 = e / (1.0 + e)
    gates = jnp.where(lane == i1, p_grp * w1, 0.0) + jnp.where(lane == i2, p_grp * w2, 0.0)
    a = jnp.minimum(i1, i2) - lo
    b = jnp.maximum(i1, i2) - lo
    pair = jnp.where(a == 0.0, b - 1.0, jnp.where(a == 1.0, b + 1.0, 5.0))
    return jnp.where(lane == 0.0, grp, jnp.where(lane == 1.0, grp * float(len(PAIR_LO)) + pair, gates))


def _mix_kernel(oh_ref, od_ref, gh_ref, h_ref, hnw_ref, dnw_ref, wo_ref, fnw_ref, wrh_ref, wrl_ref, br_ref,
                hx_ref, rt_ref, *, od_scale):
    parts = []
    for hh in range(N_HEADS):
        sl = slice(hh * HEAD_W, (hh + 1) * HEAD_W)
        gate = 1.0 / (1.0 + jnp.exp(-gh_ref[:, sl]))
        parts.append((_rms(oh_ref[:, sl], hnw_ref[...]) * gate).astype(bf16))
    for hh in range(N_HEADS):
        sl = slice(hh * HEAD_W, (hh + 1) * HEAD_W)
        parts.append((_rms(od_ref[:, sl], dnw_ref[...]) * od_scale).astype(bf16))
    h2 = h_ref[...] + _dot(jnp.concatenate(parts, axis=1), wo_ref[...])
    xn = _rms(h2, fnw_ref[...])
    x_hi = xn.astype(bf16)
    x_lo = (xn - x_hi.astype(f32)).astype(bf16)
    logits = _dot(x_hi, wrh_ref[...]) + _dot(x_lo, wrh_ref[...]) + _dot(x_hi, wrl_ref[...]) + br_ref[...]
    tm = h2.shape[0]
    for c in range(D_MODEL // HEAD_W):
        hx_ref[pl.ds(c, tm, stride=ROW_CHUNKS), :] = h2[:, c * HEAD_W:(c + 1) * HEAD_W]
    router = _route(logits)
    hx_ref[pl.ds(D_MODEL // HEAD_W, tm, stride=ROW_CHUNKS), :] = router
    rt_ref[...] = router


def _mix(o_h, o_d, g_h, h, hnw, dnw, wo_bf, fnw, wr_hi, wr_lo, br, od_scale, tm):
    n = o_h.shape[0]
    row = lambda i: (i, 0)
    fixed = lambda i: (0, 0)
    seg = pl.BlockSpec((tm, SEG_W), row)
    return pl.pallas_call(
        functools.partial(_mix_kernel, od_scale=od_scale),
        grid=(n // tm,),
        in_specs=[seg, seg, seg, pl.BlockSpec((tm, D_MODEL), row),
                  pl.BlockSpec((1, HEAD_W), fixed), pl.BlockSpec((1, HEAD_W), fixed),
                  pl.BlockSpec((D_MODEL, D_MODEL), fixed), pl.BlockSpec((1, D_MODEL), fixed),
                  pl.BlockSpec((D_MODEL, ROUTER_W), fixed), pl.BlockSpec((D_MODEL, ROUTER_W), fixed),
                  pl.BlockSpec((1, ROUTER_W), fixed)],
        out_specs=[pl.BlockSpec((tm * ROW_CHUNKS, HEAD_W), row), pl.BlockSpec((tm, ROUTER_W), row)],
        out_shape=[jax.ShapeDtypeStruct((n * ROW_CHUNKS, HEAD_W), f32), jax.ShapeDtypeStruct((n, ROUTER_W), f32)],
        compiler_params=pltpu.CompilerParams(dimension_semantics=("arbitrary",), vmem_limit_bytes=VMEM_LIMIT),
        name="mix",
    )(o_h, o_d, g_h, h, hnw, dnw, wo_bf, fnw, wr_hi, wr_lo, br)


def _moe_kernel(pos_ref, exp_ref, cnt_ref, hx_hbm, fnw_ref, onw_ref, *rest, tm, n_e, final_norm):
    wg_refs, wu_refs, wd_refs = rest[:n_e], rest[n_e:2 * n_e], rest[2 * n_e:3 * n_e]
    out_hbm, xbuf, ybuf, ids_ref, gsem, ssem = rest[3 * n_e:]
    i = pl.program_id(0)
    n_tiles = pl.num_programs(0)
    slot = i % 2

    def gather_copy(tile, sl, r):
        tok = ids_ref[tile * tm + r]
        return pltpu.make_async_copy(hx_hbm.at[pl.ds(tok * ROW_CHUNKS, ROW_CHUNKS)],
                                     xbuf.at[sl, pl.ds(r * ROW_CHUNKS, ROW_CHUNKS)], gsem.at[sl])

    def scatter_copy(tile, sl, r):
        tok = ids_ref[tile * tm + r]
        return pltpu.make_async_copy(ybuf.at[sl, pl.ds(r, 1)], out_hbm.at[pl.ds(tok, 1)], ssem.at[sl])

    @pl.when(i == 0)
    def _():
        def place(t, carry):
            ids_ref[pos_ref[t]] = t
            return carry
        lax.fori_loop(0, pos_ref.shape[0], place, 0, unroll=8)

    def start_rows(n, copy):
        def pair(k, carry):
            copy(2 * k).start(priority=0)
            copy(2 * k + 1).start(priority=1)
            return carry
        if isinstance(n, int):
            assert n % 2 == 0
            lax.fori_loop(0, n // 2, pair, 0, unroll=4)
        else:
            lax.fori_loop(0, n // 2, pair, 0)

            @pl.when(n % 2 == 1)
            def _():
                copy(n - 1).start(priority=0)

    def wait_rows(n, copy):
        def body(r, carry):
            copy(r).wait()
            return carry
        lax.fori_loop(0, n, body, 0)

    def start_tile(tile, copy):
        cnt = cnt_ref[tile]

        @pl.when(cnt == tm)
        def _():
            start_rows(tm, copy)

        @pl.when(cnt < tm)
        def _():
            start_rows(cnt, copy)

    def wait_tile(tile, copy, whole):
        cnt = cnt_ref[tile]

        @pl.when(cnt == tm)
        def _():
            whole.wait()

        @pl.when(cnt < tm)
        def _():
            wait_rows(cnt, copy)

    def gather_start(tile, sl):
        start_tile(tile, lambda r: gather_copy(tile, sl, r))

    def gather_wait(tile, sl):
        wait_tile(tile, lambda r: gather_copy(tile, sl, r),
                  pltpu.make_async_copy(hx_hbm.at[pl.ds(0, tm * ROW_CHUNKS)], xbuf.at[sl], gsem.at[sl]))

    def scatter_start(tile, sl):
        start_tile(tile, lambda r: scatter_copy(tile, sl, r))

    def scatter_wait(tile, sl):
        wait_tile(tile, lambda r: scatter_copy(tile, sl, r),
                  pltpu.make_async_copy(ybuf.at[sl], out_hbm.at[pl.ds(0, tm)], ssem.at[sl]))

    @pl.when(i == 0)
    def _():
        xbuf[...] = jnp.zeros(xbuf.shape, f32)
        gather_start(0, 0)

    @pl.when(i + 1 < n_tiles)
    def _():
        gather_start(i + 1, 1 - slot)

    gather_wait(i, slot)

    @pl.when(i >= 2)
    def _():
        scatter_wait(i - 2, slot)

    @pl.when(cnt_ref[i] > 0)
    def _():
        xs = xbuf.at[slot]
        chunk = lambda c: xs[pl.ds(c, tm, stride=ROW_CHUNKS), :]
        h2 = jnp.concatenate([chunk(c) for c in range(D_MODEL // HEAD_W)], axis=1)
        router = chunk(D_MODEL // HEAD_W)
        xn = _rms(h2, fnw_ref[...]).astype(bf16)
        lane = lax.broadcasted_iota(jnp.int32, router.shape, 1)
        acc = h2
        for e in range(n_e):
            gate_lane = GATE_LANE0 + exp_ref[i * n_e + e]
            gate = jnp.sum(jnp.where(lane == gate_lane, router, 0.0), axis=-1, keepdims=True)
            a = _dot(xn, wg_refs[e][...])
            hid = a / (1.0 + jnp.exp(-a)) * _dot(xn, wu_refs[e][...])
            acc = acc + _dot((hid * gate).astype(bf16), wd_refs[e][...])
        if final_norm:
            acc = _rms(acc, onw_ref[...])
        ybuf[slot] = acc

    scatter_start(i, slot)

    @pl.when(i == n_tiles - 1)
    def _():
        @pl.when(i >= 1)
        def _():
            scatter_wait(i - 1, 1 - slot)
        scatter_wait(i, slot)


def _dispatch(cls_f32, n_cls, tm):
    n = cls_f32.shape[0]
    n_tiles = n // tm + n_cls
    grp = cls_f32.astype(jnp.int32)
    onehot = (grp[:, None] == jnp.arange(n_cls, dtype=jnp.int32)[None, :]).astype(jnp.int32)
    csum = jnp.cumsum(onehot, axis=0)
    cnt = csum[-1]
    rank = jnp.sum(csum * onehot, axis=1) - 1
    tiles_g = (cnt + tm - 1) // tm
    tile_end = jnp.cumsum(tiles_g)
    tile_start = tile_end - tiles_g
    pos = jnp.sum(onehot * (tile_start * tm)[None, :], axis=1) + rank
    tile = jnp.arange(n_tiles, dtype=jnp.int32)
    tile_grp_raw = jnp.sum((tile[:, None] >= tile_end[None, :]).astype(jnp.int32), axis=1)
    last_grp = jnp.max(jnp.where(cnt > 0, jnp.arange(n_cls, dtype=jnp.int32), 0))
    tile_grp = jnp.minimum(tile_grp_raw, last_grp)
    in_range = tile_grp_raw < n_cls
    rows_left = cnt[tile_grp] - (tile - tile_start[tile_grp]) * tm
    tile_cnt = jnp.where(in_range, jnp.clip(rows_left, 0, tm), 0).astype(jnp.int32)
    return pos.astype(jnp.int32), tile_grp, tile_cnt


def _moe(hx, router, fnw, onw, wg_bf, wu_bf, wd_bf, layer, tm, final_norm, by_pair):
    n = router.shape[0]
    if by_pair:
        n_e = 2
        pos, tile_cls, tile_cnt = _dispatch(router[:, 1], N_GROUPS * len(PAIR_LO), tm)
        grp, pair = tile_cls // len(PAIR_LO), tile_cls % len(PAIR_LO)
        members = [jnp.array(PAIR_LO, jnp.int32)[pair], jnp.array(PAIR_HI, jnp.int32)[pair]]
    else:
        n_e = EXPERTS_PER_GROUP
        pos, grp, tile_cnt = _dispatch(router[:, 0], N_GROUPS, tm)
        members = [jnp.full_like(grp, e) for e in range(n_e)]
    tile_exp = jnp.stack([grp * EXPERTS_PER_GROUP + m for m in members], axis=1).reshape(-1).astype(jnp.int32)
    n_tiles = tile_cnt.shape[0]
    fixed = lambda i, ids_r, exp_r, cnt_r: (0, 0)

    def wspecs(shape):
        return [pl.BlockSpec((None, None) + shape, lambda i, ids_r, exp_r, cnt_r, e=e: (layer, exp_r[i * n_e + e], 0, 0))
                for e in range(n_e)]

    return pl.pallas_call(
        functools.partial(_moe_kernel, tm=tm, n_e=n_e, final_norm=final_norm),
        grid_spec=pltpu.PrefetchScalarGridSpec(
            num_scalar_prefetch=3,
            grid=(n_tiles,),
            in_specs=[pl.BlockSpec(memory_space=pl.ANY),
                      pl.BlockSpec((1, D_MODEL), fixed), pl.BlockSpec((1, D_MODEL), fixed)]
                     + wspecs((D_MODEL, EXPERT_FF)) + wspecs((D_MODEL, EXPERT_FF)) + wspecs((EXPERT_FF, D_MODEL)),
            out_specs=pl.BlockSpec(memory_space=pl.ANY),
            scratch_shapes=[pltpu.VMEM((2, tm * ROW_CHUNKS, HEAD_W), f32), pltpu.VMEM((2, tm, D_MODEL), f32),
                            pltpu.SMEM((n_tiles * tm,), jnp.int32),
                            pltpu.SemaphoreType.DMA((2,)), pltpu.SemaphoreType.DMA((2,))]),
        out_shape=jax.ShapeDtypeStruct((n, D_MODEL), f32),
        compiler_params=pltpu.CompilerParams(dimension_semantics=("arbitrary",), vmem_limit_bytes=VMEM_LIMIT),
        name="moe",
    )(pos, tile_exp, tile_cnt, hx, fnw, onw, *([wg_bf] * n_e), *([wu_bf] * n_e), *([wd_bf] * n_e))


def _stacked_sample_queries(qd):
    bd, t = qd.shape[:2]
    q5 = qd.reshape(bd, t, N_HEADS, 2, DA_DK)
    eye = jnp.eye(2, dtype=qd.dtype)
    qz = q5[:, :, :, :, None, :] * eye[None, None, None, :, :, None]
    return qz.transpose(0, 2, 3, 1, 4, 5).reshape(bd, N_HEADS * 2 * t, HEAD_W)


def kernel(x_prompt, x_sample, cache_k, cache_v, state_hgrn, page_table, attn_norm_w, w_in, hgrn_lb, hgrn_norm_w,
           diff_lambda, diff_norm_w, w_o, ffn_norm_w, w_r1, b_r1, w_r2, b_r2, w_gate, w_up, w_down, final_norm_w):
    b, s = x_prompt.shape[:2]
    bd, t = x_sample.shape[:2]
    depth = w_in.shape[0]
    n_pages = page_table.shape[1]
    past_len = n_pages * PAGE_SIZE
    assert x_prompt.shape[2] == D_MODEL and w_in.shape[2] == N_SEG * SEG_W and t <= SAMPLE_PAD
    tp = SAMPLE_PAD

    tabs_p = _rope_tables(jnp.arange(s, dtype=jnp.int32))
    tabs_s = _rope_tables(jnp.tile(past_len + jnp.arange(t, dtype=jnp.int32), bd))
    p_lb = jax.nn.softmax(hgrn_lb.astype(f32), axis=0)
    lb_all = jnp.cumsum(p_lb, axis=0) - p_lb[0:1]

    w_in_bf = w_in.astype(bf16)
    w_o_bf = w_o.astype(bf16)
    wg_bf, wu_bf, wd_bf = w_gate.astype(bf16), w_up.astype(bf16), w_down.astype(bf16)
    pad_r = ROUTER_W - N_GROUPS - N_EXPERTS
    w_r = jnp.concatenate([w_r1, w_r2, jnp.zeros((depth, D_MODEL, pad_r), f32)], axis=2)
    w_r_hi = w_r.astype(bf16)
    w_r_lo = (w_r - w_r_hi.astype(f32)).astype(bf16)
    b_r = jnp.concatenate([b_r1, b_r2, jnp.zeros((depth, pad_r), f32)], axis=1)

    tm_p = 256 if (b * s) % 256 == 0 else b * s
    tq = 512 if s % 512 == 0 else s
    tb_p = 512 if s % 512 == 0 else s
    tm_moe_p = 256 if (b * s) % 256 == 0 else b * s
    n_s = bd * t
    n_pg = next(p for p in (16, 8, 4, 2, 1) if n_pages % p == 0)
    tm_proj = 512 if s % 512 == 0 else tm_p
    q_scale = DA_DK ** -0.5 * LOG2E

    hp = x_prompt.reshape(b * s, D_MODEL)
    hs = x_sample.reshape(n_s, D_MODEL)
    zeros_state = jnp.zeros((b, N_HEADS, HEAD_W, HEAD_W), f32)
    sp_l, ss_l = [], []
    kv_p = kv_s = None
    for l in range(depth):
        lam_init = 0.8 - 0.6 * math.exp(-0.3 * l)
        dl = diff_lambda[l].astype(f32)
        lam = (jnp.exp(jnp.sum(dl[0] * dl[1])) - jnp.exp(jnp.sum(dl[2] * dl[3])) + lam_init).reshape(1)
        lb = lb_all[l].reshape(1, SEG_W)
        nw = attn_norm_w[l].reshape(1, D_MODEL)
        fnw = ffn_norm_w[l].reshape(1, D_MODEL)
        onw = final_norm_w.reshape(1, D_MODEL)
        hnw = hgrn_norm_w[l].reshape(1, HEAD_W)
        dnw = diff_norm_w[l].reshape(1, HEAD_W)
        last = l == depth - 1

        def tail(o_h, o_d, g_h, h, tm_mix, tm_moe, by_pair):
            hx, router = _mix(o_h, o_d, g_h, h, hnw, dnw, w_o_bf[l], fnw, w_r_hi[l], w_r_lo[l],
                              b_r[l].reshape(1, ROUTER_W), 1.0 - lam_init, tm_mix)
            return _moe(hx, router, fnw, onw, wg_bf, wu_bf, wd_bf, l, tm_moe, last, by_pair)

        qh, kh, ih, lf, gh_p, qd_p, kd, vd, kb_p, vb_p = _proj(hp, b * s, nw, w_in_bf[l], lb, tabs_p, tm_proj,
                                                               q_scale, depth, l, kv_p, seq=s)
        kv_p = (kd, vd)
        oh_p, s_fin = _hgrn(qh, kh, ih, lf, zeros_state, s, tb_p, HG_CHUNK)
        sp_l.append(s_fin)

        qh, kh, ih, lf, gh_s, qd, kd, vd, kb, vb = _proj(hs, n_s, nw, w_in_bf[l], lb, tabs_s, n_s, q_scale,
                                                         depth, l, kv_s)
        kv_s = (kd, vd)
        pad = lambda a: jnp.pad(a.reshape(bd, t, SEG_W), ((0, 0), (0, tp - t), (0, 0)))
        flat = lambda a: pad(a).reshape(bd * tp, SEG_W)
        oh_s, s_fin = _hgrn(flat(qh), flat(kh), flat(ih), flat(lf), state_hgrn[l].astype(f32), tp, tp, tp)
        oh_s = oh_s.reshape(bd, tp, SEG_W)[:, :t].reshape(n_s, SEG_W)
        ss_l.append(s_fin)

        od_p, od_s = _attn(page_table, lam, qd_p, kb_p, vb_p, _stacked_sample_queries(pad(qd)), pad(kb), pad(vb),
                           cache_k, cache_v, l, b, s, tq, n_pg)
        hp = tail(oh_p, od_p, gh_p, hp, tm_p, tm_moe_p, True)
        hs = tail(oh_s, od_s[:, :t].reshape(n_s, SEG_W), gh_s, hs, n_s, n_s, False)

    rows_p = lambda a: a.reshape(depth, b, s, N_HEADS, HEAD_W)
    rows_s = lambda a: a.reshape(depth, bd, t, N_HEADS, HEAD_W)
    return (hp.reshape(b, s, D_MODEL), hs.reshape(bd, t, D_MODEL), rows_p(kv_p[0]), rows_p(kv_p[1]),
            jnp.stack(sp_l), rows_s(kv_s[0]), rows_s(kv_s[1]), jnp.stack(ss_l))
```

```python
import functools
import math

import jax
import jax.numpy as jnp
from jax import lax
from jax.experimental import pallas as pl
from jax.experimental.pallas import tpu as pltpu

f32 = jnp.float32
bf16 = jnp.bfloat16

D_MODEL = 1024
N_HEADS = 4
HEAD_W = 128
SEG_W = N_HEADS * HEAD_W
N_SEG = 7
DA_DK = 64
ROT_DIM = DA_DK // 4
ROPE_THETA = 500000.0
PAGE_SIZE = 128
N_GROUPS = 4
EXPERTS_PER_GROUP = 4
N_EXPERTS = N_GROUPS * EXPERTS_PER_GROUP
PAIR_LO = (0, 0, 0, 1, 1, 2)
PAIR_HI = (1, 2, 3, 2, 3, 3)
EXPERT_FF = D_MODEL // 2
NORM_EPS = 1e-6
NEG = -1e30
F_MIN = 1e-30

HG_CHUNK = 64
LOG2E = math.log2(math.e)
SAMPLE_PAD = 8
GATE_LANE0 = N_GROUPS
ROUTER_W = 128
ROW_CHUNKS = (D_MODEL + ROUTER_W) // HEAD_W

VMEM_LIMIT = 48 * 1024 * 1024


def _dot(a, b):
    return jnp.dot(a, b, preferred_element_type=f32)


def _dot_nt(a, b):
    return lax.dot_general(a, b, (((1,), (1,)), ((), ())), preferred_element_type=f32)


def _dot_tn(a, b):
    return lax.dot_general(a, b, (((0,), (0,)), ((), ())), preferred_element_type=f32)


def _rms(x, w):
    return x * lax.rsqrt(jnp.mean(x * x, axis=-1, keepdims=True) + NORM_EPS) * w


def _split3(x):
    hi = x.astype(bf16)
    r1 = x - hi.astype(f32)
    mid = r1.astype(bf16)
    lo = (r1 - mid.astype(f32)).astype(bf16)
    return hi, mid, lo


def _proj_kernel(h_ref, nw_ref, w_ref, lb_ref, c_ref, sa_ref, sb_ref, *rest, q_scale, transposed, layer, first):
    qh_ref, kh_ref, ih_ref, lf_ref, gh_ref, qd_ref, kd_ref, vd_ref, kb_ref, vb_ref = rest[-10:]
    if first:
        for d in range(kd_ref.shape[0]):
            if d != layer:
                kd_ref[d] = jnp.zeros(kd_ref.shape[1:], f32)
                vd_ref[d] = jnp.zeros(vd_ref.shape[1:], f32)
        kd_ref, vd_ref = kd_ref.at[layer], vd_ref.at[layer]
    tm = h_ref.shape[0]
    xn = _rms(h_ref[...], nw_ref[...]).astype(bf16)

    def seg(i):
        return _dot(xn, w_ref[:, i * SEG_W:(i + 1) * SEG_W])

    qh_ref[...] = seg(0) * (HEAD_W ** -0.5)
    hf = seg(1)
    lb = lb_ref[...]
    e = jnp.exp(-jnp.abs(hf))
    r = 1.0 / (1.0 + e)
    pos = hf >= 0.0
    sig = jnp.where(pos, r, e * r)
    nsig = jnp.where(pos, e * r, r)
    f = lb + (1.0 - lb) * sig
    lf_ref[...] = jnp.log(jnp.maximum(f, F_MIN)) * LOG2E
    kh_ref[...] = (1.0 - lb) * nsig
    ih_ref[...] = seg(2)
    gh_ref[...] = seg(3)

    c = c_ref[...]
    sa = sa_ref[...]
    sb = sb_ref[...]

    def rope(z, hh):
        zz = z[:, hh * HEAD_W:(hh + 1) * HEAD_W]
        return zz * c + pltpu.roll(zz, HEAD_W - ROT_DIM // 2, 1) * sa + pltpu.roll(zz, ROT_DIM // 2, 1) * sb

    zq = seg(4)
    zk = seg(5)
    for hh in range(N_HEADS):
        sl = slice(hh * HEAD_W, (hh + 1) * HEAD_W)
        qr = rope(zq, hh) * q_scale
        if transposed:
            qd_ref[sl, :] = qr.T.astype(bf16)
        else:
            qd_ref[:, sl] = qr.astype(bf16)
        kr = rope(zk, hh)
        kd_ref[pl.ds(hh, tm, stride=N_HEADS), :] = kr
        kb_ref[:, sl] = kr.astype(bf16)
    vd = seg(6)
    for hh in range(N_HEADS):
        vd_ref[pl.ds(hh, tm, stride=N_HEADS), :] = vd[:, hh * HEAD_W:(hh + 1) * HEAD_W]
    vb_ref[...] = vd.T.astype(bf16) if transposed else vd.astype(bf16)


def _proj(h, n, nw, w_bf, lb, tabs, tm, q_scale, depth, layer, kv_prev, seq=None):
    first = kv_prev is None
    npos = tabs[0].shape[0] // tm
    row = lambda i: (i, 0)
    fixed = lambda i: (0, 0)
    tab = lambda i: (i % npos, 0)
    seg_f32 = jax.ShapeDtypeStruct((n, SEG_W), f32)
    seg_bf = jax.ShapeDtypeStruct((n, SEG_W), bf16)
    seg_spec = pl.BlockSpec((tm, SEG_W), row)
    rows_shape = jax.ShapeDtypeStruct((depth, n * N_HEADS, HEAD_W), f32)
    if first:
        rows_spec = pl.BlockSpec((depth, tm * N_HEADS, HEAD_W), lambda i: (0, i, 0))
        extra_specs, extra_args, aliases = [], (), {}
    else:
        rows_spec = pl.BlockSpec((None, tm * N_HEADS, HEAD_W), lambda i: (layer, i, 0))
        extra_specs, extra_args = [pl.BlockSpec(memory_space=pl.ANY)] * 2, tuple(kv_prev)
        aliases = {7: 6, 8: 7}
    if seq is None:
        t_shape, t_spec = seg_bf, seg_spec
    else:
        nb = seq // tm
        t_shape = jax.ShapeDtypeStruct((n // seq, SEG_W, seq), bf16)
        t_spec = pl.BlockSpec((None, SEG_W, tm), lambda i: (i // nb, 0, i % nb))
    return pl.pallas_call(
        functools.partial(_proj_kernel, q_scale=q_scale, transposed=seq is not None, layer=layer, first=first),
        grid=(n // tm,),
        in_specs=[pl.BlockSpec((tm, D_MODEL), row),
                  pl.BlockSpec((1, D_MODEL), fixed),
                  pl.BlockSpec((D_MODEL, N_SEG * SEG_W), fixed),
                  pl.BlockSpec((1, SEG_W), fixed),
                  pl.BlockSpec((tm, HEAD_W), tab),
                  pl.BlockSpec((tm, HEAD_W), tab),
                  pl.BlockSpec((tm, HEAD_W), tab)] + extra_specs,
        out_specs=[seg_spec] * 5 + [t_spec, rows_spec, rows_spec, seg_spec, t_spec],
        out_shape=[seg_f32] * 5 + [t_shape, rows_shape, rows_shape, seg_bf, t_shape],
        input_output_aliases=aliases,
        compiler_params=pltpu.CompilerParams(dimension_semantics=("arbitrary",), vmem_limit_bytes=VMEM_LIMIT),
        name="proj",
    )(h, nw, w_bf, lb, *tabs, *extra_args)


def _rope_tables(pos):
    half = ROT_DIM // 2
    inv = ROPE_THETA ** (-jnp.arange(0, ROT_DIM, 2, dtype=f32) / ROT_DIM)
    ang = pos.astype(f32)[:, None] * inv[None, :]
    cos, sin = jnp.cos(ang), jnp.sin(ang)
    t = pos.shape[0]
    rest = DA_DK - ROT_DIM
    c64 = jnp.concatenate([cos, cos, jnp.ones((t, rest), f32)], axis=1)
    sa64 = jnp.concatenate([-sin, jnp.zeros((t, half + rest), f32)], axis=1)
    sb64 = jnp.concatenate([jnp.zeros((t, half), f32), sin, jnp.zeros((t, rest), f32)], axis=1)
    return tuple(jnp.tile(a, (1, HEAD_W // DA_DK)) for a in (c64, sa64, sb64))


def _hgrn_chunk(q, k, v, lf, st, chunk, consts):
    sel, pair_masks, halves = consts
    heads = [slice(hh * HEAD_W, (hh + 1) * HEAD_W) for hh in range(N_HEADS)]
    terms = jnp.concatenate(_split3(lf), axis=1)
    gs3 = _dot(sel, terms)
    gs = gs3[:, :SEG_W] + gs3[:, SEG_W:2 * SEG_W] + gs3[:, 2 * SEG_W:]
    g = gs[:chunk]
    g_last = g[chunk - 1:chunk, :]
    qx = (q * jnp.exp2(g)).astype(bf16)
    q_bf, k_bf, v_bf = q.astype(bf16), k.astype(bf16), v.astype(bf16)
    st_bf = [s.astype(bf16) for s in st]
    o = [_dot_nt(qx[:, sl], st_bf[hh]) for hh, sl in enumerate(heads)]
    a = [_dot_nt(q_bf[:, sl], k_bf[:, sl]) * pair_masks[0] for sl in heads]
    n_mm = 1
    for lvl, h in enumerate(halves, start=1):
        if h % 8 == 0:
            ref = jnp.concatenate([jnp.broadcast_to(g[r0 + h - 1:r0 + h, :], (2 * h, SEG_W))
                                   for r0 in range(0, chunk, 2 * h)], axis=0)
        else:
            ref = gs[n_mm * chunk:(n_mm + 1) * chunk]
            n_mm += 1
        e = jnp.exp2(-jnp.abs(g - ref))
        qe, ke = (q * e).astype(bf16), (k * e).astype(bf16)
        a = [a[hh] + _dot_nt(qe[:, sl], ke[:, sl]) * pair_masks[lvl] for hh, sl in enumerate(heads)]
    kdec = (k * jnp.exp2(g_last - g)).astype(bf16)
    decay = jnp.exp2(g_last)
    o = jnp.concatenate([o[hh] + _dot(a[hh].astype(bf16), v_bf[:, sl]) for hh, sl in enumerate(heads)], axis=1)
    st_new = [st[hh] * decay[:, sl] + _dot_tn(v_bf[:, sl], kdec[:, sl]) for hh, sl in enumerate(heads)]
    return o, st_new


def _hgrn_consts(chunk):
    r = lax.broadcasted_iota(jnp.int32, (chunk, chunk), 0)
    c = lax.broadcasted_iota(jnp.int32, (chunk, chunk), 1)
    sels = [jnp.where(c <= r, 1.0, 0.0)]
    masks = [jnp.where(c == r, 1.0, 0.0)]
    halves = []
    h = chunk // 2
    while h >= 1:
        blk = -(2 * h)
        if h % 8:
            sels.append(jnp.where(c <= (r & blk) + (h - 1), 1.0, 0.0))
        same_block = (r & blk) == (c & blk)
        masks.append(jnp.where(same_block & ((r & h) != 0) & ((c & h) == 0), 1.0, 0.0))
        halves.append(h)
        h //= 2
    return jnp.concatenate(sels, axis=0).astype(bf16), masks, halves


def _hgrn_kernel(q_ref, k_ref, v_ref, lf_ref, s0_ref, *rest, chunk, n_chunks, layer, first):
    o_ref, sfin_ref, st_ref = rest[-3:]
    j = pl.program_id(1)
    consts = _hgrn_consts(chunk)
    if first:
        sfin_all, sfin_ref = sfin_ref, sfin_ref.at[layer]

    @pl.when(j == 0)
    def _():
        for hh in range(N_HEADS):
            st_ref[hh] = s0_ref[hh].T

    def body(c, carry):
        rows = pl.ds(pl.multiple_of(c * chunk, chunk), chunk)
        o, st_new = _hgrn_chunk(q_ref[rows, :], k_ref[rows, :], v_ref[rows, :], lf_ref[rows, :],
                                [st_ref[hh] for hh in range(N_HEADS)], chunk, consts)
        o_ref[rows, :] = o
        for hh in range(N_HEADS):
            st_ref[hh] = st_new[hh]
        return carry

    lax.fori_loop(0, n_chunks, body, 0, unroll=4 if n_chunks % 4 == 0 else 1)

    @pl.when(j == pl.num_programs(1) - 1)
    def _():
        for hh in range(N_HEADS):
            sfin_ref[hh] = st_ref[hh].T
        if first:
            for d in range(sfin_all.shape[0]):
                if d != layer:
                    sfin_all[d] = jnp.zeros(sfin_all.shape[1:], f32)


def _hgrn(q, k, v, lf, s0, s0_layer, t, tb, chunk, depth, layer, sfin_prev):
    n = q.shape[0]
    b = n // t
    nj = t // tb
    first = sfin_prev is None
    tok = pl.BlockSpec((tb, SEG_W), lambda bi, j: (bi * nj + j, 0))
    st_in = pl.BlockSpec((None, None, N_HEADS, HEAD_W, HEAD_W), lambda bi, j: (s0_layer, bi, 0, 0, 0))
    if first:
        st_out = pl.BlockSpec((depth, None, N_HEADS, HEAD_W, HEAD_W), lambda bi, j: (0, bi, 0, 0, 0))
        extra_specs, extra_args, aliases = [], (), {}
    else:
        st_out = pl.BlockSpec((None, None, N_HEADS, HEAD_W, HEAD_W), lambda bi, j: (layer, bi, 0, 0, 0))
        extra_specs, extra_args, aliases = [pl.BlockSpec(memory_space=pl.ANY)], (sfin_prev,), {5: 1}
    return pl.pallas_call(
        functools.partial(_hgrn_kernel, chunk=chunk, n_chunks=tb // chunk, layer=layer, first=first),
        grid=(b, nj),
        in_specs=[tok, tok, tok, tok, st_in] + extra_specs,
        out_specs=[tok, st_out],
        out_shape=[jax.ShapeDtypeStruct((n, SEG_W), f32),
                   jax.ShapeDtypeStruct((depth, b, N_HEADS, HEAD_W, HEAD_W), f32)],
        input_output_aliases=aliases,
        scratch_shapes=[pltpu.VMEM((N_HEADS, HEAD_W, HEAD_W), f32)],
        compiler_params=pltpu.CompilerParams(dimension_semantics=("arbitrary", "arbitrary"),
                                             vmem_limit_bytes=VMEM_LIMIT),
        name="hgrn",
    )(q, k, v, lf, s0, *extra_args)


def _stack_maps(q):
    lane = lax.broadcasted_iota(jnp.int32, q.shape, 1)
    zero = jnp.zeros_like(q)
    return jnp.concatenate([jnp.where(lane < DA_DK, q, zero), jnp.where(lane >= DA_DK, q, zero)], axis=0)


def _softmax_step(s, v_bf, m_ref, l_ref, acc_ref, rows=None):
    sl = slice(None) if rows is None else rows
    m_prev = m_ref[sl, :]
    m_new = jnp.maximum(m_prev, jnp.max(s, axis=-1, keepdims=True))
    alpha = jnp.exp2(m_prev - m_new)
    p = jnp.exp2(s - m_new)
    l_ref[sl, :] = alpha * l_ref[sl, :] + jnp.sum(p, axis=-1, keepdims=True)
    acc_ref[sl, :] = alpha * acc_ref[sl, :] + _dot(p.astype(bf16), v_bf)
    m_ref[sl, :] = m_new


def _attn_prompt_step(i, lam_ref, qt_ref, k_ref, vt_ref, o_ref, m_ref, l_ref, acc_ref, s_ref, tq, n_q):
    qt = qt_ref[...]
    sub = lax.broadcasted_iota(jnp.int32, qt.shape, 0)
    zero = jnp.zeros_like(qt)
    qs = jnp.concatenate([jnp.where(sub < DA_DK, qt, zero), jnp.where(sub >= DA_DK, qt, zero)], axis=1)
    m_ref[...] = jnp.full(m_ref.shape, NEG, f32)
    l_ref[...] = jnp.zeros(l_ref.shape, f32)
    acc_ref[...] = jnp.zeros(acc_ref.shape, f32)

    def scores(j):
        return _dot(k_ref[j * tq:(j + 1) * tq, :], qs)

    def update(s, vt):
        m_prev = m_ref[...]
        m_new = jnp.maximum(m_prev, jnp.max(s, axis=0, keepdims=True))
        alpha = jnp.exp2(m_prev - m_new)
        p = jnp.exp2(s - m_new)
        l_ref[...] = alpha * l_ref[...] + jnp.sum(p, axis=0, keepdims=True)
        acc_ref[...] = alpha * acc_ref[...] + _dot(vt, p.astype(bf16))
        m_ref[...] = m_new

    s_ref[0] = scores(0)
    for j in range(n_q - 1):
        @pl.when(j < i)
        def _():
            s_ref[(j + 1) % 2] = scores(j + 1)
            update(s_ref[j % 2], vt_ref[:, j * tq:(j + 1) * tq])

    s = s_ref[i % 2]
    key = lax.broadcasted_iota(jnp.int32, s.shape, 0)
    qry = lax.broadcasted_iota(jnp.int32, s.shape, 1) & (tq - 1)
    update(jnp.where(key <= qry, s, NEG), vt_ref[:, pl.ds(pl.multiple_of(i * tq, tq), tq)])
    o = acc_ref[...] / l_ref[...]
    o_ref[...] = (o[:, :tq] - lam_ref[0] * o[:, tq:]).T


def _attn_sample_step(j, n_j, lam_ref, qs_ref, kn_ref, vn_ref, k_refs, v_refs, o_ref, m_ref, l_ref, acc_ref):
    t = SAMPLE_PAD

    @pl.when(j == 0)
    def _():
        m_ref[...] = jnp.full(m_ref.shape, NEG, f32)
        l_ref[...] = jnp.zeros(l_ref.shape, f32)
        acc_ref[...] = jnp.zeros(acc_ref.shape, f32)

    head_rows = [slice(hh * 2 * t, (hh + 1) * 2 * t) for hh in range(N_HEADS)]
    head_toks = [pl.ds(hh, PAGE_SIZE, stride=N_HEADS) for hh in range(N_HEADS)]
    qs = qs_ref[...]
    s = [_dot_nt(qs[rows, :], jnp.concatenate([kr[toks, :].astype(bf16) for kr in k_refs], axis=0))
         for rows, toks in zip(head_rows, head_toks)]
    m_prev = m_ref[...]
    m_new = jnp.maximum(m_prev, jnp.concatenate([jnp.max(sh, axis=-1, keepdims=True) for sh in s], axis=0))
    alpha = jnp.exp2(m_prev - m_new)
    p = [jnp.exp2(sh - m_new[rows, :]) for sh, rows in zip(s, head_rows)]
    l_ref[...] = alpha * l_ref[...] + jnp.concatenate([jnp.sum(ph, axis=-1, keepdims=True) for ph in p], axis=0)
    pv = [_dot(ph.astype(bf16), jnp.concatenate([vr[toks, :].astype(bf16) for vr in v_refs], axis=0))
          for ph, toks in zip(p, head_toks)]
    acc_ref[...] = alpha * acc_ref[...] + jnp.concatenate(pv, axis=0)
    m_ref[...] = m_new

    @pl.when(j == n_j - 1)
    def _():
        for hh in range(N_HEADS):
            rows = slice(hh * 2 * t, (hh + 1) * 2 * t)
            sl = slice(hh * HEAD_W, (hh + 1) * HEAD_W)
            s = _dot_nt(qs_ref[rows, :], kn_ref[:, sl].astype(bf16))
            row = lax.broadcasted_iota(jnp.int32, s.shape, 0) & (t - 1)
            col = lax.broadcasted_iota(jnp.int32, s.shape, 1)
            s = jnp.where(col <= row, s, NEG)
            _softmax_step(s, vn_ref[:, sl].astype(bf16), m_ref, l_ref, acc_ref, rows)
            o = acc_ref[rows, :] / l_ref[rows, :]
            o_ref[:, sl] = o[:t] - lam_ref[0] * o[t:]


def _attn_kernel(pt_ref, lam_ref, qt_ref, k_ref, vt_ref, qs_ref, kn_ref, vn_ref, *rest,
                 tq, n_q, n_pg, n_j, steps_p, steps_s):
    del pt_ref
    k_pages, v_pages = rest[:n_pg], rest[n_pg:2 * n_pg]
    op_ref, os_ref, pm_ref, pl_ref, pacc_ref, ps_ref, sm_ref, sl_ref, sacc_ref = rest[2 * n_pg:]
    step = pl.program_id(0)

    def prompt():
        _attn_prompt_step(step % n_q, lam_ref, qt_ref, k_ref, vt_ref, op_ref, pm_ref, pl_ref, pacc_ref, ps_ref,
                          tq, n_q)

    def sample():
        _attn_sample_step(step % n_j, n_j, lam_ref, qs_ref, kn_ref, vn_ref, k_pages, v_pages, os_ref,
                          sm_ref, sl_ref, sacc_ref)

    if steps_p == steps_s:
        prompt()
        sample()
    else:
        pl.when(step < steps_p)(prompt)
        pl.when(step < steps_s)(sample)


def _attn(page_table, lam, qt_bf, k_bf, vt_bf, qs, kn, vn, cache_k, cache_v, layer, b, s, tq, n_pg):
    bd, n_pages = page_table.shape
    t = SAMPLE_PAD
    n_q, n_j = s // tq, n_pages // n_pg
    steps_p, steps_s = b * N_HEADS * n_q, bd * n_j
    k3 = k_bf.reshape(b, s, SEG_W)
    cache_k = cache_k.reshape(cache_k.shape[:2] + (PAGE_SIZE * N_HEADS, HEAD_W))
    cache_v = cache_v.reshape(cache_v.shape[:2] + (PAGE_SIZE * N_HEADS, HEAD_W))

    def p_idx(step):
        lin = jnp.minimum(step, steps_p - 1)
        return lin // (N_HEADS * n_q), (lin // n_q) % N_HEADS, lin % n_q

    def s_idx(step):
        lin = jnp.minimum(step, steps_s - 1)
        return lin // n_j, lin % n_j

    def on_p(f):
        return lambda step, pt: f(*p_idx(step))

    def per_seq(shape):
        return pl.BlockSpec((None,) + shape, lambda step, pt: (s_idx(step)[0], 0, 0))

    def page_spec(p):
        def index(step, pt):
            sb, j = s_idx(step)
            return layer, pt[sb * n_pages + j * n_pg + p], 0, 0
        return pl.BlockSpec((None, None, PAGE_SIZE * N_HEADS, HEAD_W), index)

    pages = [page_spec(p) for p in range(n_pg)]
    out_p, out_s = pl.pallas_call(
        functools.partial(_attn_kernel, tq=tq, n_q=n_q, n_pg=n_pg, n_j=n_j, steps_p=steps_p, steps_s=steps_s),
        grid_spec=pltpu.PrefetchScalarGridSpec(
            num_scalar_prefetch=1,
            grid=(max(steps_p, steps_s),),
            in_specs=[pl.BlockSpec(memory_space=pltpu.SMEM),
                      pl.BlockSpec((None, HEAD_W, tq), on_p(lambda bi, h, i: (bi, h, i))),
                      pl.BlockSpec((None, s, HEAD_W), on_p(lambda bi, h, i: (bi, 0, h))),
                      pl.BlockSpec((None, HEAD_W, s), on_p(lambda bi, h, i: (bi, h, 0))),
                      per_seq((2 * t * N_HEADS, HEAD_W)), per_seq((t, SEG_W)), per_seq((t, SEG_W))] + pages + pages,
            out_specs=[pl.BlockSpec((None, tq, HEAD_W), on_p(lambda bi, h, i: (bi, i, h))), per_seq((t, SEG_W))],
            scratch_shapes=[pltpu.VMEM((1, 2 * tq), f32), pltpu.VMEM((1, 2 * tq), f32),
                            pltpu.VMEM((HEAD_W, 2 * tq), f32), pltpu.VMEM((2, tq, 2 * tq), f32),
                            pltpu.VMEM((2 * t * N_HEADS, 1), f32), pltpu.VMEM((2 * t * N_HEADS, 1), f32),
                            pltpu.VMEM((2 * t * N_HEADS, HEAD_W), f32)]),
        out_shape=[jax.ShapeDtypeStruct((b, s, SEG_W), f32), jax.ShapeDtypeStruct((bd, t, SEG_W), f32)],
        compiler_params=pltpu.CompilerParams(dimension_semantics=("arbitrary",), vmem_limit_bytes=VMEM_LIMIT),
        name="attn",
    )(page_table.reshape(-1), lam, qt_bf, k3, vt_bf, qs, kn, vn, *([cache_k] * n_pg), *([cache_v] * n_pg))
    return out_p.reshape(b * s, SEG_W), out_s


def _route(logits):
    lane = lax.broadcasted_iota(jnp.int32, logits.shape, 1).astype(f32)
    big = float(ROUTER_W)
    is_g = lane < N_GROUPS
    m1 = jnp.max(jnp.where(is_g, logits, -jnp.inf), axis=-1, keepdims=True)
    grp = jnp.min(jnp.where(is_g & (logits == m1), lane, big), axis=-1, keepdims=True)
    p_grp = 1.0 / jnp.sum(jnp.where(is_g, jnp.exp(logits - m1), 0.0), axis=-1, keepdims=True)
    lo = GATE_LANE0 + EXPERTS_PER_GROUP * grp
    in_g = (lane >= lo) & (lane < lo + EXPERTS_PER_GROUP)
    v1 = jnp.max(jnp.where(in_g, logits, -jnp.inf), axis=-1, keepdims=True)
    i1 = jnp.min(jnp.where(in_g & (logits == v1), lane, big), axis=-1, keepdims=True)
    rest = in_g & (lane != i1)
    v2 = jnp.max(jnp.where(rest, logits, -jnp.inf), axis=-1, keepdims=True)
    i2 = jnp.min(jnp.where(rest & (logits == v2), lane, big), axis=-1, keepdims=True)
    e = jnp.exp(v2 - v1)
    w1 = 1.0 / (1.0 + e)
    w2 = e / (1.0 + e)
    gates = jnp.where(lane == i1, p_grp * w1, 0.0) + jnp.where(lane == i2, p_grp * w2, 0.0)
    a = jnp.minimum(i1, i2) - lo
    b = jnp.maximum(i1, i2) - lo
    pair = jnp.where(a == 0.0, b - 1.0, jnp.where(a == 1.0, b + 1.0, 5.0))
    return jnp.where(lane == 0.0, grp, jnp.where(lane == 1.0, grp * float(len(PAIR_LO)) + pair, gates))


def _mix_kernel(oh_ref, od_ref, gh_ref, h_ref, hnw_ref, dnw_ref, wo_ref, fnw_ref, wrh_ref, wrl_ref, br_ref,
                hx_ref, rt_ref, *, od_scale):
    parts = []
    for hh in range(N_HEADS):
        sl = slice(hh * HEAD_W, (hh + 1) * HEAD_W)
        gate = 1.0 / (1.0 + jnp.exp(-gh_ref[:, sl]))
        parts.append((_rms(oh_ref[:, sl], hnw_ref[...]) * gate).astype(bf16))
    for hh in range(N_HEADS):
        sl = slice(hh * HEAD_W, (hh + 1) * HEAD_W)
        parts.append((_rms(od_ref[:, sl], dnw_ref[...]) * od_scale).astype(bf16))
    h2 = h_ref[...] + _dot(jnp.concatenate(parts, axis=1), wo_ref[...])
    xn = _rms(h2, fnw_ref[...])
    x_hi = xn.astype(bf16)
    x_lo = (xn - x_hi.astype(f32)).astype(bf16)
    logits = _dot(x_hi, wrh_ref[...]) + _dot(x_lo, wrh_ref[...]) + _dot(x_hi, wrl_ref[...]) + br_ref[...]
    tm = h2.shape[0]
    for c in range(D_MODEL // HEAD_W):
        hx_ref[pl.ds(c, tm, stride=ROW_CHUNKS), :] = h2[:, c * HEAD_W:(c + 1) * HEAD_W]
    router = _route(logits)
    hx_ref[pl.ds(D_MODEL // HEAD_W, tm, stride=ROW_CHUNKS), :] = router
    rt_ref[...] = router


def _mix(o_h, o_d, g_h, h, hnw, dnw, wo_bf, fnw, wr_hi, wr_lo, br, od_scale, tm):
    n = o_h.shape[0]
    row = lambda i: (i, 0)
    fixed = lambda i: (0, 0)
    seg = pl.BlockSpec((tm, SEG_W), row)
    return pl.pallas_call(
        functools.partial(_mix_kernel, od_scale=od_scale),
        grid=(n // tm,),
        in_specs=[seg, seg, seg, pl.BlockSpec((tm, D_MODEL), row),
                  pl.BlockSpec((1, HEAD_W), fixed), pl.BlockSpec((1, HEAD_W), fixed),
                  pl.BlockSpec((D_MODEL, D_MODEL), fixed), pl.BlockSpec((1, D_MODEL), fixed),
                  pl.BlockSpec((D_MODEL, ROUTER_W), fixed), pl.BlockSpec((D_MODEL, ROUTER_W), fixed),
                  pl.BlockSpec((1, ROUTER_W), fixed)],
        out_specs=[pl.BlockSpec((tm * ROW_CHUNKS, HEAD_W), row), pl.BlockSpec((tm, ROUTER_W), row)],
        out_shape=[jax.ShapeDtypeStruct((n * ROW_CHUNKS, HEAD_W), f32), jax.ShapeDtypeStruct((n, ROUTER_W), f32)],
        compiler_params=pltpu.CompilerParams(dimension_semantics=("arbitrary",), vmem_limit_bytes=VMEM_LIMIT),
        name="mix",
    )(o_h, o_d, g_h, h, hnw, dnw, wo_bf, fnw, wr_hi, wr_lo, br)


def _moe_kernel(pos_ref, exp_ref, cnt_ref, hx_hbm, fnw_ref, onw_ref, *rest, tm, n_e, final_norm):
    wg_refs, wu_refs, wd_refs = rest[:n_e], rest[n_e:2 * n_e], rest[2 * n_e:3 * n_e]
    out_hbm, xbuf, ybuf, ids_ref, gsem, ssem = rest[3 * n_e:]
    i = pl.program_id(0)
    n_tiles = pl.num_programs(0)
    slot = i % 2

    def gather_copy(tile, sl, r):
        tok = ids_ref[tile * tm + r]
        return pltpu.make_async_copy(hx_hbm.at[pl.ds(tok * ROW_CHUNKS, ROW_CHUNKS)],
                                     xbuf.at[sl, pl.ds(r * ROW_CHUNKS, ROW_CHUNKS)], gsem.at[sl])

    def scatter_copy(tile, sl, r):
        tok = ids_ref[tile * tm + r]
        return pltpu.make_async_copy(ybuf.at[sl, pl.ds(r, 1)], out_hbm.at[pl.ds(tok, 1)], ssem.at[sl])

    @pl.when(i == 0)
    def _():
        def place(t, carry):
            ids_ref[pos_ref[t]] = t
            return carry
        lax.fori_loop(0, pos_ref.shape[0], place, 0, unroll=8)

    def start_rows(n, copy):
        def pair(k, carry):
            copy(2 * k).start(priority=0)
            copy(2 * k + 1).start(priority=1)
            return carry
        if isinstance(n, int):
            assert n % 2 == 0
            lax.fori_loop(0, n // 2, pair, 0, unroll=4)
        else:
            lax.fori_loop(0, n // 2, pair, 0)

            @pl.when(n % 2 == 1)
            def _():
                copy(n - 1).start(priority=0)

    def wait_rows(n, copy):
        def body(r, carry):
            copy(r).wait()
            return carry
        lax.fori_loop(0, n, body, 0)

    def start_tile(tile, copy):
        cnt = cnt_ref[tile]

        @pl.when(cnt == tm)
        def _():
            start_rows(tm, copy)

        @pl.when(cnt < tm)
        def _():
            start_rows(cnt, copy)

    def wait_tile(tile, copy, whole):
        cnt = cnt_ref[tile]

        @pl.when(cnt == tm)
        def _():
            whole.wait()

        @pl.when(cnt < tm)
        def _():
            wait_rows(cnt, copy)

    def gather_start(tile, sl):
        start_tile(tile, lambda r: gather_copy(tile, sl, r))

    def gather_wait(tile, sl):
        wait_tile(tile, lambda r: gather_copy(tile, sl, r),
                  pltpu.make_async_copy(hx_hbm.at[pl.ds(0, tm * ROW_CHUNKS)], xbuf.at[sl], gsem.at[sl]))

    def scatter_start(tile, sl):
        start_tile(tile, lambda r: scatter_copy(tile, sl, r))

    def scatter_wait(tile, sl):
        wait_tile(tile, lambda r: scatter_copy(tile, sl, r),
                  pltpu.make_async_copy(ybuf.at[sl], out_hbm.at[pl.ds(0, tm)], ssem.at[sl]))

    @pl.when(i == 0)
    def _():
        xbuf[...] = jnp.zeros(xbuf.shape, f32)
        gather_start(0, 0)

    @pl.when(i + 1 < n_tiles)
    def _():
        gather_start(i + 1, 1 - slot)

    gather_wait(i, slot)

    @pl.when(i >= 2)
    def _():
        scatter_wait(i - 2, slot)

    @pl.when(cnt_ref[i] > 0)
    def _():
        xs = xbuf.at[slot]
        chunk = lambda c: xs[pl.ds(c, tm, stride=ROW_CHUNKS), :]
        h2 = jnp.concatenate([chunk(c) for c in range(D_MODEL // HEAD_W)], axis=1)
        router = chunk(D_MODEL // HEAD_W)
        xn = _rms(h2, fnw_ref[...]).astype(bf16)
        lane = lax.broadcasted_iota(jnp.int32, router.shape, 1)
        acc = h2
        for e in range(n_e):
            gate_lane = GATE_LANE0 + exp_ref[i * n_e + e]
            gate = jnp.sum(jnp.where(lane == gate_lane, router, 0.0), axis=-1, keepdims=True)
            a = _dot(xn, wg_refs[e][...])
            hid = a / (1.0 + jnp.exp(-a)) * _dot(xn, wu_refs[e][...])
            acc = acc + _dot((hid * gate).astype(bf16), wd_refs[e][...])
        if final_norm:
            acc = _rms(acc, onw_ref[...])
        ybuf[slot] = acc

    scatter_start(i, slot)

    @pl.when(i == n_tiles - 1)
    def _():
        @pl.when(i >= 1)
        def _():
            scatter_wait(i - 1, 1 - slot)
        scatter_wait(i, slot)


def _dispatch(cls_f32, n_cls, tm):
    n = cls_f32.shape[0]
    n_tiles = n // tm + n_cls
    grp = cls_f32.astype(jnp.int32)
    onehot = (grp[:, None] == jnp.arange(n_cls, dtype=jnp.int32)[None, :]).astype(jnp.int32)
    csum = jnp.cumsum(onehot, axis=0)
    cnt = csum[-1]
    rank = jnp.sum(csum * onehot, axis=1) - 1
    tiles_g = (cnt + tm - 1) // tm
    tile_end = jnp.cumsum(tiles_g)
    tile_start = tile_end - tiles_g
    pos = jnp.sum(onehot * (tile_start * tm)[None, :], axis=1) + rank
    tile = jnp.arange(n_tiles, dtype=jnp.int32)
    tile_grp_raw = jnp.sum((tile[:, None] >= tile_end[None, :]).astype(jnp.int32), axis=1)
    last_grp = jnp.max(jnp.where(cnt > 0, jnp.arange(n_cls, dtype=jnp.int32), 0))
    tile_grp = jnp.minimum(tile_grp_raw, last_grp)
    in_range = tile_grp_raw < n_cls
    rows_left = cnt[tile_grp] - (tile - tile_start[tile_grp]) * tm
    tile_cnt = jnp.where(in_range, jnp.clip(rows_left, 0, tm), 0).astype(jnp.int32)
    return pos.astype(jnp.int32), tile_grp, tile_cnt


def _moe(hx, router, fnw, onw, wg_bf, wu_bf, wd_bf, layer, tm, final_norm, by_pair):
    n = router.shape[0]
    if by_pair:
        n_e = 2
        pos, tile_cls, tile_cnt = _dispatch(router[:, 1], N_GROUPS * len(PAIR_LO), tm)
        grp, pair = tile_cls // len(PAIR_LO), tile_cls % len(PAIR_LO)
        members = [jnp.array(PAIR_LO, jnp.int32)[pair], jnp.array(PAIR_HI, jnp.int32)[pair]]
    else:
        n_e = EXPERTS_PER_GROUP
        pos, grp, tile_cnt = _dispatch(router[:, 0], N_GROUPS, tm)
        members = [jnp.full_like(grp, e) for e in range(n_e)]
    tile_exp = jnp.stack([grp * EXPERTS_PER_GROUP + m for m in members], axis=1).reshape(-1).astype(jnp.int32)
    n_tiles = tile_cnt.shape[0]
    fixed = lambda i, ids_r, exp_r, cnt_r: (0, 0)

    def wspecs(shape):
        return [pl.BlockSpec((None, None) + shape, lambda i, ids_r, exp_r, cnt_r, e=e: (layer, exp_r[i * n_e + e], 0, 0))
                for e in range(n_e)]

    return pl.pallas_call(
        functools.partial(_moe_kernel, tm=tm, n_e=n_e, final_norm=final_norm),
        grid_spec=pltpu.PrefetchScalarGridSpec(
            num_scalar_prefetch=3,
            grid=(n_tiles,),
            in_specs=[pl.BlockSpec(memory_space=pl.ANY),
                      pl.BlockSpec((1, D_MODEL), fixed), pl.BlockSpec((1, D_MODEL), fixed)]
                     + wspecs((D_MODEL, EXPERT_FF)) + wspecs((D_MODEL, EXPERT_FF)) + wspecs((EXPERT_FF, D_MODEL)),
            out_specs=pl.BlockSpec(memory_space=pl.ANY),
            scratch_shapes=[pltpu.VMEM((2, tm * ROW_CHUNKS, HEAD_W), f32), pltpu.VMEM((2, tm, D_MODEL), f32),
                            pltpu.SMEM((n_tiles * tm,), jnp.int32),
                            pltpu.SemaphoreType.DMA((2,)), pltpu.SemaphoreType.DMA((2,))]),
        out_shape=jax.ShapeDtypeStruct((n, D_MODEL), f32),
        compiler_params=pltpu.CompilerParams(dimension_semantics=("arbitrary",), vmem_limit_bytes=VMEM_LIMIT),
        name="moe",
    )(pos, tile_exp, tile_cnt, hx, fnw, onw, *([wg_bf] * n_e), *([wu_bf] * n_e), *([wd_bf] * n_e))


def _stacked_sample_queries(qd):
    bd, t = qd.shape[:2]
    q5 = qd.reshape(bd, t, N_HEADS, 2, DA_DK)
    eye = jnp.eye(2, dtype=qd.dtype)
    qz = q5[:, :, :, :, None, :] * eye[None, None, None, :, :, None]
    return qz.transpose(0, 2, 3, 1, 4, 5).reshape(bd, N_HEADS * 2 * t, HEAD_W)


def kernel(x_prompt, x_sample, cache_k, cache_v, state_hgrn, page_table, attn_norm_w, w_in, hgrn_lb, hgrn_norm_w,
           diff_lambda, diff_norm_w, w_o, ffn_norm_w, w_r1, b_r1, w_r2, b_r2, w_gate, w_up, w_down, final_norm_w):
    b, s = x_prompt.shape[:2]
    bd, t = x_sample.shape[:2]
    depth = w_in.shape[0]
    n_pages = page_table.shape[1]
    past_len = n_pages * PAGE_SIZE
    assert x_prompt.shape[2] == D_MODEL and w_in.shape[2] == N_SEG * SEG_W and t <= SAMPLE_PAD
    tp = SAMPLE_PAD

    tabs_p = _rope_tables(jnp.arange(s, dtype=jnp.int32))
    tabs_s = _rope_tables(jnp.tile(past_len + jnp.arange(t, dtype=jnp.int32), bd))
    p_lb = jax.nn.softmax(hgrn_lb.astype(f32), axis=0)
    lb_all = jnp.cumsum(p_lb, axis=0) - p_lb[0:1]

    w_in_bf = w_in.astype(bf16)
    w_o_bf = w_o.astype(bf16)
    wg_bf, wu_bf, wd_bf = w_gate.astype(bf16), w_up.astype(bf16), w_down.astype(bf16)
    pad_r = ROUTER_W - N_GROUPS - N_EXPERTS
    w_r = jnp.concatenate([w_r1, w_r2, jnp.zeros((depth, D_MODEL, pad_r), f32)], axis=2)
    w_r_hi = w_r.astype(bf16)
    w_r_lo = (w_r - w_r_hi.astype(f32)).astype(bf16)
    b_r = jnp.concatenate([b_r1, b_r2, jnp.zeros((depth, pad_r), f32)], axis=1)

    tm_p = 256 if (b * s) % 256 == 0 else b * s
    tq = 512 if s % 512 == 0 else s
    tb_p = 512 if s % 512 == 0 else s
    tm_moe_p = 256 if (b * s) % 256 == 0 else b * s
    n_s = bd * t
    n_pg = next(p for p in (16, 8, 4, 2, 1) if n_pages % p == 0)
    tm_proj = 512 if s % 512 == 0 else tm_p
    q_scale = DA_DK ** -0.5 * LOG2E

    hp = x_prompt.reshape(b * s, D_MODEL)
    hs = x_sample.reshape(n_s, D_MODEL)
    zeros_state = jnp.zeros((1, b, N_HEADS, HEAD_W, HEAD_W), f32)
    state_s = state_hgrn.astype(f32)
    sfin_p = sfin_s = None
    kv_p = kv_s = None
    for l in range(depth):
        lam_init = 0.8 - 0.6 * math.exp(-0.3 * l)
        dl = diff_lambda[l].astype(f32)
        lam = (jnp.exp(jnp.sum(dl[0] * dl[1])) - jnp.exp(jnp.sum(dl[2] * dl[3])) + lam_init).reshape(1)
        lb = lb_all[l].reshape(1, SEG_W)
        nw = attn_norm_w[l].reshape(1, D_MODEL)
        fnw = ffn_norm_w[l].reshape(1, D_MODEL)
        onw = final_norm_w.reshape(1, D_MODEL)
        hnw = hgrn_norm_w[l].reshape(1, HEAD_W)
        dnw = diff_norm_w[l].reshape(1, HEAD_W)
        last = l == depth - 1

        def tail(o_h, o_d, g_h, h, tm_mix, tm_moe, by_pair):
            hx, router = _mix(o_h, o_d, g_h, h, hnw, dnw, w_o_bf[l], fnw, w_r_hi[l], w_r_lo[l],
                              b_r[l].reshape(1, ROUTER_W), 1.0 - lam_init, tm_mix)
            return _moe(hx, router, fnw, onw, wg_bf, wu_bf, wd_bf, l, tm_moe, last, by_pair)

        qh, kh, ih, lf, gh_p, qd_p, kd, vd, kb_p, vb_p = _proj(hp, b * s, nw, w_in_bf[l], lb, tabs_p, tm_proj,
                                                               q_scale, depth, l, kv_p, seq=s)
        kv_p = (kd, vd)
        oh_p, sfin_p = _hgrn(qh, kh, ih, lf, zeros_state, 0, s, tb_p, HG_CHUNK, depth, l, sfin_p)

        qh, kh, ih, lf, gh_s, qd, kd, vd, kb, vb = _proj(hs, n_s, nw, w_in_bf[l], lb, tabs_s, n_s, q_scale,
                                                         depth, l, kv_s)
        kv_s = (kd, vd)
        pad = lambda a: jnp.pad(a.reshape(bd, t, SEG_W), ((0, 0), (0, tp - t), (0, 0)))
        flat = lambda a: pad(a).reshape(bd * tp, SEG_W)
        oh_s, sfin_s = _hgrn(flat(qh), flat(kh), flat(ih), flat(lf), state_s, l, tp, tp, tp, depth, l, sfin_s)
        oh_s = oh_s.reshape(bd, tp, SEG_W)[:, :t].reshape(n_s, SEG_W)

        od_p, od_s = _attn(page_table, lam, qd_p, kb_p, vb_p, _stacked_sample_queries(pad(qd)), pad(kb), pad(vb),
                           cache_k, cache_v, l, b, s, tq, n_pg)
        hp = tail(oh_p, od_p, gh_p, hp, tm_p, tm_moe_p, True)
        hs = tail(oh_s, od_s[:, :t].reshape(n_s, SEG_W), gh_s, hs, n_s, n_s, False)

    rows_p = lambda a: a.reshape(depth, b, s, N_HEADS, HEAD_W)
    rows_s = lambda a: a.reshape(depth, bd, t, N_HEADS, HEAD_W)
    return (hp.reshape(b, s, D_MODEL), hs.reshape(bd, t, D_MODEL), rows_p(kv_p[0]), rows_p(kv_p[1]),
            sfin_p, rows_s(kv_s[0]), rows_s(kv_s[1]), sfin_s)
```

```python
import functools
import math

import jax
import jax.numpy as jnp
from jax import lax
from jax.experimental import pallas as pl
from jax.experimental.pallas import tpu as pltpu

f32 = jnp.float32
bf16 = jnp.bfloat16

D_MODEL = 1024
N_HEADS = 4
HEAD_W = 128
SEG_W = N_HEADS * HEAD_W
N_SEG = 7
DA_DK = 64
ROT_DIM = DA_DK // 4
ROPE_THETA = 500000.0
PAGE_SIZE = 128
N_GROUPS = 4
EXPERTS_PER_GROUP = 4
N_EXPERTS = N_GROUPS * EXPERTS_PER_GROUP
PAIR_LO = (0, 0, 0, 1, 1, 2)
PAIR_HI = (1, 2, 3, 2, 3, 3)
EXPERT_FF = D_MODEL // 2
NORM_EPS = 1e-6
NEG = -1e30
F_MIN = 1e-30

HG_CHUNK = 64
LOG2E = math.log2(math.e)
SAMPLE_PAD = 8
GATE_LANE0 = N_GROUPS
ROUTER_W = 128
ROW_CHUNKS = (D_MODEL + ROUTER_W) // HEAD_W

VMEM_LIMIT = 48 * 1024 * 1024


def _dot(a, b):
    return jnp.dot(a, b, preferred_element_type=f32)


def _dot_nt(a, b):
    return lax.dot_general(a, b, (((1,), (1,)), ((), ())), preferred_element_type=f32)


def _dot_tn(a, b):
    return lax.dot_general(a, b, (((0,), (0,)), ((), ())), preferred_element_type=f32)


def _rms(x, w):
    return x * lax.rsqrt(jnp.mean(x * x, axis=-1, keepdims=True) + NORM_EPS) * w


def _split3(x):
    hi = x.astype(bf16)
    r1 = x - hi.astype(f32)
    mid = r1.astype(bf16)
    lo = (r1 - mid.astype(f32)).astype(bf16)
    return hi, mid, lo


def _proj_kernel(h_ref, nw_ref, w_ref, lb_ref, c_ref, sa_ref, sb_ref, *rest, q_scale, transposed, layer, first):
    qh_ref, kh_ref, ih_ref, lf_ref, gh_ref, qd_ref, kd_ref, vd_ref, kb_ref, vb_ref = rest[-10:]
    if first:
        for d in range(kd_ref.shape[0]):
            if d != layer:
                kd_ref[d] = jnp.zeros(kd_ref.shape[1:], f32)
                vd_ref[d] = jnp.zeros(vd_ref.shape[1:], f32)
        kd_ref, vd_ref = kd_ref.at[layer], vd_ref.at[layer]
    tm = h_ref.shape[0]
    xn = _rms(h_ref[...], nw_ref[...]).astype(bf16)

    def seg(i):
        return _dot(xn, w_ref[:, i * SEG_W:(i + 1) * SEG_W])

    qh_ref[...] = seg(0) * (HEAD_W ** -0.5)
    hf = seg(1)
    lb = lb_ref[...]
    e = jnp.exp(-jnp.abs(hf))
    r = 1.0 / (1.0 + e)
    pos = hf >= 0.0
    sig = jnp.where(pos, r, e * r)
    nsig = jnp.where(pos, e * r, r)
    f = lb + (1.0 - lb) * sig
    lf_ref[...] = jnp.log(jnp.maximum(f, F_MIN)) * LOG2E
    kh_ref[...] = (1.0 - lb) * nsig
    ih_ref[...] = seg(2)
    gh_ref[...] = seg(3)

    c = c_ref[...]
    sa = sa_ref[...]
    sb = sb_ref[...]

    def rope(z, hh):
        zz = z[:, hh * HEAD_W:(hh + 1) * HEAD_W]
        return zz * c + pltpu.roll(zz, HEAD_W - ROT_DIM // 2, 1) * sa + pltpu.roll(zz, ROT_DIM // 2, 1) * sb

    zq = seg(4)
    zk = seg(5)
    for hh in range(N_HEADS):
        sl = slice(hh * HEAD_W, (hh + 1) * HEAD_W)
        qr = rope(zq, hh) * q_scale
        if transposed:
            qd_ref[sl, :] = qr.T.astype(bf16)
        else:
            qd_ref[:, sl] = qr.astype(bf16)
        kr = rope(zk, hh)
        kd_ref[pl.ds(hh, tm, stride=N_HEADS), :] = kr
        kb_ref[:, sl] = kr.astype(bf16)
    vd = seg(6)
    for hh in range(N_HEADS):
        vd_ref[pl.ds(hh, tm, stride=N_HEADS), :] = vd[:, hh * HEAD_W:(hh + 1) * HEAD_W]
    vb_ref[...] = vd.T.astype(bf16) if transposed else vd.astype(bf16)


def _proj(h, n, nw, w_bf, lb, tabs, tm, q_scale, depth, layer, kv_prev, seq=None):
    first = kv_prev is None
    npos = tabs[0].shape[0] // tm
    row = lambda i: (i, 0)
    fixed = lambda i: (0, 0)
    tab = lambda i: (i % npos, 0)
    seg_f32 = jax.ShapeDtypeStruct((n, SEG_W), f32)
    seg_bf = jax.ShapeDtypeStruct((n, SEG_W), bf16)
    seg_spec = pl.BlockSpec((tm, SEG_W), row)
    rows_shape = jax.ShapeDtypeStruct((depth, n * N_HEADS, HEAD_W), f32)
    if first:
        rows_spec = pl.BlockSpec((depth, tm * N_HEADS, HEAD_W), lambda i: (0, i, 0))
        extra_specs, extra_args, aliases = [], (), {}
    else:
        rows_spec = pl.BlockSpec((None, tm * N_HEADS, HEAD_W), lambda i: (layer, i, 0))
        extra_specs, extra_args = [pl.BlockSpec(memory_space=pl.ANY)] * 2, tuple(kv_prev)
        aliases = {7: 6, 8: 7}
    if seq is None:
        t_shape, t_spec = seg_bf, seg_spec
    else:
        nb = seq // tm
        t_shape = jax.ShapeDtypeStruct((n // seq, SEG_W, seq), bf16)
        t_spec = pl.BlockSpec((None, SEG_W, tm), lambda i: (i // nb, 0, i % nb))
    return pl.pallas_call(
        functools.partial(_proj_kernel, q_scale=q_scale, transposed=seq is not None, layer=layer, first=first),
        grid=(n // tm,),
        in_specs=[pl.BlockSpec((tm, D_MODEL), row),
                  pl.BlockSpec((1, D_MODEL), fixed),
                  pl.BlockSpec((D_MODEL, N_SEG * SEG_W), fixed),
                  pl.BlockSpec((1, SEG_W), fixed),
                  pl.BlockSpec((tm, HEAD_W), tab),
                  pl.BlockSpec((tm, HEAD_W), tab),
                  pl.BlockSpec((tm, HEAD_W), tab)] + extra_specs,
        out_specs=[seg_spec] * 5 + [t_spec, rows_spec, rows_spec, seg_spec, t_spec],
        out_shape=[seg_f32] * 5 + [t_shape, rows_shape, rows_shape, seg_bf, t_shape],
        input_output_aliases=aliases,
        compiler_params=pltpu.CompilerParams(dimension_semantics=("arbitrary",), vmem_limit_bytes=VMEM_LIMIT),
        name="proj",
    )(h, nw, w_bf, lb, *tabs, *extra_args)


def _rope_tables(pos):
    half = ROT_DIM // 2
    inv = ROPE_THETA ** (-jnp.arange(0, ROT_DIM, 2, dtype=f32) / ROT_DIM)
    ang = pos.astype(f32)[:, None] * inv[None, :]
    cos, sin = jnp.cos(ang), jnp.sin(ang)
    t = pos.shape[0]
    rest = DA_DK - ROT_DIM
    c64 = jnp.concatenate([cos, cos, jnp.ones((t, rest), f32)], axis=1)
    sa64 = jnp.concatenate([-sin, jnp.zeros((t, half + rest), f32)], axis=1)
    sb64 = jnp.concatenate([jnp.zeros((t, half), f32), sin, jnp.zeros((t, rest), f32)], axis=1)
    return tuple(jnp.tile(a, (1, HEAD_W // DA_DK)) for a in (c64, sa64, sb64))


def _hgrn_chunk(q, k, v, lf, st, chunk, consts):
    sel, pair_masks, halves = consts
    heads = [slice(hh * HEAD_W, (hh + 1) * HEAD_W) for hh in range(N_HEADS)]
    terms = jnp.concatenate(_split3(lf), axis=1)
    gs3 = _dot(sel, terms)
    gs = gs3[:, :SEG_W] + gs3[:, SEG_W:2 * SEG_W] + gs3[:, 2 * SEG_W:]
    g = gs[:chunk]
    g_last = g[chunk - 1:chunk, :]
    qx = (q * jnp.exp2(g)).astype(bf16)
    q_bf, k_bf, v_bf = q.astype(bf16), k.astype(bf16), v.astype(bf16)
    st_bf = [s.astype(bf16) for s in st]
    o = [_dot_nt(qx[:, sl], st_bf[hh]) for hh, sl in enumerate(heads)]
    a = [_dot_nt(q_bf[:, sl], k_bf[:, sl]) * pair_masks[0] for sl in heads]
    n_mm = 1
    for lvl, h in enumerate(halves, start=1):
        if h % 8 == 0:
            ref = jnp.concatenate([jnp.broadcast_to(g[r0 + h - 1:r0 + h, :], (2 * h, SEG_W))
                                   for r0 in range(0, chunk, 2 * h)], axis=0)
        else:
            ref = gs[n_mm * chunk:(n_mm + 1) * chunk]
            n_mm += 1
        e = jnp.exp2(-jnp.abs(g - ref))
        qe, ke = (q * e).astype(bf16), (k * e).astype(bf16)
        a = [a[hh] + _dot_nt(qe[:, sl], ke[:, sl]) * pair_masks[lvl] for hh, sl in enumerate(heads)]
    kdec = (k * jnp.exp2(g_last - g)).astype(bf16)
    decay = jnp.exp2(g_last)
    o = jnp.concatenate([o[hh] + _dot(a[hh].astype(bf16), v_bf[:, sl]) for hh, sl in enumerate(heads)], axis=1)
    st_new = [st[hh] * decay[:, sl] + _dot_tn(v_bf[:, sl], kdec[:, sl]) for hh, sl in enumerate(heads)]
    return o, st_new


def _hgrn_consts(chunk):
    r = lax.broadcasted_iota(jnp.int32, (chunk, chunk), 0)
    c = lax.broadcasted_iota(jnp.int32, (chunk, chunk), 1)
    sels = [jnp.where(c <= r, 1.0, 0.0)]
    masks = [jnp.where(c == r, 1.0, 0.0)]
    halves = []
    h = chunk // 2
    while h >= 1:
        blk = -(2 * h)
        if h % 8:
            sels.append(jnp.where(c <= (r & blk) + (h - 1), 1.0, 0.0))
        same_block = (r & blk) == (c & blk)
        masks.append(jnp.where(same_block & ((r & h) != 0) & ((c & h) == 0), 1.0, 0.0))
        halves.append(h)
        h //= 2
    return jnp.concatenate(sels, axis=0).astype(bf16), masks, halves


def _hgrn_kernel(q_ref, k_ref, v_ref, lf_ref, s0_ref, *rest, chunk, n_chunks, layer, first):
    o_ref, sfin_ref, st_ref = rest[-3:]
    j = pl.program_id(1)
    consts = _hgrn_consts(chunk)
    if first:
        sfin_all, sfin_ref = sfin_ref, sfin_ref.at[layer]

    @pl.when(j == 0)
    def _():
        for hh in range(N_HEADS):
            st_ref[hh] = s0_ref[hh].T

    def body(c, carry):
        rows = pl.ds(pl.multiple_of(c * chunk, chunk), chunk)
        o, st_new = _hgrn_chunk(q_ref[rows, :], k_ref[rows, :], v_ref[rows, :], lf_ref[rows, :],
                                [st_ref[hh] for hh in range(N_HEADS)], chunk, consts)
        o_ref[rows, :] = o
        for hh in range(N_HEADS):
            st_ref[hh] = st_new[hh]
        return carry

    lax.fori_loop(0, n_chunks, body, 0, unroll=4 if n_chunks % 4 == 0 else 1)

    @pl.when(j == pl.num_programs(1) - 1)
    def _():
        for hh in range(N_HEADS):
            sfin_ref[hh] = st_ref[hh].T
        if first:
            for d in range(sfin_all.shape[0]):
                if d != layer:
                    sfin_all[d] = jnp.zeros(sfin_all.shape[1:], f32)


def _hgrn(q, k, v, lf, s0, s0_layer, t, tb, chunk, depth, layer, sfin_prev):
    n = q.shape[0]
    b = n // t
    nj = t // tb
    first = sfin_prev is None
    tok = pl.BlockSpec((tb, SEG_W), lambda bi, j: (bi * nj + j, 0))
    st_in = pl.BlockSpec((None, None, N_HEADS, HEAD_W, HEAD_W), lambda bi, j: (s0_layer, bi, 0, 0, 0))
    if first:
        st_out = pl.BlockSpec((depth, None, N_HEADS, HEAD_W, HEAD_W), lambda bi, j: (0, bi, 0, 0, 0))
        extra_specs, extra_args, aliases = [], (), {}
    else:
        st_out = pl.BlockSpec((None, None, N_HEADS, HEAD_W, HEAD_W), lambda bi, j: (layer, bi, 0, 0, 0))
        extra_specs, extra_args, aliases = [pl.BlockSpec(memory_space=pl.ANY)], (sfin_prev,), {5: 1}
    return pl.pallas_call(
        functools.partial(_hgrn_kernel, chunk=chunk, n_chunks=tb // chunk, layer=layer, first=first),
        grid=(b, nj),
        in_specs=[tok, tok, tok, tok, st_in] + extra_specs,
        out_specs=[tok, st_out],
        out_shape=[jax.ShapeDtypeStruct((n, SEG_W), f32),
                   jax.ShapeDtypeStruct((depth, b, N_HEADS, HEAD_W, HEAD_W), f32)],
        input_output_aliases=aliases,
        scratch_shapes=[pltpu.VMEM((N_HEADS, HEAD_W, HEAD_W), f32)],
        compiler_params=pltpu.CompilerParams(dimension_semantics=("arbitrary", "arbitrary"),
                                             vmem_limit_bytes=VMEM_LIMIT),
        name="hgrn",
    )(q, k, v, lf, s0, *extra_args)


def _stack_maps(q):
    lane = lax.broadcasted_iota(jnp.int32, q.shape, 1)
    zero = jnp.zeros_like(q)
    return jnp.concatenate([jnp.where(lane < DA_DK, q, zero), jnp.where(lane >= DA_DK, q, zero)], axis=0)


def _softmax_step(s, v_bf, m_ref, l_ref, acc_ref, rows=None):
    sl = slice(None) if rows is None else rows
    m_prev = m_ref[sl, :]
    m_new = jnp.maximum(m_prev, jnp.max(s, axis=-1, keepdims=True))
    alpha = jnp.exp2(m_prev - m_new)
    p = jnp.exp2(s - m_new)
    l_ref[sl, :] = alpha * l_ref[sl, :] + jnp.sum(p, axis=-1, keepdims=True)
    acc_ref[sl, :] = alpha * acc_ref[sl, :] + _dot(p.astype(bf16), v_bf)
    m_ref[sl, :] = m_new


def _attn_prompt_step(i, lam_ref, qt_ref, k_ref, vt_ref, o_ref, m_ref, l_ref, acc_ref, s_ref, tq, n_q):
    qt = qt_ref[...]
    sub = lax.broadcasted_iota(jnp.int32, qt.shape, 0)
    zero = jnp.zeros_like(qt)
    qs = jnp.concatenate([jnp.where(sub < DA_DK, qt, zero), jnp.where(sub >= DA_DK, qt, zero)], axis=1)
    m_ref[...] = jnp.full(m_ref.shape, NEG, f32)
    l_ref[...] = jnp.zeros(l_ref.shape, f32)
    acc_ref[...] = jnp.zeros(acc_ref.shape, f32)

    def scores(j):
        return _dot(k_ref[j * tq:(j + 1) * tq, :], qs)

    def update(s, vt):
        m_prev = m_ref[...]
        m_new = jnp.maximum(m_prev, jnp.max(s, axis=0, keepdims=True))
        alpha = jnp.exp2(m_prev - m_new)
        p = jnp.exp2(s - m_new)
        l_ref[...] = alpha * l_ref[...] + jnp.sum(p, axis=0, keepdims=True)
        acc_ref[...] = alpha * acc_ref[...] + _dot(vt, p.astype(bf16))
        m_ref[...] = m_new

    s_ref[0] = scores(0)
    for j in range(n_q - 1):
        @pl.when(j < i)
        def _():
            s_ref[(j + 1) % 2] = scores(j + 1)
            update(s_ref[j % 2], vt_ref[:, j * tq:(j + 1) * tq])

    s = s_ref[i % 2]
    key = lax.broadcasted_iota(jnp.int32, s.shape, 0)
    qry = lax.broadcasted_iota(jnp.int32, s.shape, 1) & (tq - 1)
    update(jnp.where(key <= qry, s, NEG), vt_ref[:, pl.ds(pl.multiple_of(i * tq, tq), tq)])
    o = acc_ref[...] / l_ref[...]
    o_ref[...] = (o[:, :tq] - lam_ref[0] * o[:, tq:]).T


def _attn_sample_step(j, n_j, lam_ref, qs_ref, kn_ref, vn_ref, k_refs, v_refs, o_ref, m_ref, l_ref, acc_ref):
    t = SAMPLE_PAD

    @pl.when(j == 0)
    def _():
        m_ref[...] = jnp.full(m_ref.shape, NEG, f32)
        l_ref[...] = jnp.zeros(l_ref.shape, f32)
        acc_ref[...] = jnp.zeros(acc_ref.shape, f32)

    head_rows = [slice(hh * 2 * t, (hh + 1) * 2 * t) for hh in range(N_HEADS)]
    head_toks = [pl.ds(hh, PAGE_SIZE, stride=N_HEADS) for hh in range(N_HEADS)]
    qs = qs_ref[...]
    s = [_dot_nt(qs[rows, :], jnp.concatenate([kr[toks, :].astype(bf16) for kr in k_refs], axis=0))
         for rows, toks in zip(head_rows, head_toks)]
    m_prev = m_ref[...]
    m_new = jnp.maximum(m_prev, jnp.concatenate([jnp.max(sh, axis=-1, keepdims=True) for sh in s], axis=0))
    alpha = jnp.exp2(m_prev - m_new)
    p = [jnp.exp2(sh - m_new[rows, :]) for sh, rows in zip(s, head_rows)]
    l_ref[...] = alpha * l_ref[...] + jnp.concatenate([jnp.sum(ph, axis=-1, keepdims=True) for ph in p], axis=0)
    pv = [_dot(ph.astype(bf16), jnp.concatenate([vr[toks, :].astype(bf16) for vr in v_refs], axis=0))
          for ph, toks in zip(p, head_toks)]
    acc_ref[...] = alpha * acc_ref[...] + jnp.concatenate(pv, axis=0)
    m_ref[...] = m_new

    @pl.when(j == n_j - 1)
    def _():
        for hh in range(N_HEADS):
            rows = slice(hh * 2 * t, (hh + 1) * 2 * t)
            sl = slice(hh * HEAD_W, (hh + 1) * HEAD_W)
            s = _dot_nt(qs_ref[rows, :], kn_ref[:, sl].astype(bf16))
            row = lax.broadcasted_iota(jnp.int32, s.shape, 0) & (t - 1)
            col = lax.broadcasted_iota(jnp.int32, s.shape, 1)
            s = jnp.where(col <= row, s, NEG)
            _softmax_step(s, vn_ref[:, sl].astype(bf16), m_ref, l_ref, acc_ref, rows)
            o = acc_ref[rows, :] / l_ref[rows, :]
            o_ref[:, sl] = o[:t] - lam_ref[0] * o[t:]


def _attn_kernel(pt_ref, lam_ref, qt_ref, k_ref, vt_ref, qs_ref, kn_ref, vn_ref, *rest,
                 tq, n_q, n_pg, n_j, steps_p, steps_s):
    del pt_ref
    k_pages, v_pages = rest[:n_pg], rest[n_pg:2 * n_pg]
    op_ref, os_ref, pm_ref, pl_ref, pacc_ref, ps_ref, sm_ref, sl_ref, sacc_ref = rest[2 * n_pg:]
    step = pl.program_id(0)

    def prompt():
        _attn_prompt_step(step % n_q, lam_ref, qt_ref, k_ref, vt_ref, op_ref, pm_ref, pl_ref, pacc_ref, ps_ref,
                          tq, n_q)

    def sample():
        _attn_sample_step(step % n_j, n_j, lam_ref, qs_ref, kn_ref, vn_ref, k_pages, v_pages, os_ref,
                          sm_ref, sl_ref, sacc_ref)

    if steps_p == steps_s:
        prompt()
        sample()
    else:
        pl.when(step < steps_p)(prompt)
        pl.when(step < steps_s)(sample)


def _attn(page_table, lam, qt_bf, k_bf, vt_bf, qs, kn, vn, cache_k, cache_v, layer, b, s, tq, n_pg):
    bd, n_pages = page_table.shape
    t = SAMPLE_PAD
    n_q, n_j = s // tq, n_pages // n_pg
    steps_p, steps_s = b * N_HEADS * n_q, bd * n_j
    k3 = k_bf.reshape(b, s, SEG_W)
    cache_k = cache_k.reshape(cache_k.shape[:2] + (PAGE_SIZE * N_HEADS, HEAD_W))
    cache_v = cache_v.reshape(cache_v.shape[:2] + (PAGE_SIZE * N_HEADS, HEAD_W))

    def p_idx(step):
        lin = jnp.minimum(step, steps_p - 1)
        return lin // (N_HEADS * n_q), (lin // n_q) % N_HEADS, lin % n_q

    def s_idx(step):
        lin = jnp.minimum(step, steps_s - 1)
        return lin // n_j, lin % n_j

    def on_p(f):
        return lambda step, pt: f(*p_idx(step))

    def per_seq(shape):
        return pl.BlockSpec((None,) + shape, lambda step, pt: (s_idx(step)[0], 0, 0))

    def page_spec(p):
        def index(step, pt):
            sb, j = s_idx(step)
            return layer, pt[sb * n_pages + j * n_pg + p], 0, 0
        return pl.BlockSpec((None, None, PAGE_SIZE * N_HEADS, HEAD_W), index)

    pages = [page_spec(p) for p in range(n_pg)]
    out_p, out_s = pl.pallas_call(
        functools.partial(_attn_kernel, tq=tq, n_q=n_q, n_pg=n_pg, n_j=n_j, steps_p=steps_p, steps_s=steps_s),
        grid_spec=pltpu.PrefetchScalarGridSpec(
            num_scalar_prefetch=1,
            grid=(max(steps_p, steps_s),),
            in_specs=[pl.BlockSpec(memory_space=pltpu.SMEM),
                      pl.BlockSpec((None, HEAD_W, tq), on_p(lambda bi, h, i: (bi, h, i))),
                      pl.BlockSpec((None, s, HEAD_W), on_p(lambda bi, h, i: (bi, 0, h))),
                      pl.BlockSpec((None, HEAD_W, s), on_p(lambda bi, h, i: (bi, h, 0))),
                      per_seq((2 * t * N_HEADS, HEAD_W)), per_seq((t, SEG_W)), per_seq((t, SEG_W))] + pages + pages,
            out_specs=[pl.BlockSpec((None, tq, HEAD_W), on_p(lambda bi, h, i: (bi, i, h))), per_seq((t, SEG_W))],
            scratch_shapes=[pltpu.VMEM((1, 2 * tq), f32), pltpu.VMEM((1, 2 * tq), f32),
                            pltpu.VMEM((HEAD_W, 2 * tq), f32), pltpu.VMEM((2, tq, 2 * tq), f32),
                            pltpu.VMEM((2 * t * N_HEADS, 1), f32), pltpu.VMEM((2 * t * N_HEADS, 1), f32),
                            pltpu.VMEM((2 * t * N_HEADS, HEAD_W), f32)]),
        out_shape=[jax.ShapeDtypeStruct((b, s, SEG_W), f32), jax.ShapeDtypeStruct((bd, t, SEG_W), f32)],
        compiler_params=pltpu.CompilerParams(dimension_semantics=("arbitrary",), vmem_limit_bytes=VMEM_LIMIT),
        name="attn",
    )(page_table.reshape(-1), lam, qt_bf, k3, vt_bf, qs, kn, vn, *([cache_k] * n_pg), *([cache_v] * n_pg))
    return out_p.reshape(b * s, SEG_W), out_s


def _route(logits):
    lane = lax.broadcasted_iota(jnp.int32, logits.shape, 1).astype(f32)
    big = float(ROUTER_W)
    is_g = lane < N_GROUPS
    m1 = jnp.max(jnp.where(is_g, logits, -jnp.inf), axis=-1, keepdims=True)
    grp = jnp.min(jnp.where(is_g & (logits == m1), lane, big), axis=-1, keepdims=True)
    p_grp = 1.0 / jnp.sum(jnp.where(is_g, jnp.exp(logits - m1), 0.0), axis=-1, keepdims=True)
    lo = GATE_LANE0 + EXPERTS_PER_GROUP * grp
    in_g = (lane >= lo) & (lane < lo + EXPERTS_PER_GROUP)
    v1 = jnp.max(jnp.where(in_g, logits, -jnp.inf), axis=-1, keepdims=True)
    i1 = jnp.min(jnp.where(in_g & (logits == v1), lane, big), axis=-1, keepdims=True)
    rest = in_g & (lane != i1)
    v2 = jnp.max(jnp.where(rest, logits, -jnp.inf), axis=-1, keepdims=True)
    i2 = jnp.min(jnp.where(rest & (logits == v2), lane, big), axis=-1, keepdims=True)
    e = jnp.exp(v2 - v1)
    w1 = 1.0 / (1.0 + e)
    w2 = e / (1.0 + e)
    gates = jnp.where(lane == i1, p_grp * w1, 0.0) + jnp.where(lane == i2, p_grp * w2, 0.0)
    a = jnp.minimum(i1, i2) - lo
    b = jnp.maximum(i1, i2) - lo
    pair = jnp.where(a == 0.0, b - 1.0, jnp.where(a == 1.0, b + 1.0, 5.0))
    return jnp.where(lane == 0.0, grp, jnp.where(lane == 1.0, grp * float(len(PAIR_LO)) + pair, gates))


def _mix_kernel(oh_ref, od_ref, gh_ref, h_ref, hnw_ref, dnw_ref, wo_ref, fnw_ref, wrh_ref, wrl_ref, br_ref,
                hx_ref, rt_ref, *, od_scale):
    parts = []
    for hh in range(N_HEADS):
        sl = slice(hh * HEAD_W, (hh + 1) * HEAD_W)
        gate = 1.0 / (1.0 + jnp.exp(-gh_ref[:, sl]))
        parts.append((_rms(oh_ref[:, sl], hnw_ref[...]) * gate).astype(bf16))
    for hh in range(N_HEADS):
        sl = slice(hh * HEAD_W, (hh + 1) * HEAD_W)
        parts.append((_rms(od_ref[:, sl], dnw_ref[...]) * od_scale).astype(bf16))
    h2 = h_ref[...] + _dot(jnp.concatenate(parts, axis=1), wo_ref[...])
    xn = _rms(h2, fnw_ref[...])
    x_hi = xn.astype(bf16)
    x_lo = (xn - x_hi.astype(f32)).astype(bf16)
    logits = _dot(x_hi, wrh_ref[...]) + _dot(x_lo, wrh_ref[...]) + _dot(x_hi, wrl_ref[...]) + br_ref[...]
    tm = h2.shape[0]
    for c in range(D_MODEL // HEAD_W):
        hx_ref[pl.ds(c, tm, stride=ROW_CHUNKS), :] = h2[:, c * HEAD_W:(c + 1) * HEAD_W]
    router = _route(logits)
    hx_ref[pl.ds(D_MODEL // HEAD_W, tm, stride=ROW_CHUNKS), :] = router
    rt_ref[...] = router


def _mix(o_h, o_d, g_h, h, hnw, dnw, wo_bf, fnw, wr_hi, wr_lo, br, od_scale, tm):
    n = o_h.shape[0]
    row = lambda i: (i, 0)
    fixed = lambda i: (0, 0)
    seg = pl.BlockSpec((tm, SEG_W), row)
    return pl.pallas_call(
        functools.partial(_mix_kernel, od_scale=od_scale),
        grid=(n // tm,),
        in_specs=[seg, seg, seg, pl.BlockSpec((tm, D_MODEL), row),
                  pl.BlockSpec((1, HEAD_W), fixed), pl.BlockSpec((1, HEAD_W), fixed),
                  pl.BlockSpec((D_MODEL, D_MODEL), fixed), pl.BlockSpec((1, D_MODEL), fixed),
                  pl.BlockSpec((D_MODEL, ROUTER_W), fixed), pl.BlockSpec((D_MODEL, ROUTER_W), fixed),
                  pl.BlockSpec((1, ROUTER_W), fixed)],
        out_specs=[pl.BlockSpec((tm * ROW_CHUNKS, HEAD_W), row), pl.BlockSpec((tm, ROUTER_W), row)],
        out_shape=[jax.ShapeDtypeStruct((n * ROW_CHUNKS, HEAD_W), f32), jax.ShapeDtypeStruct((n, ROUTER_W), f32)],
        compiler_params=pltpu.CompilerParams(dimension_semantics=("arbitrary",), vmem_limit_bytes=VMEM_LIMIT),
        name="mix",
    )(o_h, o_d, g_h, h, hnw, dnw, wo_bf, fnw, wr_hi, wr_lo, br)


def _moe_kernel(pos_ref, exp_ref, cnt_ref, hx_hbm, fnw_ref, onw_ref, *rest, tm, n_e, final_norm):
    wg_refs, wu_refs, wd_refs = rest[:n_e], rest[n_e:2 * n_e], rest[2 * n_e:3 * n_e]
    out_hbm, xbuf, ybuf, ids_ref, gsem, ssem = rest[3 * n_e:]
    i = pl.program_id(0)
    n_tiles = pl.num_programs(0)
    slot = i % 2

    def gather_copy(tile, sl, r):
        tok = ids_ref[tile * tm + r]
        return pltpu.make_async_copy(hx_hbm.at[pl.ds(tok * ROW_CHUNKS, ROW_CHUNKS)],
                                     xbuf.at[sl, pl.ds(r * ROW_CHUNKS, ROW_CHUNKS)], gsem.at[sl])

    def scatter_copy(tile, sl, r):
        tok = ids_ref[tile * tm + r]
        return pltpu.make_async_copy(ybuf.at[sl, pl.ds(r, 1)], out_hbm.at[pl.ds(tok, 1)], ssem.at[sl])

    @pl.when(i == 0)
    def _():
        def place(t, carry):
            ids_ref[pos_ref[t]] = t
            return carry
        lax.fori_loop(0, pos_ref.shape[0], place, 0, unroll=8)

    def start_rows(n, copy):
        def pair(k, carry):
            copy(2 * k).start(priority=0)
            copy(2 * k + 1).start(priority=1)
            return carry
        if isinstance(n, int):
            assert n % 2 == 0
            lax.fori_loop(0, n // 2, pair, 0, unroll=4)
        else:
            lax.fori_loop(0, n // 2, pair, 0)

            @pl.when(n % 2 == 1)
            def _():
                copy(n - 1).start(priority=0)

    def wait_rows(n, copy):
        def body(r, carry):
            copy(r).wait()
            return carry
        lax.fori_loop(0, n, body, 0)

    def start_tile(tile, copy):
        cnt = cnt_ref[tile]

        @pl.when(cnt == tm)
        def _():
            start_rows(tm, copy)

        @pl.when(cnt < tm)
        def _():
            start_rows(cnt, copy)

    def wait_tile(tile, copy, whole):
        cnt = cnt_ref[tile]

        @pl.when(cnt == tm)
        def _():
            whole.wait()

        @pl.when(cnt < tm)
        def _():
            wait_rows(cnt, copy)

    def gather_start(tile, sl):
        start_tile(tile, lambda r: gather_copy(tile, sl, r))

    def gather_wait(tile, sl):
        wait_tile(tile, lambda r: gather_copy(tile, sl, r),
                  pltpu.make_async_copy(hx_hbm.at[pl.ds(0, tm * ROW_CHUNKS)], xbuf.at[sl], gsem.at[sl]))

    def scatter_start(tile, sl):
        start_tile(tile, lambda r: scatter_copy(tile, sl, r))

    def scatter_wait(tile, sl):
        wait_tile(tile, lambda r: scatter_copy(tile, sl, r),
                  pltpu.make_async_copy(ybuf.at[sl], out_hbm.at[pl.ds(0, tm)], ssem.at[sl]))

    @pl.when(i == 0)
    def _():
        xbuf[...] = jnp.zeros(xbuf.shape, f32)
        gather_start(0, 0)

    @pl.when(i + 1 < n_tiles)
    def _():
        gather_start(i + 1, 1 - slot)

    gather_wait(i, slot)

    @pl.when(i >= 2)
    def _():
        scatter_wait(i - 2, slot)

    @pl.when(cnt_ref[i] > 0)
    def _():
        xs = xbuf.at[slot]
        chunk = lambda c: xs[pl.ds(c, tm, stride=ROW_CHUNKS), :]
        h2 = jnp.concatenate([chunk(c) for c in range(D_MODEL // HEAD_W)], axis=1)
        router = chunk(D_MODEL // HEAD_W)
        xn = _rms(h2, fnw_ref[...]).astype(bf16)
        lane = lax.broadcasted_iota(jnp.int32, router.shape, 1)
        acc = h2
        for e in range(n_e):
            gate_lane = GATE_LANE0 + exp_ref[i * n_e + e]
            gate = jnp.sum(jnp.where(lane == gate_lane, router, 0.0), axis=-1, keepdims=True)
            a = _dot(xn, wg_refs[e][...])
            hid = a / (1.0 + jnp.exp(-a)) * _dot(xn, wu_refs[e][...])
            acc = acc + _dot((hid * gate).astype(bf16), wd_refs[e][...])
        if final_norm:
            acc = _rms(acc, onw_ref[...])
        ybuf[slot] = acc

    scatter_start(i, slot)

    @pl.when(i == n_tiles - 1)
    def _():
        @pl.when(i >= 1)
        def _():
            scatter_wait(i - 1, 1 - slot)
        scatter_wait(i, slot)


def _dispatch(cls_f32, n_cls, tm):
    n = cls_f32.shape[0]
    n_tiles = n // tm + n_cls
    grp = cls_f32.astype(jnp.int32)
    onehot = (grp[:, None] == jnp.arange(n_cls, dtype=jnp.int32)[None, :]).astype(jnp.int32)
    csum = jnp.cumsum(onehot, axis=0)
    cnt = csum[-1]
    rank = jnp.sum(csum * onehot, axis=1) - 1
    tiles_g = (cnt + tm - 1) // tm
    tile_end = jnp.cumsum(tiles_g)
    tile_start = tile_end - tiles_g
    pos = jnp.sum(onehot * (tile_start * tm)[None, :], axis=1) + rank
    tile = jnp.arange(n_tiles, dtype=jnp.int32)
    tile_grp_raw = jnp.sum((tile[:, None] >= tile_end[None, :]).astype(jnp.int32), axis=1)
    last_grp = jnp.max(jnp.where(cnt > 0, jnp.arange(n_cls, dtype=jnp.int32), 0))
    tile_grp = jnp.minimum(tile_grp_raw, last_grp)
    in_range = tile_grp_raw < n_cls
    rows_left = cnt[tile_grp] - (tile - tile_start[tile_grp]) * tm
    tile_cnt = jnp.where(in_range, jnp.clip(rows_left, 0, tm), 0).astype(jnp.int32)
    return pos.astype(jnp.int32), tile_grp, tile_cnt


def _moe(hx, router, fnw, onw, wg_bf, wu_bf, wd_bf, layer, tm, final_norm, by_pair):
    n = router.shape[0]
    if by_pair:
        n_e = 2
        pos, tile_cls, tile_cnt = _dispatch(router[:, 1], N_GROUPS * len(PAIR_LO), tm)
        grp, pair = tile_cls // len(PAIR_LO), tile_cls % len(PAIR_LO)
        members = [jnp.array(PAIR_LO, jnp.int32)[pair], jnp.array(PAIR_HI, jnp.int32)[pair]]
    else:
        n_e = EXPERTS_PER_GROUP
        pos, grp, tile_cnt = _dispatch(router[:, 0], N_GROUPS, tm)
        members = [jnp.full_like(grp, e) for e in range(n_e)]
    tile_exp = jnp.stack([grp * EXPERTS_PER_GROUP + m for m in members], axis=1).reshape(-1).astype(jnp.int32)
    n_tiles = tile_cnt.shape[0]
    fixed = lambda i, ids_r, exp_r, cnt_r: (0, 0)

    def wspecs(shape):
        return [pl.BlockSpec((None, None) + shape, lambda i, ids_r, exp_r, cnt_r, e=e: (layer, exp_r[i * n_e + e], 0, 0))
                for e in range(n_e)]

    return pl.pallas_call(
        functools.partial(_moe_kernel, tm=tm, n_e=n_e, final_norm=final_norm),
        grid_spec=pltpu.PrefetchScalarGridSpec(
            num_scalar_prefetch=3,
            grid=(n_tiles,),
            in_specs=[pl.BlockSpec(memory_space=pl.ANY),
                      pl.BlockSpec((1, D_MODEL), fixed), pl.BlockSpec((1, D_MODEL), fixed)]
                     + wspecs((D_MODEL, EXPERT_FF)) + wspecs((D_MODEL, EXPERT_FF)) + wspecs((EXPERT_FF, D_MODEL)),
            out_specs=pl.BlockSpec(memory_space=pl.ANY),
            scratch_shapes=[pltpu.VMEM((2, tm * ROW_CHUNKS, HEAD_W), f32), pltpu.VMEM((2, tm, D_MODEL), f32),
                            pltpu.SMEM((n_tiles * tm,), jnp.int32),
                            pltpu.SemaphoreType.DMA((2,)), pltpu.SemaphoreType.DMA((2,))]),
        out_shape=jax.ShapeDtypeStruct((n, D_MODEL), f32),
        compiler_params=pltpu.CompilerParams(dimension_semantics=("arbitrary",), vmem_limit_bytes=VMEM_LIMIT),
        name="moe",
    )(pos, tile_exp, tile_cnt, hx, fnw, onw, *([wg_bf] * n_e), *([wu_bf] * n_e), *([wd_bf] * n_e))


def _stacked_sample_queries(qd):
    bd, t = qd.shape[:2]
    q5 = qd.reshape(bd, t, N_HEADS, 2, DA_DK)
    eye = jnp.eye(2, dtype=qd.dtype)
    qz = q5[:, :, :, :, None, :] * eye[None, None, None, :, :, None]
    return qz.transpose(0, 2, 3, 1, 4, 5).reshape(bd, N_HEADS * 2 * t, HEAD_W)


def kernel(x_prompt, x_sample, cache_k, cache_v, state_hgrn, page_table, attn_norm_w, w_in, hgrn_lb, hgrn_norm_w,
           diff_lambda, diff_norm_w, w_o, ffn_norm_w, w_r1, b_r1, w_r2, b_r2, w_gate, w_up, w_down, final_norm_w):
    b, s = x_prompt.shape[:2]
    bd, t = x_sample.shape[:2]
    depth = w_in.shape[0]
    n_pages = page_table.shape[1]
    past_len = n_pages * PAGE_SIZE
    assert x_prompt.shape[2] == D_MODEL and w_in.shape[2] == N_SEG * SEG_W and t <= SAMPLE_PAD
    tp = SAMPLE_PAD

    tabs_p = _rope_tables(jnp.arange(s, dtype=jnp.int32))
    tabs_s = _rope_tables(jnp.tile(past_len + jnp.arange(t, dtype=jnp.int32), bd))
    p_lb = jax.nn.softmax(hgrn_lb.astype(f32), axis=0)
    lb_all = jnp.cumsum(p_lb, axis=0) - p_lb[0:1]

    w_in_bf = w_in.astype(bf16)
    w_o_bf = w_o.astype(bf16)
    wg_bf, wu_bf, wd_bf = w_gate.astype(bf16), w_up.astype(bf16), w_down.astype(bf16)
    pad_r = ROUTER_W - N_GROUPS - N_EXPERTS
    w_r = jnp.concatenate([w_r1, w_r2, jnp.zeros((depth, D_MODEL, pad_r), f32)], axis=2)
    w_r_hi = w_r.astype(bf16)
    w_r_lo = (w_r - w_r_hi.astype(f32)).astype(bf16)
    b_r = jnp.concatenate([b_r1, b_r2, jnp.zeros((depth, pad_r), f32)], axis=1)

    tm_p = 512 if (b * s) % 512 == 0 else b * s
    tq = 512 if s % 512 == 0 else s
    tb_p = 1024 if s % 1024 == 0 else s
    tm_moe_p = 256 if (b * s) % 256 == 0 else b * s
    n_s = bd * t
    n_pg = next(p for p in (16, 8, 4, 2, 1) if n_pages % p == 0)
    tm_proj = 512 if s % 512 == 0 else tm_p
    q_scale = DA_DK ** -0.5 * LOG2E

    hp = x_prompt.reshape(b * s, D_MODEL)
    hs = x_sample.reshape(n_s, D_MODEL)
    zeros_state = jnp.zeros((1, b, N_HEADS, HEAD_W, HEAD_W), f32)
    state_s = state_hgrn.astype(f32)
    sfin_p = sfin_s = None
    kv_p = kv_s = None
    for l in range(depth):
        lam_init = 0.8 - 0.6 * math.exp(-0.3 * l)
        dl = diff_lambda[l].astype(f32)
        lam = (jnp.exp(jnp.sum(dl[0] * dl[1])) - jnp.exp(jnp.sum(dl[2] * dl[3])) + lam_init).reshape(1)
        lb = lb_all[l].reshape(1, SEG_W)
        nw = attn_norm_w[l].reshape(1, D_MODEL)
        fnw = ffn_norm_w[l].reshape(1, D_MODEL)
        onw = final_norm_w.reshape(1, D_MODEL)
        hnw = hgrn_norm_w[l].reshape(1, HEAD_W)
        dnw = diff_norm_w[l].reshape(1, HEAD_W)
        last = l == depth - 1

        def tail(o_h, o_d, g_h, h, tm_mix, tm_moe, by_pair):
            hx, router = _mix(o_h, o_d, g_h, h, hnw, dnw, w_o_bf[l], fnw, w_r_hi[l], w_r_lo[l],
                              b_r[l].reshape(1, ROUTER_W), 1.0 - lam_init, tm_mix)
            return _moe(hx, router, fnw, onw, wg_bf, wu_bf, wd_bf, l, tm_moe, last, by_pair)

        qh, kh, ih, lf, gh_p, qd_p, kd, vd, kb_p, vb_p = _proj(hp, b * s, nw, w_in_bf[l], lb, tabs_p, tm_proj,
                                                               q_scale, depth, l, kv_p, seq=s)
        kv_p = (kd, vd)
        oh_p, sfin_p = _hgrn(qh, kh, ih, lf, zeros_state, 0, s, tb_p, HG_CHUNK, depth, l, sfin_p)

        qh, kh, ih, lf, gh_s, qd, kd, vd, kb, vb = _proj(hs, n_s, nw, w_in_bf[l], lb, tabs_s, n_s, q_scale,
                                                         depth, l, kv_s)
        kv_s = (kd, vd)
        pad = lambda a: jnp.pad(a.reshape(bd, t, SEG_W), ((0, 0), (0, tp - t), (0, 0)))
        flat = lambda a: pad(a).reshape(bd * tp, SEG_W)
        oh_s, sfin_s = _hgrn(flat(qh), flat(kh), flat(ih), flat(lf), state_s, l, tp, tp, tp, depth, l, sfin_s)
        oh_s = oh_s.reshape(bd, tp, SEG_W)[:, :t].reshape(n_s, SEG_W)

        od_p, od_s = _attn(page_table, lam, qd_p, kb_p, vb_p, _stacked_sample_queries(pad(qd)), pad(kb), pad(vb),
                           cache_k, cache_v, l, b, s, tq, n_pg)
        hp = tail(oh_p, od_p, gh_p, hp, tm_p, tm_moe_p, True)
        hs = tail(oh_s, od_s[:, :t].reshape(n_s, SEG_W), gh_s, hs, n_s, n_s, False)

    rows_p = lambda a: a.reshape(depth, b, s, N_HEADS, HEAD_W)
    rows_s = lambda a: a.reshape(depth, bd, t, N_HEADS, HEAD_W)
    return (hp.reshape(b, s, D_MODEL), hs.reshape(bd, t, D_MODEL), rows_p(kv_p[0]), rows_p(kv_p[1]),
            sfin_p, rows_s(kv_s[0]), rows_s(kv_s[1]), sfin_s)
```
